```python
import math
import jax, jax.numpy as jnp
from jax import lax
import numpy as np

D_MODEL = 1024
BATCH = 2
SEQ = 8192
DEPTH = 1

HEAD_DIM = 64
NSA_HEADS = 8
NSA_GROUPS = 2
NSA_HPG = NSA_HEADS // NSA_GROUPS
CMP_LEN = 32
CMP_STRIDE = 16
CMP_HIDDEN = 256
SEL_BLOCK = 64
SEL_TOPN = 16
WINDOW = 512
SB_HEADS = 8
Q_BLOCK = 128
ROPE_THETA = 10000.0
PEER_HEADS = 8
PEER_NKEYS = 128
PEER_N_EXPERTS = PEER_NKEYS * PEER_NKEYS
PEER_QDIM = 256
PEER_TOPK = 16
PEER_CHUNK = 128
EPS = 1e-6
NEG = -1e30

NSA_W = NSA_HEADS * HEAD_DIM
KV_W = NSA_GROUPS * HEAD_DIM
SB_W = SB_HEADS * HEAD_DIM
IN_WIDTH = NSA_W + 6 * KV_W + 3 * NSA_HEADS + 3 * SB_W + 2 * D_MODEL

kernel_name = "hybrid_nsa_stickbreaking_peer_block"


def rms_norm(x, g):
    xf = x.astype(jnp.float32)
    return xf * lax.rsqrt(jnp.mean(xf * xf, axis=-1, keepdims=True) + EPS) * g.astype(jnp.float32)


def rope(x, pos):
    half = HEAD_DIM // 2
    freqs = ROPE_THETA ** (-jnp.arange(half, dtype=jnp.float32) / half)
    ang = pos.astype(jnp.float32)[:, None] * freqs[None, :]
    cos, sin = jnp.cos(ang), jnp.sin(ang)
    x1, x2 = x[..., :half], x[..., half:]
    return jnp.concatenate([x1 * cos - x2 * sin, x2 * cos + x1 * sin], axis=-1)


def compress(tok, pos_emb, w1, w2):
    s = tok.shape[2]
    nc = (s - CMP_LEN) // CMP_STRIDE + 1
    idx = jnp.arange(nc)[:, None] * CMP_STRIDE + jnp.arange(CMP_LEN)[None, :]
    blocks = tok[:, :, idx] + pos_emb.astype(jnp.float32)
    flat = blocks.reshape(blocks.shape[0], blocks.shape[1], nc, CMP_LEN * HEAD_DIM)
    return jax.nn.gelu(flat @ w1) @ w2


def token_mixers(xn, w_in, q_norm_g, k_norm_g, cmp_pos_k, cmp_pos_v, cmp_k_w1, cmp_k_w2,
                 cmp_v_w1, cmp_v_w2, w_branch_nsa, w_branch_sb, w_out):
    b, s, _ = xn.shape
    scale = 1.0 / math.sqrt(HEAD_DIM)
    pos = jnp.arange(s)
    proj = xn @ w_in
    c0 = NSA_W
    c1 = c0 + 6 * KV_W
    c2 = c1 + 3 * NSA_HEADS
    c3 = c2 + 3 * SB_W
    q_n, kv_n, g_n, qkv_sb, br_g = jnp.split(proj, [c0, c1, c2, c3], axis=-1)

    q = q_n.reshape(b, s, NSA_GROUPS, NSA_HPG, HEAD_DIM).transpose(0, 2, 3, 1, 4)
    q = rope(rms_norm(q, q_norm_g), pos)
    kv = kv_n.reshape(b, s, 6, NSA_GROUPS, HEAD_DIM).transpose(2, 0, 3, 1, 4)
    k_c, v_c, k_s, v_s, k_w, v_w = kv[0], kv[1], kv[2], kv[3], kv[4], kv[5]
    nc = (s - CMP_LEN) // CMP_STRIDE + 1
    cmp_end = jnp.arange(nc) * CMP_STRIDE + CMP_LEN - 1
    kc = rope(rms_norm(compress(k_c, cmp_pos_k, cmp_k_w1, cmp_k_w2), k_norm_g[0]), cmp_end)
    vc = compress(v_c, cmp_pos_v, cmp_v_w1, cmp_v_w2)
    k_s = rope(rms_norm(k_s, k_norm_g[1]), pos)
    k_w = rope(rms_norm(k_w, k_norm_g[2]), pos)
    ns = s // SEL_BLOCK
    n_sel = min(SEL_TOPN, ns)
    ks_blocks = k_s.reshape(b, NSA_GROUPS, ns, SEL_BLOCK, HEAD_DIM)
    vs_blocks = v_s.reshape(b, NSA_GROUPS, ns, SEL_BLOCK, HEAD_DIM)
    pad = ((0, 0), (0, 0), (WINDOW, 0), (0, 0))
    kw_pad = jnp.pad(k_w, pad)
    vw_pad = jnp.pad(v_w, pad)
    g_nsa = jax.nn.sigmoid(g_n.reshape(b, s, 3, NSA_GROUPS, NSA_HPG).transpose(2, 0, 3, 4, 1))
    c_start = jnp.arange(nc) * CMP_STRIDE
    j_start = jnp.arange(ns) * SEL_BLOCK
    overlap = jnp.clip(jnp.minimum(c_start[:, None] + CMP_LEN, j_start[None, :] + SEL_BLOCK)
                       - jnp.maximum(c_start[:, None], j_start[None, :]), 0).astype(jnp.float32) / CMP_LEN
    bi = jnp.arange(b)[:, None, None, None]
    gi = jnp.arange(NSA_GROUPS)[None, :, None, None]
    blk = jnp.arange(ns)

    sb = qkv_sb.reshape(b, s, 3, SB_HEADS, HEAD_DIM).transpose(2, 0, 3, 1, 4)
    q_sb, k_sb, v_sb = sb[0], sb[1], sb[2]

    def block_fn(bidx):
        t0 = bidx * Q_BLOCK
        t = t0 + jnp.arange(Q_BLOCK)
        qb = lax.dynamic_slice_in_dim(q, t0, Q_BLOCK, axis=3)
        sc = jnp.einsum('bghtd,bgcd->bghtc', qb, kc) * scale
        cmask = cmp_end[None, :] <= t[:, None]
        has_c = (t >= CMP_LEN - 1).astype(jnp.float32)[:, None]
        pc = jax.nn.softmax(jnp.where(cmask, sc, NEG), axis=-1) * has_c
        o_cmp = jnp.einsum('bghtc,bgcd->bghtd', pc, vc)
        imp = jnp.einsum('bghtc,cj->bgtj', pc, overlap)
        cur = t // SEL_BLOCK
        forced = (blk[None, :] == 0) | (blk[None, :] == cur[:, None]) | (blk[None, :] == cur[:, None] - 1)
        valid = blk[None, :] * SEL_BLOCK <= t[:, None]
        score = jnp.where(forced, 1e4, jnp.where(valid, imp, -1e4))
        _, sel = lax.top_k(score, n_sel)
        ks_g = ks_blocks[bi, gi, sel]
        vs_g = vs_blocks[bi, gi, sel]
        ss = jnp.einsum('bghtd,bgtnld->bghtnl', qb, ks_g) * scale
        spos = sel[..., None] * SEL_BLOCK + jnp.arange(SEL_BLOCK)
        smask = spos <= t[None, None, :, None, None]
        ps = jax.nn.softmax(jnp.where(smask[:, :, None], ss, NEG), axis=(-2, -1))
        o_sel = jnp.einsum('bghtnl,bgtnld->bghtd', ps, vs_g)
        kwb = lax.dynamic_slice_in_dim(kw_pad, t0, WINDOW + Q_BLOCK, axis=2)
        vwb = lax.dynamic_slice_in_dim(vw_pad, t0, WINDOW + Q_BLOCK, axis=2)
        wpos = t0 - WINDOW + jnp.arange(WINDOW + Q_BLOCK)
        wmask = (wpos[None, :] >= 0) & (wpos[None, :] <= t[:, None]) & (wpos[None, :] > t[:, None] - WINDOW)
        sw = jnp.einsum('bghtd,bgkd->bghtk', qb, kwb) * scale
        pw = jax.nn.softmax(jnp.where(wmask, sw, NEG), axis=-1)
        o_win = jnp.einsum('bghtk,bgkd->bghtd', pw, vwb)
        gb = lax.dynamic_slice_in_dim(g_nsa, t0, Q_BLOCK, axis=4)[..., None]
        o_nsa = gb[0] * o_cmp + gb[1] * o_sel + gb[2] * o_win
        qsb = lax.dynamic_slice_in_dim(q_sb, t0, Q_BLOCK, axis=2)
        z = jnp.einsum('bhtd,bhsd->bhts', qsb, k_sb) * scale
        m = pos[None, :] < t[:, None]
        l1m = jnp.where(m, jax.nn.log_sigmoid(-z), 0.0)
        after = lax.cumsum(l1m, axis=3, reverse=True) - l1m
        a_w = jnp.where(m, jnp.exp(jax.nn.log_sigmoid(z) + after), 0.0)
        o_sb = jnp.einsum('bhts,bhsd->bhtd', a_w, v_sb)
        return o_nsa, o_sb

    o_nsa_all, o_sb_all = lax.map(block_fn, jnp.arange(s // Q_BLOCK))
    o_nsa = o_nsa_all.transpose(1, 0, 4, 2, 3, 5).reshape(b, s, NSA_W)
    o_sb = o_sb_all.transpose(1, 0, 3, 2, 4).reshape(b, s, SB_W)
    gate_a, gate_b = jnp.split(jax.nn.sigmoid(br_g), 2, axis=-1)
    merged = gate_a * (o_nsa @ w_branch_nsa) + gate_b * (o_sb @ w_branch_sb)
    return merged @ w_out


def peer(xn, wq, subkeys, u, v):
    d = xn.shape[-1]
    tokens = xn.reshape(-1, PEER_CHUNK, d)

    def chunk_fn(xc):
        q = (xc @ wq).reshape(PEER_CHUNK, PEER_HEADS, 2, PEER_QDIM // 2)
        sc = jnp.einsum('thpd,pkd->thpk', q, subkeys)
        sv, si = lax.top_k(sc, PEER_TOPK)
        cand = sv[:, :, 0, :, None] + sv[:, :, 1, None, :]
        cv, ci = lax.top_k(cand.reshape(PEER_CHUNK, PEER_HEADS, PEER_TOPK * PEER_TOPK), PEER_TOPK)
        i1 = jnp.take_along_axis(si[:, :, 0], ci // PEER_TOPK, axis=-1)
        i2 = jnp.take_along_axis(si[:, :, 1], ci % PEER_TOPK, axis=-1)
        e = i1 * PEER_NKEYS + i2
        g = jax.nn.softmax(cv.astype(jnp.float32), axis=-1)
        act = jax.nn.gelu(jnp.einsum('td,thkd->thk', xc, u[e]))
        return jnp.einsum('thk,thkd->td', g * act, v[e])

    return lax.map(chunk_fn, tokens).reshape(xn.shape)


def setup_inputs(seed: int = 0) -> dict:
    key = jax.random.key(seed)
    ks = jax.random.split(key, 20)
    f = jnp.float32
    nrm = lambda k, shape, sc: jax.random.normal(k, shape, f) * sc
    L = DEPTH
    return {
        "x": nrm(ks[0], (BATCH, SEQ, D_MODEL), 1.0),
        "norm1_g": 1.0 + nrm(ks[1], (L, D_MODEL), 0.02),
        "w_in": nrm(ks[2], (L, D_MODEL, IN_WIDTH), D_MODEL ** -0.5),
        "q_norm_g": 1.0 + nrm(ks[3], (L, HEAD_DIM), 0.02),
        "k_norm_g": 1.0 + nrm(ks[4], (L, 3, HEAD_DIM), 0.02),
        "cmp_pos_k": nrm(ks[5], (L, CMP_LEN, HEAD_DIM), 0.1),
        "cmp_pos_v": nrm(ks[6], (L, CMP_LEN, HEAD_DIM), 0.1),
        "cmp_k_w1": nrm(ks[7], (L, CMP_LEN * HEAD_DIM, CMP_HIDDEN), (CMP_LEN * HEAD_DIM) ** -0.5),
        "cmp_k_w2": nrm(ks[8], (L, CMP_HIDDEN, HEAD_DIM), CMP_HIDDEN ** -0.5),
        "cmp_v_w1": nrm(ks[9], (L, CMP_LEN * HEAD_DIM, CMP_HIDDEN), (CMP_LEN * HEAD_DIM) ** -0.5),
        "cmp_v_w2": nrm(ks[10], (L, CMP_HIDDEN, HEAD_DIM), CMP_HIDDEN ** -0.5),
        "w_branch_nsa": nrm(ks[11], (L, NSA_W, D_MODEL), NSA_W ** -0.5),
        "w_branch_sb": nrm(ks[12], (L, SB_W, D_MODEL), SB_W ** -0.5),
        "w_out": nrm(ks[13], (L, D_MODEL, D_MODEL), D_MODEL ** -0.5),
        "norm2_g": 1.0 + nrm(ks[14], (L, D_MODEL), 0.02),
        "peer_w_query": nrm(ks[15], (L, D_MODEL, PEER_HEADS * PEER_QDIM), D_MODEL ** -0.5),
        "peer_subkeys": nrm(ks[16], (L, 2, PEER_NKEYS, PEER_QDIM // 2), (PEER_QDIM // 2) ** -0.5),
        "peer_u": nrm(ks[17], (L, PEER_N_EXPERTS, D_MODEL), D_MODEL ** -0.5),
        "peer_v": nrm(ks[18], (L, PEER_N_EXPERTS, D_MODEL), 0.5),
    }


def reference(x, norm1_g, w_in, q_norm_g, k_norm_g, cmp_pos_k, cmp_pos_v, cmp_k_w1, cmp_k_w2,
              cmp_v_w1, cmp_v_w2, w_branch_nsa, w_branch_sb, w_out, norm2_g, peer_w_query,
              peer_subkeys, peer_u, peer_v):
    h = x.astype(jnp.float32)
    for l in range(DEPTH):
        xn = rms_norm(h, norm1_g[l])
        h = h + token_mixers(xn, w_in[l], q_norm_g[l], k_norm_g[l], cmp_pos_k[l], cmp_pos_v[l],
                             cmp_k_w1[l], cmp_k_w2[l], cmp_v_w1[l], cmp_v_w2[l],
                             w_branch_nsa[l], w_branch_sb[l], w_out[l])
        hn = rms_norm(h, norm2_g[l])
        h = h + peer(hn, peer_w_query[l], peer_subkeys[l], peer_u[l], peer_v[l])
    return h.astype(x.dtype)
```

```python
import functools
import math

import jax
import jax.numpy as jnp
from jax import lax
from jax.experimental import pallas as pl
from jax.experimental.pallas import tpu as pltpu

F32 = jnp.float32
BF16 = jnp.bfloat16

HEAD_DIM = 64
NSA_HEADS = 8
NSA_GROUPS = 2
NSA_HPG = NSA_HEADS // NSA_GROUPS
CMP_LEN = 32
CMP_STRIDE = 16
CMP_HIDDEN = 256
SEL_BLOCK = 64
SEL_TOPN = 16
WINDOW = 512
SB_HEADS = 8
ROPE_THETA = 10000.0
PEER_HEADS = 8
PEER_NKEYS = 128
PEER_QDIM = 256
PEER_TOPK = 16
EPS = 1e-6
NEG = -1e30

LANES = 128
VMEM_LIMIT = 56 * 1024 * 1024

CB_GATE = 0
CB_Q = 16
CB_KC, CB_VC, CB_KS, CB_VS, CB_KW, CB_VW = 20, 21, 22, 23, 24, 25
CB_QSB, CB_KSB, CB_VSB = 26, 30, 34
CB_GN = 38
N_CB = 39

SB_EXIT = 104.0


def _cparams(sem, vmem=None):
    return pltpu.CompilerParams(dimension_semantics=sem, vmem_limit_bytes=vmem or VMEM_LIMIT)


def _norm_matmul_kernel(x_ref, g_ref, w_ref, o_ref):
    x = x_ref[...]
    ms = jnp.mean(x * x, axis=-1, keepdims=True)
    xn = (x * lax.rsqrt(ms + EPS) * g_ref[...]).astype(BF16)
    o_ref[...] = jnp.dot(xn, w_ref[...], preferred_element_type=F32)


def _norm_matmul(x2d, gamma, w_bf16, tm, tn, name):
    t, d = x2d.shape
    n = w_bf16.shape[1]
    return pl.pallas_call(
        _norm_matmul_kernel,
        out_shape=jax.ShapeDtypeStruct((t, n), F32),
        grid=(n // tn, t // tm),
        in_specs=[
            pl.BlockSpec((tm, d), lambda j, i: (i, 0)),
            pl.BlockSpec((1, d), lambda j, i: (0, 0)),
            pl.BlockSpec((d, tn), lambda j, i: (0, j)),
        ],
        out_specs=pl.BlockSpec((tm, tn), lambda j, i: (i, j)),
        compiler_params=_cparams(("parallel", "parallel")),
        name=name,
    )(x2d, gamma.reshape(1, d), w_bf16)


def _group_mean_sq(x):
    ss = x * x
    r = lax.broadcasted_iota(jnp.int32, (LANES, LANES), 0) // HEAD_DIM
    c = lax.broadcasted_iota(jnp.int32, (LANES, LANES), 1) // HEAD_DIM
    ones_bd = jnp.where(r == c, 1.0, 0.0).astype(BF16)
    hi = ss.astype(BF16)
    lo = (ss - hi.astype(F32)).astype(BF16)
    tot = (jnp.dot(hi, ones_bd, preferred_element_type=F32)
           + jnp.dot(lo, ones_bd, preferred_element_type=F32))
    return tot * (1.0 / HEAD_DIM)


def _norm_rope(x, cos, sin_signed, gamma):
    xn = x * lax.rsqrt(_group_mean_sq(x) + EPS) * gamma
    lane = lax.broadcasted_iota(jnp.int32, xn.shape, 1)
    first_half = (lane % HEAD_DIM) < (HEAD_DIM // 2)
    swapped = jnp.where(first_half, pltpu.roll(xn, LANES - HEAD_DIM // 2, 1), pltpu.roll(xn, HEAD_DIM // 2, 1))
    return xn * cos + swapped * sin_signed


def _rope_tables(pos):
    half = HEAD_DIM // 2
    d = jnp.arange(LANES) % HEAD_DIM
    freqs = ROPE_THETA ** (-(d % half).astype(F32) / half)
    ang = pos.astype(F32)[:, None] * freqs[None, :]
    sign = jnp.where(d < half, -1.0, 1.0).astype(F32)
    return jnp.cos(ang), jnp.sin(ang) * sign[None, :]


def _norm_rope_kernel(x_ref, cos_ref, sin_ref, g_ref, o_ref):
    o_ref[...] = _norm_rope(x_ref[...], cos_ref[...], sin_ref[...], g_ref[0]).astype(o_ref.dtype)


def _prep_norm_rope(proj, cos, sin, gammas, seq, ts):
    t = proj.shape[0]
    n_slab = gammas.shape[0]
    s_tiles = seq // ts
    return pl.pallas_call(
        _norm_rope_kernel,
        out_shape=jax.ShapeDtypeStruct((t, n_slab * LANES), BF16),
        grid=(t // ts, n_slab),
        in_specs=[
            pl.BlockSpec((ts, LANES), lambda i, j: (i, CB_Q + j + 2 * (j // 4) + (j // 5))),
            pl.BlockSpec((ts, LANES), lambda i, j: (i % s_tiles, 0)),
            pl.BlockSpec((ts, LANES), lambda i, j: (i % s_tiles, 0)),
            pl.BlockSpec((1, 1, LANES), lambda i, j: (j, 0, 0)),
        ],
        out_specs=pl.BlockSpec((ts, LANES), lambda i, j: (i, j)),
        compiler_params=_cparams(("parallel", "parallel")),
        name="prep_norm_rope",
    )(proj, cos, sin, gammas)


T_SLABS = (CB_VS, CB_VW, CB_QSB, CB_QSB + 1, CB_QSB + 2, CB_QSB + 3,
           CB_VSB, CB_VSB + 1, CB_VSB + 2, CB_VSB + 3, CB_GN)
TS_VS, TS_VW, TS_QSB, TS_VSB, TS_GN = 0, 1, 2, 6, 10


def _transpose_kernel(cm_ref, x_ref, o_ref):
    del cm_ref
    o_ref[0] = x_ref[...].T.astype(o_ref.dtype)


def _prep_transpose(proj, batch, seq, ts):
    s_tiles = seq // ts
    colmap = jnp.asarray(T_SLABS, jnp.int32)
    return pl.pallas_call(
        _transpose_kernel,
        out_shape=jax.ShapeDtypeStruct((batch, len(T_SLABS) * LANES, seq), BF16),
        grid_spec=pltpu.PrefetchScalarGridSpec(
            num_scalar_prefetch=1,
            grid=(batch, s_tiles, len(T_SLABS)),
            in_specs=[pl.BlockSpec((ts, LANES), lambda b, i, j, cm: (b * s_tiles + i, cm[j]))],
            out_specs=pl.BlockSpec((1, LANES, ts), lambda b, i, j, cm: (b, j, i)),
        ),
        compiler_params=_cparams(("parallel", "parallel", "parallel")),
        name="prep_transpose",
    )(colmap, proj)


def _compress_kernel(a_ref, pa_ref, pb_ref, w1a_ref, w1b_ref, w2_ref, *rest, is_key):
    a = a_ref[0]
    ncp = a.shape[0]
    p = jnp.dot((a + pa_ref[...]).astype(BF16), w1a_ref[...], preferred_element_type=F32)
    q = jnp.dot((a + pb_ref[...]).astype(BF16), w1b_ref[...], preferred_element_type=F32)
    hid = jax.nn.gelu(p + pltpu.roll(q, ncp - 1, 0))
    if is_key:
        cos_ref, sin_ref, g_ref, o_ref = rest
        out = jnp.dot(hid.astype(BF16), w2_ref[...], preferred_element_type=F32)
        o_ref[0] = _norm_rope(out, cos_ref[...], sin_ref[...], g_ref[...]).astype(o_ref.dtype)
    else:
        (o_ref,) = rest
        nt_dims = (((1,), (1,)), ((), ()))
        out_t = lax.dot_general(w2_ref[...], hid.astype(BF16), nt_dims, preferred_element_type=F32)
        o_ref[0] = out_t.astype(o_ref.dtype)


def _expand_cmp_weights(w1, w2):
    eye = jnp.eye(NSA_GROUPS, dtype=F32)
    w1r = w1.reshape(2, CMP_STRIDE, HEAD_DIM, CMP_HIDDEN)
    ex = w1r[:, :, None, :, None, :] * eye[None, None, :, None, :, None]
    ex = ex.reshape(2, CMP_STRIDE * LANES, NSA_GROUPS * CMP_HIDDEN).astype(BF16)
    w2x = (w2[None, :, None, :] * eye[:, None, :, None]).reshape(NSA_GROUPS * CMP_HIDDEN, LANES).astype(BF16)
    return ex[0], ex[1], w2x


def _compress(tok_slab, pos_emb, w1, w2, batch, seq, key_args=None):
    ncp = seq // CMP_STRIDE
    a = tok_slab.reshape(batch, ncp, CMP_STRIDE * LANES)
    pos2 = jnp.tile(pos_emb.astype(F32)[:, None, :], (1, NSA_GROUPS, 1)).reshape(2, 1, CMP_STRIDE * LANES)
    w1a, w1b, w2x = _expand_cmp_weights(w1, w2)
    kdim = CMP_STRIDE * LANES
    hdim = NSA_GROUPS * CMP_HIDDEN
    if key_args is None:
        w2x = w2x.T
    in_specs = [
        pl.BlockSpec((1, ncp, kdim), lambda b: (b, 0, 0)),
        pl.BlockSpec((1, kdim), lambda b: (0, 0)),
        pl.BlockSpec((1, kdim), lambda b: (0, 0)),
        pl.BlockSpec((kdim, hdim), lambda b: (0, 0)),
        pl.BlockSpec((kdim, hdim), lambda b: (0, 0)),
        pl.BlockSpec(w2x.shape, lambda b: (0, 0)),
    ]
    args = [a, pos2[0], pos2[1], w1a, w1b, w2x]
    if key_args is not None:
        cos, sin, gamma = key_args
        in_specs += [pl.BlockSpec((ncp, LANES), lambda b: (0, 0)),
                     pl.BlockSpec((ncp, LANES), lambda b: (0, 0)),
                     pl.BlockSpec((1, LANES), lambda b: (0, 0))]
        args += [cos, sin, gamma]
        out_shape = jax.ShapeDtypeStruct((batch, ncp, LANES), BF16)
        out_spec = pl.BlockSpec((1, ncp, LANES), lambda b: (b, 0, 0))
    else:
        out_shape = jax.ShapeDtypeStruct((batch, LANES, ncp), BF16)
        out_spec = pl.BlockSpec((1, LANES, ncp), lambda b: (b, 0, 0))
    return pl.pallas_call(
        functools.partial(_compress_kernel, is_key=key_args is not None),
        out_shape=out_shape,
        grid=(batch,),
        in_specs=in_specs,
        out_specs=out_spec,
        compiler_params=_cparams(("parallel",)),
        name="compress_k" if key_args is not None else "compress_v",
    )(*args)


NSA_TQ = 128
NSA_KB = 128


def _flash_step(k_tile, vt_tile, qtm, bias, carry):
    m, l, acc = carry
    s = jnp.dot(k_tile, qtm, preferred_element_type=F32) + bias
    m_new = jnp.maximum(m, jnp.max(s, axis=0, keepdims=True))
    alpha = jnp.exp(m - m_new)
    p = jnp.exp(s - m_new)
    l = alpha * l + jnp.sum(p, axis=0, keepdims=True)
    acc = alpha * acc + jnp.dot(vt_tile, p.astype(BF16), preferred_element_type=F32)
    return m_new, l, acc


def _nsa_kernel(q_ref, kc_ref, vct_ref, ks_ref, kw_ref, vst_ref, vwt_ref, gate_ref, ovt_ref,
                o_ref, selb_ref, *, n_sel):
    g = pl.program_id(1)
    i = pl.program_id(2)
    tq, kb, hpg = NSA_TQ, NSA_KB, NSA_HPG
    n = hpg * tq
    t0 = i * tq
    grow = pl.multiple_of(g * HEAD_DIM, HEAD_DIM)

    qt = q_ref[0].astype(F32).T
    rowgrp = lax.broadcasted_iota(jnp.int32, (LANES, tq), 0) // HEAD_DIM
    parts = []
    for h in range(hpg):
        blk = qt[h * HEAD_DIM:(h + 1) * HEAD_DIM]
        parts.append(jnp.where(rowgrp == g, jnp.concatenate([blk, blk], axis=0), 0.0))
    qtm = jnp.concatenate(parts, axis=1).astype(BF16)

    tok1 = t0 + lax.broadcasted_iota(jnp.int32, (1, tq), 1)
    tokn = jnp.concatenate([tok1] * hpg, axis=1)

    ncp = kc_ref.shape[1]
    sc = jnp.dot(kc_ref[0], qtm, preferred_element_type=F32)
    cend = lax.broadcasted_iota(jnp.int32, (ncp, n), 0) * CMP_STRIDE + (CMP_LEN - 1)
    sc = jnp.where(cend <= tokn, sc, NEG)
    mc = jnp.max(sc, axis=0, keepdims=True)
    ec = jnp.exp(sc - mc)
    has_c = jnp.where(tokn >= CMP_LEN - 1, 1.0, 0.0)
    pc = ec * (has_c / jnp.sum(ec, axis=0, keepdims=True))
    o_cmp = jnp.dot(vct_ref[0, pl.ds(grow, HEAD_DIM), :], pc.astype(BF16), preferred_element_type=F32)

    psum = pc[:, 0:tq]
    for h in range(1, hpg):
        psum = psum + pc[:, h * tq:(h + 1) * tq]
    p_hi = psum.astype(BF16)
    p_lo = (psum - p_hi.astype(F32)).astype(BF16)
    imp = (jnp.dot(ovt_ref[...], p_hi, preferred_element_type=F32)
           + jnp.dot(ovt_ref[...], p_lo, preferred_element_type=F32))
    ns = imp.shape[0]
    jidx = lax.broadcasted_iota(jnp.int32, (ns, tq), 0)
    jf = jidx.astype(F32)
    tokb = jnp.broadcast_to(tok1, (ns, tq))
    cur = tokb // SEL_BLOCK
    forced = (jidx == 0) | (jidx == cur) | (jidx == cur - 1)
    valid = jidx * SEL_BLOCK <= tokb
    score = jnp.where(forced, 1e4, jnp.where(valid, imp, -1e4))
    selb = jnp.full((ns, tq), NEG, F32)
    for _ in range(n_sel):
        mx = jnp.max(score, axis=0, keepdims=True)
        first = jnp.min(jnp.where(score == mx, jf, float(ns)), axis=0, keepdims=True)
        pick = jf == first
        selb = jnp.where(pick, 0.0, selb)
        score = jnp.where(pick, -jnp.inf, score)
    selb_ref[...] = selb

    krow = lax.broadcasted_iota(jnp.int32, (kb, tq), 0)
    init = (jnp.full((1, n), NEG, F32), jnp.zeros((1, n), F32), jnp.zeros((HEAD_DIM, n), F32))

    half = kb // SEL_BLOCK
    def sel_body(k, carry):
        k0 = pl.multiple_of(k * kb, kb)
        rows = [jnp.broadcast_to(selb_ref[pl.ds(k * half + r, 1), :], (SEL_BLOCK, tq)) for r in range(half)]
        bias1 = jnp.where(k0 + krow <= tok1, jnp.concatenate(rows, axis=0), NEG)
        bias = jnp.concatenate([bias1] * hpg, axis=1)
        return _flash_step(ks_ref[0, pl.ds(k0, kb), :], vst_ref[0, pl.ds(grow, HEAD_DIM), pl.ds(k0, kb)],
                           qtm, bias, carry)
    _, l_s, acc_s = lax.fori_loop(0, i + 1, sel_body, init)

    def win_body(k, carry):
        k0 = pl.multiple_of(k * kb, kb)
        kpos = k0 + krow
        bias1 = jnp.where((kpos <= tok1) & (kpos > tok1 - WINDOW), 0.0, NEG)
        bias = jnp.concatenate([bias1] * hpg, axis=1)
        return _flash_step(kw_ref[0, pl.ds(k0, kb), :], vwt_ref[0, pl.ds(grow, HEAD_DIM), pl.ds(k0, kb)],
                           qtm, bias, carry)
    _, l_w, acc_w = lax.fori_loop(jnp.maximum(i - WINDOW // kb, 0), i + 1, win_body, init)

    gt = jax.nn.sigmoid(gate_ref[0, pl.ds(pl.multiple_of(g * 16, 16), 16), :].astype(F32))
    o_sel = acc_s / l_s
    o_win = acc_w / l_w
    for h in range(hpg):
        sl = slice(h * tq, (h + 1) * tq)
        o_h = (gt[h:h + 1] * o_cmp[:, sl] + gt[hpg + h:hpg + h + 1] * o_sel[:, sl]
               + gt[2 * hpg + h:2 * hpg + h + 1] * o_win[:, sl])
        o_ref[0, h * HEAD_DIM:(h + 1) * HEAD_DIM, :] = o_h.astype(o_ref.dtype)


def _nsa(qk, kc, vct, tposed, ovt, batch, seq):
    tq = NSA_TQ
    ncp = seq // CMP_STRIDE
    ns = seq // SEL_BLOCK
    n_sel = min(SEL_TOPN, ns)
    qw = NSA_HPG * HEAD_DIM
    return pl.pallas_call(
        functools.partial(_nsa_kernel, n_sel=n_sel),
        out_shape=jax.ShapeDtypeStruct((batch, NSA_HEADS * HEAD_DIM, seq), BF16),
        grid=(batch, NSA_GROUPS, seq // tq),
        in_specs=[
            pl.BlockSpec((1, tq, qw), lambda b, g, i: (b, i, g)),
            pl.BlockSpec((1, ncp, LANES), lambda b, g, i: (b, 0, 0)),
            pl.BlockSpec((1, LANES, ncp), lambda b, g, i: (b, 0, 0)),
            pl.BlockSpec((1, seq, LANES), lambda b, g, i: (b, 0, 4)),
            pl.BlockSpec((1, seq, LANES), lambda b, g, i: (b, 0, 5)),
            pl.BlockSpec((1, LANES, seq), lambda b, g, i: (b, TS_VS, 0)),
            pl.BlockSpec((1, LANES, seq), lambda b, g, i: (b, TS_VW, 0)),
            pl.BlockSpec((1, LANES, tq), lambda b, g, i: (b, TS_GN, i)),
            pl.BlockSpec((ns, ncp), lambda b, g, i: (0, 0)),
        ],
        out_specs=pl.BlockSpec((1, qw, tq), lambda b, g, i: (b, g, i)),
        scratch_shapes=[pltpu.VMEM((ns, tq), F32)],
        compiler_params=_cparams(("parallel", "parallel", "arbitrary")),
        name="nsa",
    )(qk, kc, vct, qk, qk, tposed, tposed, tposed, ovt)


SB_TQ = 256


def _sb_kernel(qt_ref, k_ref, vt_ref, o_ref):
    hd = pl.program_id(1)
    i = pl.program_id(2)
    tq = SB_TQ
    kb = tq
    scale = 1.0 / math.sqrt(HEAD_DIM)

    q2 = qt_ref[0].astype(F32) * scale
    rowgrp = lax.broadcasted_iota(jnp.int32, (LANES, tq), 0) // HEAD_DIM
    qtm = jnp.where(rowgrp == hd % 2, jnp.concatenate([q2, q2], axis=0), 0.0).astype(BF16)

    r = lax.broadcasted_iota(jnp.int32, (kb, kb), 0)
    c = lax.broadcasted_iota(jnp.int32, (kb, kb), 1)
    later = jnp.where(c > r, 1.0, 0.0).astype(BF16)
    krow = lax.broadcasted_iota(jnp.int32, (kb, tq), 0)
    tcol = lax.broadcasted_iota(jnp.int32, (kb, tq), 1)

    def body(st):
        k, carry, acc, _ = st
        k0 = pl.multiple_of(k * kb, kb)
        z = jnp.dot(k_ref[0, pl.ds(k0, kb), :].astype(BF16), qtm, preferred_element_type=F32)
        sp = jnp.maximum(z, 0.0) + jnp.log1p(jnp.exp(-jnp.abs(z)))
        mask = (k0 + krow) < (i * tq + tcol)
        spm = jnp.where(mask, sp, 0.0)
        hi = spm.astype(BF16)
        lo = (spm - hi.astype(F32)).astype(BF16)
        after = (jnp.dot(later, hi, preferred_element_type=F32)
                 + jnp.dot(later, lo, preferred_element_type=F32))
        a = jnp.where(mask, jnp.exp(z - sp - after - carry), 0.0)
        acc = acc + jnp.dot(vt_ref[0, :, pl.ds(k0, kb)], a.astype(BF16), preferred_element_type=F32)
        carry = carry + after[0:1] + spm[0:1]
        go = jnp.logical_and(k > 0, jnp.min(carry) <= SB_EXIT)
        return k - 1, carry, acc, go

    st = (i, jnp.zeros((1, tq), F32), jnp.zeros((HEAD_DIM, tq), F32), i >= 0)
    _, _, acc, _ = lax.while_loop(lambda s: s[3], body, st)
    o_ref[0] = acc.astype(o_ref.dtype)


def _sb(proj3, tposed, batch, seq):
    tq = SB_TQ
    return pl.pallas_call(
        _sb_kernel,
        out_shape=jax.ShapeDtypeStruct((batch, SB_HEADS * HEAD_DIM, seq), BF16),
        grid=(batch, SB_HEADS, seq // tq),
        in_specs=[
            pl.BlockSpec((1, HEAD_DIM, tq), lambda b, h, i: (b, 2 * TS_QSB + h, i)),
            pl.BlockSpec((1, seq, LANES), lambda b, h, i: (b, 0, CB_KSB + h // 2)),
            pl.BlockSpec((1, HEAD_DIM, seq), lambda b, h, i: (b, 2 * TS_VSB + h, 0)),
        ],
        out_specs=pl.BlockSpec((1, HEAD_DIM, tq), lambda b, h, i: (b, h, i)),
        compiler_params=_cparams(("parallel", "parallel", "arbitrary")),
        name="stickbreaking",
    )(tposed, proj3, tposed)


def _merge_kernel(ont_ref, ost_ref, ga_ref, gb_ref, x_ref, wa_ref, wb_ref, wo_ref, o_ref):
    tn_dims = (((0,), (0,)), ((), ()))
    a = lax.dot_general(ont_ref[0], wa_ref[...], tn_dims, preferred_element_type=F32)
    b = lax.dot_general(ost_ref[0], wb_ref[...], tn_dims, preferred_element_type=F32)
    merged = jax.nn.sigmoid(ga_ref[...]) * a + jax.nn.sigmoid(gb_ref[...]) * b
    o_ref[...] = x_ref[...] + jnp.dot(merged.astype(BF16), wo_ref[...], preferred_element_type=F32)


def _merge(ont, ost, proj, x2d, wa, wb, wo, batch, seq, tm):
    d = x2d.shape[1]
    s_tiles = seq // tm
    hw = ont.shape[1]
    return pl.pallas_call(
        _merge_kernel,
        out_shape=jax.ShapeDtypeStruct(x2d.shape, F32),
        grid=(batch, s_tiles),
        in_specs=[
            pl.BlockSpec((1, hw, tm), lambda b, i: (b, 0, i)),
            pl.BlockSpec((1, hw, tm), lambda b, i: (b, 0, i)),
            pl.BlockSpec((tm, d), lambda b, i: (b * s_tiles + i, 0)),
            pl.BlockSpec((tm, d), lambda b, i: (b * s_tiles + i, 1)),
            pl.BlockSpec((tm, d), lambda b, i: (b * s_tiles + i, 0)),
            pl.BlockSpec((hw, d), lambda b, i: (0, 0)),
            pl.BlockSpec((hw, d), lambda b, i: (0, 0)),
            pl.BlockSpec((d, d), lambda b, i: (0, 0)),
        ],
        out_specs=pl.BlockSpec((tm, d), lambda b, i: (b * s_tiles + i, 0)),
        compiler_params=_cparams(("parallel", "parallel")),
        name="merge",
    )(ont, ost, proj, proj, x2d, wa, wb, wo)


PEER_TR = 256


def _topk_ranked(s, idx, k, vals_ref=None):
    rank = jnp.full(s.shape, float(k), F32)
    for r in range(k):
        mx = jnp.max(s, axis=0, keepdims=True)
        first = jnp.min(jnp.where(s == mx, idx, 1e9), axis=0, keepdims=True)
        pick = idx == first
        rank = jnp.where(pick, float(r), rank)
        s = jnp.where(pick, -jnp.inf, s)
        if vals_ref is not None:
            vals_ref[r:r + 1, :] = mx
    return rank


def _route_kernel(q_ref, sk_ref, n1_ref, a1_ref, b2_ref, e2_ref, v1_ref, v2_ref):
    k = PEER_TOPK
    half = PEER_QDIM // 2
    nt_dims = (((1,), (1,)), ((), ()))
    qh = q_ref[...].astype(BF16)
    s1 = lax.dot_general(sk_ref[0].astype(BF16), qh[:, :half], nt_dims, preferred_element_type=F32)
    s2 = lax.dot_general(sk_ref[1].astype(BF16), qh[:, half:], nt_dims, preferred_element_type=F32)
    tr = s1.shape[1]
    kidx = lax.broadcasted_iota(jnp.int32, s1.shape, 0).astype(F32)
    rank1 = _topk_ranked(s1, kidx, k, v1_ref)
    rank2 = _topk_ranked(s2, kidx, k, v2_ref)
    v1 = v1_ref[...]
    v2 = v2_ref[...]

    ha = k // 2
    cand = jnp.concatenate([v1[a:a + 1] + v2 for a in range(ha)] + [v1[ha:k] + v2[0:1]], axis=0)
    ncand = cand.shape[0]
    row = lax.broadcasted_iota(jnp.int32, (ncand, tr), 0)
    flat = jnp.where(row < ha * k, row, (row - ha * k + ha) * k).astype(F32)
    crank = _topk_ranked(cand, flat, k)
    selc = jnp.where(crank < float(k), 1.0, 0.0)
    mx = v1[0:1] + v2[0:1]
    z = jnp.sum(selc * jnp.exp(cand - mx), axis=0, keepdims=True)
    n_a = [jnp.sum(selc[a * k:(a + 1) * k], axis=0, keepdims=True) for a in range(ha)]
    n_a += [selc[ha * k + a - ha:ha * k + a - ha + 1] for a in range(ha, k)]

    n1 = jnp.zeros_like(s1)
    for a in range(k):
        n1 = jnp.where(rank1 == float(a), n_a[a], n1)
    n1_ref[0] = n1
    a1_ref[0] = jnp.exp(s1 - v1[0:1]) / z
    b2_ref[0] = rank2
    e2_ref[0] = jnp.exp(s2 - v2[0:1])


def _route(qp, subkeys):
    t = qp.shape[0]
    tr = PEER_TR
    nk = PEER_NKEYS
    shp = jax.ShapeDtypeStruct((PEER_HEADS, nk, t), F32)
    spec = pl.BlockSpec((1, nk, tr), lambda i, h: (h, 0, i))
    return pl.pallas_call(
        _route_kernel,
        out_shape=(shp, shp, shp, shp),
        grid=(t // tr, PEER_HEADS),
        in_specs=[
            pl.BlockSpec((tr, PEER_QDIM), lambda i, h: (i, h)),
            pl.BlockSpec((2, nk, PEER_QDIM // 2), lambda i, h: (0, 0, 0)),
        ],
        out_specs=(spec, spec, spec, spec),
        scratch_shapes=[pltpu.VMEM((PEER_TOPK, tr), F32), pltpu.VMEM((PEER_TOPK, tr), F32)],
        compiler_params=_cparams(("parallel", "parallel")),
        name="peer_route",
    )(qp, subkeys)


PEER_TM = 512
PEER_NB = 2


def _peer_kernel(h_ref, g_ref, u_ref, vt_ref, n1_ref, a1_ref, b2_ref, e2_ref, o_ref, hnt_ref, acc_ref):
    kk = pl.program_id(1)
    nk = PEER_NKEYS

    @pl.when(kk == 0)
    def _():
        h = h_ref[...]
        ms = jnp.mean(h * h, axis=-1, keepdims=True)
        hnt_ref[...] = (h * lax.rsqrt(ms + EPS) * g_ref[...]).T.astype(BF16)
        acc_ref[...] = jnp.zeros_like(acc_ref)

    act = jax.nn.gelu(jnp.dot(u_ref[...], hnt_ref[...], preferred_element_type=F32))
    parts = []
    for j in range(PEER_NB):
        i1 = kk * PEER_NB + j
        w = jnp.zeros((nk, act.shape[1]), F32)
        for hd in range(PEER_HEADS):
            n1 = n1_ref[hd, pl.ds(i1, 1), :]
            a1 = a1_ref[hd, pl.ds(i1, 1), :]
            w = w + jnp.where(b2_ref[hd] < n1, e2_ref[hd] * a1, 0.0)
        parts.append((w * act[j * nk:(j + 1) * nk]).astype(BF16))
    mt = jnp.concatenate(parts, axis=0)
    acc_ref[...] += jnp.dot(vt_ref[...], mt, preferred_element_type=F32)

    @pl.when(kk == pl.num_programs(1) - 1)
    def _():
        o_ref[...] = h_ref[...] + acc_ref[...].T


def _peer(h2d, gamma, u_bf16, vt_bf16, tables):
    t, d = h2d.shape
    tm = min(PEER_TM, t)
    nk = PEER_NKEYS
    ke = PEER_NB * nk
    tspec = pl.BlockSpec((PEER_HEADS, nk, tm), lambda i, k: (0, 0, i))
    return pl.pallas_call(
        _peer_kernel,
        out_shape=jax.ShapeDtypeStruct((t, d), F32),
        grid=(t // tm, nk // PEER_NB),
        in_specs=[
            pl.BlockSpec((tm, d), lambda i, k: (i, 0)),
            pl.BlockSpec((1, d), lambda i, k: (0, 0)),
            pl.BlockSpec((ke, d), lambda i, k: (k, 0)),
            pl.BlockSpec((d, ke), lambda i, k: (0, k)),
            tspec, tspec, tspec, tspec,
        ],
        out_specs=pl.BlockSpec((tm, d), lambda i, k: (i, 0)),
        scratch_shapes=[pltpu.VMEM((d, tm), BF16), pltpu.VMEM((d, tm), F32)],
        compiler_params=_cparams(("parallel", "arbitrary")),
        name="peer_experts",
    )(h2d, gamma.reshape(1, d), u_bf16, vt_bf16, *tables)


def _reorder_w_in(w_in):
    d = w_in.shape[0]
    nsa_w = NSA_HEADS * HEAD_DIM
    kv_w = NSA_GROUPS * HEAD_DIM
    sb_w = SB_HEADS * HEAD_DIM
    c0 = nsa_w
    c1 = c0 + 6 * kv_w
    c2 = c1 + 3 * NSA_HEADS
    c3 = c2 + 3 * sb_w
    gn = w_in[:, c1:c2].reshape(d, 3, NSA_GROUPS, NSA_HPG).transpose(0, 2, 1, 3)
    gn = jnp.pad(gn.reshape(d, NSA_GROUPS, 3 * NSA_HPG), ((0, 0), (0, 0), (0, 16 - 3 * NSA_HPG)))
    gn = jnp.pad(gn.reshape(d, NSA_GROUPS * 16), ((0, 0), (0, LANES - NSA_GROUPS * 16)))
    return jnp.concatenate([w_in[:, c3:], w_in[:, :c0], w_in[:, c0:c1], w_in[:, c2:c3], gn], axis=1).astype(BF16)


def _overlap_t(seq):
    ncp = seq // CMP_STRIDE
    nc = (seq - CMP_LEN) // CMP_STRIDE + 1
    ns = seq // SEL_BLOCK
    c_start = jnp.arange(ncp) * CMP_STRIDE
    j_start = jnp.arange(ns) * SEL_BLOCK
    ov = jnp.clip(jnp.minimum(c_start[None, :] + CMP_LEN, j_start[:, None] + SEL_BLOCK)
                  - jnp.maximum(c_start[None, :], j_start[:, None]), 0).astype(F32) / CMP_LEN
    return jnp.where(jnp.arange(ncp)[None, :] < nc, ov, 0.0).astype(BF16)


def _token_mixers(x, norm1_g, w_in, q_norm_g, k_norm_g, cmp_pos_k, cmp_pos_v, cmp_k_w1, cmp_k_w2,
                  cmp_v_w1, cmp_v_w2, w_branch_nsa, w_branch_sb, w_out):
    batch, seq, d = x.shape
    t = batch * seq
    x2d = x.reshape(t, d)
    scale = 1.0 / math.sqrt(HEAD_DIM)

    proj = _norm_matmul(x2d, norm1_g, _reorder_w_in(w_in), tm=512, tn=13 * LANES, name="in_proj")

    cos, sin = _rope_tables(jnp.arange(seq))
    qg = jnp.tile(q_norm_g.astype(F32), 2) * scale
    gammas = jnp.stack([qg] * 4 + [jnp.tile(k_norm_g[1].astype(F32), 2), jnp.tile(k_norm_g[2].astype(F32), 2)])
    qk = _prep_norm_rope(proj, cos, sin, gammas.reshape(6, 1, LANES), seq, ts=512).reshape(batch, seq, 6 * LANES)
    tposed = _prep_transpose(proj, batch, seq, ts=512)

    ncp = seq // CMP_STRIDE
    cosc, sinc = _rope_tables(jnp.arange(ncp) * CMP_STRIDE + CMP_LEN - 1)
    kc = _compress(proj[:, CB_KC * LANES:(CB_KC + 1) * LANES], cmp_pos_k, cmp_k_w1, cmp_k_w2, batch, seq,
                   key_args=(cosc, sinc, jnp.tile(k_norm_g[0].astype(F32), 2).reshape(1, LANES)))
    vct = _compress(proj[:, CB_VC * LANES:(CB_VC + 1) * LANES], cmp_pos_v, cmp_v_w1, cmp_v_w2, batch, seq)

    ont = _nsa(qk, kc, vct, tposed, _overlap_t(seq), batch, seq)
    ost = _sb(proj.reshape(batch, seq, N_CB * LANES), tposed, batch, seq)
    return _merge(ont, ost, proj, x2d, w_branch_nsa.astype(BF16), w_branch_sb.astype(BF16),
                  w_out.astype(BF16), batch, seq, tm=256)


def kernel(x, norm1_g, w_in, q_norm_g, k_norm_g, cmp_pos_k, cmp_pos_v, cmp_k_w1, cmp_k_w2, cmp_v_w1, cmp_v_w2,
           w_branch_nsa, w_branch_sb, w_out, norm2_g, peer_w_query, peer_subkeys, peer_u, peer_v):
    batch, seq, d = x.shape
    h = x.astype(F32)
    for l in range(norm1_g.shape[0]):
        h2d = _token_mixers(h, norm1_g[l], w_in[l], q_norm_g[l], k_norm_g[l], cmp_pos_k[l], cmp_pos_v[l],
                            cmp_k_w1[l], cmp_k_w2[l], cmp_v_w1[l], cmp_v_w2[l],
                            w_branch_nsa[l], w_branch_sb[l], w_out[l])
        qp = _norm_matmul(h2d, norm2_g[l], peer_w_query[l].astype(BF16), tm=512, tn=1024, name="peer_query")
        tables = _route(qp, peer_subkeys[l])
        h2d = _peer(h2d, norm2_g[l], peer_u[l].astype(BF16), peer_v[l].T.astype(BF16), tables)
        h = h2d.reshape(batch, seq, d)
    return h.astype(x.dtype)
```

```python
import functools
import math

import jax
import jax.numpy as jnp
from jax import lax
from jax.experimental import pallas as pl
from jax.experimental.pallas import tpu as pltpu

F32 = jnp.float32
BF16 = jnp.bfloat16

HEAD_DIM = 64
NSA_HEADS = 8
NSA_GROUPS = 2
NSA_HPG = NSA_HEADS // NSA_GROUPS
CMP_LEN = 32
CMP_STRIDE = 16
CMP_HIDDEN = 256
SEL_BLOCK = 64
SEL_TOPN = 16
WINDOW = 512
SB_HEADS = 8
ROPE_THETA = 10000.0
PEER_HEADS = 8
PEER_NKEYS = 128
PEER_QDIM = 256
PEER_TOPK = 16
EPS = 1e-6
NEG = -1e30

LANES = 128
VMEM_LIMIT = 56 * 1024 * 1024

CB_GATE = 0
CB_Q = 16
CB_KC, CB_VC, CB_KS, CB_VS, CB_KW, CB_VW = 20, 21, 22, 23, 24, 25
CB_QSB, CB_KSB, CB_VSB = 26, 30, 34
CB_GN = 38
N_CB = 39

SB_EXIT = 104.0


def _cparams(sem, vmem=None):
    return pltpu.CompilerParams(dimension_semantics=sem, vmem_limit_bytes=vmem or VMEM_LIMIT)


def _norm_matmul_kernel(x_ref, g_ref, w_ref, o_ref):
    x = x_ref[...]
    ms = jnp.mean(x * x, axis=-1, keepdims=True)
    xn = (x * lax.rsqrt(ms + EPS) * g_ref[...]).astype(BF16)
    o_ref[...] = jnp.dot(xn, w_ref[...], preferred_element_type=F32)


def _norm_matmul(x2d, gamma, w_bf16, tm, tn, name):
    t, d = x2d.shape
    n = w_bf16.shape[1]
    return pl.pallas_call(
        _norm_matmul_kernel,
        out_shape=jax.ShapeDtypeStruct((t, n), F32),
        grid=(n // tn, t // tm),
        in_specs=[
            pl.BlockSpec((tm, d), lambda j, i: (i, 0)),
            pl.BlockSpec((1, d), lambda j, i: (0, 0)),
            pl.BlockSpec((d, tn), lambda j, i: (0, j)),
        ],
        out_specs=pl.BlockSpec((tm, tn), lambda j, i: (i, j)),
        compiler_params=_cparams(("parallel", "parallel")),
        name=name,
    )(x2d, gamma.reshape(1, d), w_bf16)


def _group_mean_sq(x):
    ss = x * x
    r = lax.broadcasted_iota(jnp.int32, (LANES, LANES), 0) // HEAD_DIM
    c = lax.broadcasted_iota(jnp.int32, (LANES, LANES), 1) // HEAD_DIM
    ones_bd = jnp.where(r == c, 1.0, 0.0).astype(BF16)
    hi = ss.astype(BF16)
    lo = (ss - hi.astype(F32)).astype(BF16)
    tot = (jnp.dot(hi, ones_bd, preferred_element_type=F32)
           + jnp.dot(lo, ones_bd, preferred_element_type=F32))
    return tot * (1.0 / HEAD_DIM)


def _norm_rope(x, cos, sin_signed, gamma):
    xn = x * lax.rsqrt(_group_mean_sq(x) + EPS) * gamma
    lane = lax.broadcasted_iota(jnp.int32, xn.shape, 1)
    first_half = (lane % HEAD_DIM) < (HEAD_DIM // 2)
    swapped = jnp.where(first_half, pltpu.roll(xn, LANES - HEAD_DIM // 2, 1), pltpu.roll(xn, HEAD_DIM // 2, 1))
    return xn * cos + swapped * sin_signed


def _rope_tables(pos):
    half = HEAD_DIM // 2
    d = jnp.arange(LANES) % HEAD_DIM
    freqs = ROPE_THETA ** (-(d % half).astype(F32) / half)
    ang = pos.astype(F32)[:, None] * freqs[None, :]
    sign = jnp.where(d < half, -1.0, 1.0).astype(F32)
    return jnp.cos(ang), jnp.sin(ang) * sign[None, :]


def _norm_rope_kernel(x_ref, cos_ref, sin_ref, g_ref, o_ref):
    o_ref[...] = _norm_rope(x_ref[...], cos_ref[...], sin_ref[...], g_ref[0]).astype(o_ref.dtype)


def _prep_norm_rope(proj, cos, sin, gammas, seq, ts):
    t = proj.shape[0]
    n_slab = gammas.shape[0]
    s_tiles = seq // ts
    return pl.pallas_call(
        _norm_rope_kernel,
        out_shape=jax.ShapeDtypeStruct((t, n_slab * LANES), BF16),
        grid=(t // ts, n_slab),
        in_specs=[
            pl.BlockSpec((ts, LANES), lambda i, j: (i, CB_Q + j + 2 * (j // 4) + (j // 5))),
            pl.BlockSpec((ts, LANES), lambda i, j: (i % s_tiles, 0)),
            pl.BlockSpec((ts, LANES), lambda i, j: (i % s_tiles, 0)),
            pl.BlockSpec((1, 1, LANES), lambda i, j: (j, 0, 0)),
        ],
        out_specs=pl.BlockSpec((ts, LANES), lambda i, j: (i, j)),
        compiler_params=_cparams(("parallel", "parallel")),
        name="prep_norm_rope",
    )(proj, cos, sin, gammas)


T_SLABS = (CB_VS, CB_VW, CB_QSB, CB_QSB + 1, CB_QSB + 2, CB_QSB + 3,
           CB_VSB, CB_VSB + 1, CB_VSB + 2, CB_VSB + 3, CB_GN)
TS_VS, TS_VW, TS_QSB, TS_VSB, TS_GN = 0, 1, 2, 6, 10


def _transpose_kernel(cm_ref, x_ref, o_ref):
    del cm_ref
    o_ref[0] = x_ref[...].T.astype(o_ref.dtype)


def _prep_transpose(proj, batch, seq, ts):
    s_tiles = seq // ts
    colmap = jnp.asarray(T_SLABS, jnp.int32)
    return pl.pallas_call(
        _transpose_kernel,
        out_shape=jax.ShapeDtypeStruct((batch, len(T_SLABS) * LANES, seq), BF16),
        grid_spec=pltpu.PrefetchScalarGridSpec(
            num_scalar_prefetch=1,
            grid=(batch, s_tiles, len(T_SLABS)),
            in_specs=[pl.BlockSpec((ts, LANES), lambda b, i, j, cm: (b * s_tiles + i, cm[j]))],
            out_specs=pl.BlockSpec((1, LANES, ts), lambda b, i, j, cm: (b, j, i)),
        ),
        compiler_params=_cparams(("parallel", "parallel", "parallel")),
        name="prep_transpose",
    )(colmap, proj)


def _compress_kernel(a_ref, pa_ref, pb_ref, w1a_ref, w1b_ref, w2_ref, *rest, is_key):
    a = a_ref[0]
    ncp = a.shape[0]
    p = jnp.dot((a + pa_ref[...]).astype(BF16), w1a_ref[...], preferred_element_type=F32)
    q = jnp.dot((a + pb_ref[...]).astype(BF16), w1b_ref[...], preferred_element_type=F32)
    hid = jax.nn.gelu(p + pltpu.roll(q, ncp - 1, 0))
    if is_key:
        cos_ref, sin_ref, g_ref, o_ref = rest
        out = jnp.dot(hid.astype(BF16), w2_ref[...], preferred_element_type=F32)
        o_ref[0] = _norm_rope(out, cos_ref[...], sin_ref[...], g_ref[...]).astype(o_ref.dtype)
    else:
        (o_ref,) = rest
        nt_dims = (((1,), (1,)), ((), ()))
        out_t = lax.dot_general(w2_ref[...], hid.astype(BF16), nt_dims, preferred_element_type=F32)
        o_ref[0] = out_t.astype(o_ref.dtype)


def _expand_cmp_weights(w1, w2):
    eye = jnp.eye(NSA_GROUPS, dtype=F32)
    w1r = w1.reshape(2, CMP_STRIDE, HEAD_DIM, CMP_HIDDEN)
    ex = w1r[:, :, None, :, None, :] * eye[None, None, :, None, :, None]
    ex = ex.reshape(2, CMP_STRIDE * LANES, NSA_GROUPS * CMP_HIDDEN).astype(BF16)
    w2x = (w2[None, :, None, :] * eye[:, None, :, None]).reshape(NSA_GROUPS * CMP_HIDDEN, LANES).astype(BF16)
    return ex[0], ex[1], w2x


def _compress(tok_slab, pos_emb, w1, w2, batch, seq, key_args=None):
    ncp = seq // CMP_STRIDE
    a = tok_slab.reshape(batch, ncp, CMP_STRIDE * LANES)
    pos2 = jnp.tile(pos_emb.astype(F32)[:, None, :], (1, NSA_GROUPS, 1)).reshape(2, 1, CMP_STRIDE * LANES)
    w1a, w1b, w2x = _expand_cmp_weights(w1, w2)
    kdim = CMP_STRIDE * LANES
    hdim = NSA_GROUPS * CMP_HIDDEN
    if key_args is None:
        w2x = w2x.T
    in_specs = [
        pl.BlockSpec((1, ncp, kdim), lambda b: (b, 0, 0)),
        pl.BlockSpec((1, kdim), lambda b: (0, 0)),
        pl.BlockSpec((1, kdim), lambda b: (0, 0)),
        pl.BlockSpec((kdim, hdim), lambda b: (0, 0)),
        pl.BlockSpec((kdim, hdim), lambda b: (0, 0)),
        pl.BlockSpec(w2x.shape, lambda b: (0, 0)),
    ]
    args = [a, pos2[0], pos2[1], w1a, w1b, w2x]
    if key_args is not None:
        cos, sin, gamma = key_args
        in_specs += [pl.BlockSpec((ncp, LANES), lambda b: (0, 0)),
                     pl.BlockSpec((ncp, LANES), lambda b: (0, 0)),
                     pl.BlockSpec((1, LANES), lambda b: (0, 0))]
        args += [cos, sin, gamma]
        out_shape = jax.ShapeDtypeStruct((batch, ncp, LANES), BF16)
        out_spec = pl.BlockSpec((1, ncp, LANES), lambda b: (b, 0, 0))
    else:
        out_shape = jax.ShapeDtypeStruct((batch, LANES, ncp), BF16)
        out_spec = pl.BlockSpec((1, LANES, ncp), lambda b: (b, 0, 0))
    return pl.pallas_call(
        functools.partial(_compress_kernel, is_key=key_args is not None),
        out_shape=out_shape,
        grid=(batch,),
        in_specs=in_specs,
        out_specs=out_spec,
        compiler_params=_cparams(("parallel",)),
        name="compress_k" if key_args is not None else "compress_v",
    )(*args)


NSA_TQ = 128
NSA_KB = 512


def _flash_step(k_tile, vt_tile, qtm, bias, carry):
    m, l, acc = carry
    s = jnp.dot(k_tile, qtm, preferred_element_type=F32) + bias
    m_new = jnp.maximum(m, jnp.max(s, axis=0, keepdims=True))
    alpha = jnp.exp(m - m_new)
    p = jnp.exp(s - m_new)
    l = alpha * l + jnp.sum(p, axis=0, keepdims=True)
    acc = alpha * acc + jnp.dot(vt_tile, p.astype(BF16), preferred_element_type=F32)
    return m_new, l, acc


def _nsa_kernel(q_ref, kc_ref, vct_ref, ks_ref, kw_ref, vst_ref, vwt_ref, gate_ref, ovt_ref,
                o_ref, selb_ref, *, n_sel):
    g = pl.program_id(1)
    i = pl.program_id(2)
    tq, kb, hpg = NSA_TQ, NSA_KB, NSA_HPG
    n = hpg * tq
    t0 = i * tq
    grow = pl.multiple_of(g * HEAD_DIM, HEAD_DIM)

    qt = q_ref[0].astype(F32).T
    rowgrp = lax.broadcasted_iota(jnp.int32, (LANES, tq), 0) // HEAD_DIM
    parts = []
    for h in range(hpg):
        blk = qt[h * HEAD_DIM:(h + 1) * HEAD_DIM]
        parts.append(jnp.where(rowgrp == g, jnp.concatenate([blk, blk], axis=0), 0.0))
    qtm = jnp.concatenate(parts, axis=1).astype(BF16)

    tok1 = t0 + lax.broadcasted_iota(jnp.int32, (1, tq), 1)
    tokn = jnp.concatenate([tok1] * hpg, axis=1)

    ncp = kc_ref.shape[1]
    sc = jnp.dot(kc_ref[0], qtm, preferred_element_type=F32)
    cend = lax.broadcasted_iota(jnp.int32, (ncp, n), 0) * CMP_STRIDE + (CMP_LEN - 1)
    sc = jnp.where(cend <= tokn, sc, NEG)
    mc = jnp.max(sc, axis=0, keepdims=True)
    ec = jnp.exp(sc - mc)
    has_c = jnp.where(tokn >= CMP_LEN - 1, 1.0, 0.0)
    pc = ec * (has_c / jnp.sum(ec, axis=0, keepdims=True))
    o_cmp = jnp.dot(vct_ref[0, pl.ds(grow, HEAD_DIM), :], pc.astype(BF16), preferred_element_type=F32)

    psum = pc[:, 0:tq]
    for h in range(1, hpg):
        psum = psum + pc[:, h * tq:(h + 1) * tq]
    p_hi = psum.astype(BF16)
    p_lo = (psum - p_hi.astype(F32)).astype(BF16)
    imp = (jnp.dot(ovt_ref[...], p_hi, preferred_element_type=F32)
           + jnp.dot(ovt_ref[...], p_lo, preferred_element_type=F32))
    ns = imp.shape[0]
    jidx = lax.broadcasted_iota(jnp.int32, (ns, tq), 0)
    jf = jidx.astype(F32)
    tokb = jnp.broadcast_to(tok1, (ns, tq))
    cur = tokb // SEL_BLOCK
    forced = (jidx == 0) | (jidx == cur) | (jidx == cur - 1)
    valid = jidx * SEL_BLOCK <= tokb
    score = jnp.where(forced, 1e4, jnp.where(valid, imp, -1e4))
    selb = jnp.full((ns, tq), NEG, F32)
    for _ in range(n_sel):
        mx = jnp.max(score, axis=0, keepdims=True)
        first = jnp.min(jnp.where(score == mx, jf, float(ns)), axis=0, keepdims=True)
        pick = jf == first
        selb = jnp.where(pick, 0.0, selb)
        score = jnp.where(pick, -jnp.inf, score)
    selb_ref[...] = selb

    wlen = WINDOW + tq
    w0 = pl.multiple_of(jnp.maximum(t0 - WINDOW, 0), tq)
    wpos = w0 + lax.broadcasted_iota(jnp.int32, (wlen, tq), 0)
    wbias1 = jnp.where((wpos <= tok1) & (wpos > tok1 - WINDOW), 0.0, NEG)
    sw = (jnp.dot(kw_ref[0, pl.ds(w0, wlen), :], qtm, preferred_element_type=F32)
          + jnp.concatenate([wbias1] * hpg, axis=1))
    pw = jnp.exp(sw - jnp.max(sw, axis=0, keepdims=True))
    o_win = (jnp.dot(vwt_ref[0, pl.ds(grow, HEAD_DIM), pl.ds(w0, wlen)], pw.astype(BF16),
                     preferred_element_type=F32) / jnp.sum(pw, axis=0, keepdims=True))

    krow = lax.broadcasted_iota(jnp.int32, (kb, tq), 0)
    init = (jnp.full((1, n), NEG, F32), jnp.zeros((1, n), F32), jnp.zeros((HEAD_DIM, n), F32))
    nblk = kb // SEL_BLOCK
    def sel_body(k, carry):
        k0 = pl.multiple_of(k * kb, kb)
        rows = [jnp.broadcast_to(selb_ref[pl.ds(k * nblk + r, 1), :], (SEL_BLOCK, tq)) for r in range(nblk)]
        bias1 = jnp.where(k0 + krow <= tok1, jnp.concatenate(rows, axis=0), NEG)
        bias = jnp.concatenate([bias1] * hpg, axis=1)
        return _flash_step(ks_ref[0, pl.ds(k0, kb), :], vst_ref[0, pl.ds(grow, HEAD_DIM), pl.ds(k0, kb)],
                           qtm, bias, carry)
    _, l_s, acc_s = lax.fori_loop(0, t0 // kb + 1, sel_body, init)

    gt = jax.nn.sigmoid(gate_ref[0, pl.ds(pl.multiple_of(g * 16, 16), 16), :].astype(F32))
    o_sel = acc_s / l_s
    for h in range(hpg):
        sl = slice(h * tq, (h + 1) * tq)
        o_h = (gt[h:h + 1] * o_cmp[:, sl] + gt[hpg + h:hpg + h + 1] * o_sel[:, sl]
               + gt[2 * hpg + h:2 * hpg + h + 1] * o_win[:, sl])
        o_ref[0, h * HEAD_DIM:(h + 1) * HEAD_DIM, :] = o_h.astype(o_ref.dtype)


def _nsa(qk, kc, vct, tposed, ovt, batch, seq):
    tq = NSA_TQ
    ncp = seq // CMP_STRIDE
    ns = seq // SEL_BLOCK
    n_sel = min(SEL_TOPN, ns)
    qw = NSA_HPG * HEAD_DIM
    return pl.pallas_call(
        functools.partial(_nsa_kernel, n_sel=n_sel),
        out_shape=jax.ShapeDtypeStruct((batch, NSA_HEADS * HEAD_DIM, seq), BF16),
        grid=(batch, NSA_GROUPS, seq // tq),
        in_specs=[
            pl.BlockSpec((1, tq, qw), lambda b, g, i: (b, i, g)),
            pl.BlockSpec((1, ncp, LANES), lambda b, g, i: (b, 0, 0)),
            pl.BlockSpec((1, LANES, ncp), lambda b, g, i: (b, 0, 0)),
            pl.BlockSpec((1, seq, LANES), lambda b, g, i: (b, 0, 4)),
            pl.BlockSpec((1, seq, LANES), lambda b, g, i: (b, 0, 5)),
            pl.BlockSpec((1, LANES, seq), lambda b, g, i: (b, TS_VS, 0)),
            pl.BlockSpec((1, LANES, seq), lambda b, g, i: (b, TS_VW, 0)),
            pl.BlockSpec((1, LANES, tq), lambda b, g, i: (b, TS_GN, i)),
            pl.BlockSpec((ns, ncp), lambda b, g, i: (0, 0)),
        ],
        out_specs=pl.BlockSpec((1, qw, tq), lambda b, g, i: (b, g, i)),
        scratch_shapes=[pltpu.VMEM((ns, tq), F32)],
        compiler_params=_cparams(("parallel", "parallel", "arbitrary")),
        name="nsa",
    )(qk, kc, vct, qk, qk, tposed, tposed, tposed, ovt)


SB_TQ = 256


def _sb_kernel(qt_ref, k_ref, vt_ref, o_ref):
    i = pl.program_id(2)
    tq = SB_TQ
    kb = tq
    n = 2 * tq
    scale = 1.0 / math.sqrt(HEAD_DIM)

    q2 = qt_ref[0].astype(F32) * scale
    rowgrp = lax.broadcasted_iota(jnp.int32, (LANES, tq), 0) // HEAD_DIM
    qtm = jnp.concatenate([jnp.where(rowgrp == 0, q2, 0.0), jnp.where(rowgrp == 1, q2, 0.0)],
                          axis=1).astype(BF16)

    r = lax.broadcasted_iota(jnp.int32, (kb, kb), 0)
    c = lax.broadcasted_iota(jnp.int32, (kb, kb), 1)
    later = jnp.where(c > r, 1.0, 0.0).astype(BF16)
    krow = lax.broadcasted_iota(jnp.int32, (kb, n), 0)
    tcol = lax.broadcasted_iota(jnp.int32, (kb, n), 1) % tq

    def body(st):
        k, carry, acc_a, acc_b, _ = st
        k0 = pl.multiple_of(k * kb, kb)
        z = jnp.dot(k_ref[0, pl.ds(k0, kb), :].astype(BF16), qtm, preferred_element_type=F32)
        sp = jnp.maximum(z, 0.0) + jnp.log1p(jnp.exp(-jnp.abs(z)))
        mask = (k0 + krow) < (i * tq + tcol)
        spm = jnp.where(mask, sp, 0.0)
        hi = spm.astype(BF16)
        lo = (spm - hi.astype(F32)).astype(BF16)
        after = (jnp.dot(later, hi, preferred_element_type=F32)
                 + jnp.dot(later, lo, preferred_element_type=F32))
        a = jnp.where(mask, jnp.exp(z - sp - after - carry), 0.0).astype(BF16)
        acc_a = acc_a + jnp.dot(vt_ref[0, :HEAD_DIM, pl.ds(k0, kb)], a[:, :tq], preferred_element_type=F32)
        acc_b = acc_b + jnp.dot(vt_ref[0, HEAD_DIM:, pl.ds(k0, kb)], a[:, tq:], preferred_element_type=F32)
        carry = carry + after[0:1] + spm[0:1]
        go = jnp.logical_and(k > 0, jnp.min(carry) <= SB_EXIT)
        return k - 1, carry, acc_a, acc_b, go

    zacc = jnp.zeros((HEAD_DIM, tq), F32)
    st = (i, jnp.zeros((1, n), F32), zacc, zacc, i >= 0)
    _, _, acc_a, acc_b, _ = lax.while_loop(lambda s: s[4], body, st)
    o_ref[0, :HEAD_DIM, :] = acc_a.astype(o_ref.dtype)
    o_ref[0, HEAD_DIM:, :] = acc_b.astype(o_ref.dtype)


def _sb(proj3, tposed, batch, seq):
    tq = SB_TQ
    return pl.pallas_call(
        _sb_kernel,
        out_shape=jax.ShapeDtypeStruct((batch, SB_HEADS * HEAD_DIM, seq), BF16),
        grid=(batch, SB_HEADS // 2, seq // tq),
        in_specs=[
            pl.BlockSpec((1, LANES, tq), lambda b, h, i: (b, TS_QSB + h, i)),
            pl.BlockSpec((1, seq, LANES), lambda b, h, i: (b, 0, CB_KSB + h)),
            pl.BlockSpec((1, LANES, seq), lambda b, h, i: (b, TS_VSB + h, 0)),
        ],
        out_specs=pl.BlockSpec((1, LANES, tq), lambda b, h, i: (b, h, i)),
        compiler_params=_cparams(("parallel", "parallel", "arbitrary")),
        name="stickbreaking",
    )(tposed, proj3, tposed)


def _merge_kernel(ont_ref, ost_ref, ga_ref, gb_ref, x_ref, wa_ref, wb_ref, wo_ref, o_ref):
    tn_dims = (((0,), (0,)), ((), ()))
    a = lax.dot_general(ont_ref[0], wa_ref[...], tn_dims, preferred_element_type=F32)
    b = lax.dot_general(ost_ref[0], wb_ref[...], tn_dims, preferred_element_type=F32)
    merged = jax.nn.sigmoid(ga_ref[...]) * a + jax.nn.sigmoid(gb_ref[...]) * b
    o_ref[...] = x_ref[...] + jnp.dot(merged.astype(BF16), wo_ref[...], preferred_element_type=F32)


def _merge(ont, ost, proj, x2d, wa, wb, wo, batch, seq, tm):
    d = x2d.shape[1]
    s_tiles = seq // tm
    hw = ont.shape[1]
    return pl.pallas_call(
        _merge_kernel,
        out_shape=jax.ShapeDtypeStruct(x2d.shape, F32),
        grid=(batch, s_tiles),
        in_specs=[
            pl.BlockSpec((1, hw, tm), lambda b, i: (b, 0, i)),
            pl.BlockSpec((1, hw, tm), lambda b, i: (b, 0, i)),
            pl.BlockSpec((tm, d), lambda b, i: (b * s_tiles + i, 0)),
            pl.BlockSpec((tm, d), lambda b, i: (b * s_tiles + i, 1)),
            pl.BlockSpec((tm, d), lambda b, i: (b * s_tiles + i, 0)),
            pl.BlockSpec((hw, d), lambda b, i: (0, 0)),
            pl.BlockSpec((hw, d), lambda b, i: (0, 0)),
            pl.BlockSpec((d, d), lambda b, i: (0, 0)),
        ],
        out_specs=pl.BlockSpec((tm, d), lambda b, i: (b * s_tiles + i, 0)),
        compiler_params=_cparams(("parallel", "parallel")),
        name="merge",
    )(ont, ost, proj, proj, x2d, wa, wb, wo)


PEER_TR = 256


def _topk_ranked(s, idx, k, vals_ref=None):
    rank = jnp.full(s.shape, float(k), F32)
    for r in range(k):
        mx = jnp.max(s, axis=0, keepdims=True)
        first = jnp.min(jnp.where(s == mx, idx, 1e9), axis=0, keepdims=True)
        pick = idx == first
        rank = jnp.where(pick, float(r), rank)
        s = jnp.where(pick, -jnp.inf, s)
        if vals_ref is not None:
            vals_ref[r:r + 1, :] = mx
    return rank


def _route_kernel(q_ref, sk_ref, n1_ref, a1_ref, b2_ref, e2_ref, v1_ref, v2_ref):
    k = PEER_TOPK
    half = PEER_QDIM // 2
    nt_dims = (((1,), (1,)), ((), ()))
    qh = q_ref[...].astype(BF16)
    s1 = lax.dot_general(sk_ref[0].astype(BF16), qh[:, :half], nt_dims, preferred_element_type=F32)
    s2 = lax.dot_general(sk_ref[1].astype(BF16), qh[:, half:], nt_dims, preferred_element_type=F32)
    tr = s1.shape[1]
    kidx = lax.broadcasted_iota(jnp.int32, s1.shape, 0).astype(F32)
    rank1 = _topk_ranked(s1, kidx, k, v1_ref)
    rank2 = _topk_ranked(s2, kidx, k, v2_ref)
    v1 = v1_ref[...]
    v2 = v2_ref[...]

    ha = k // 2
    cand = jnp.concatenate([v1[a:a + 1] + v2 for a in range(ha)] + [v1[ha:k] + v2[0:1]], axis=0)
    ncand = cand.shape[0]
    row = lax.broadcasted_iota(jnp.int32, (ncand, tr), 0)
    flat = jnp.where(row < ha * k, row, (row - ha * k + ha) * k).astype(F32)
    crank = _topk_ranked(cand, flat, k)
    selc = jnp.where(crank < float(k), 1.0, 0.0)
    mx = v1[0:1] + v2[0:1]
    z = jnp.sum(selc * jnp.exp(cand - mx), axis=0, keepdims=True)
    n_a = [jnp.sum(selc[a * k:(a + 1) * k], axis=0, keepdims=True) for a in range(ha)]
    n_a += [selc[ha * k + a - ha:ha * k + a - ha + 1] for a in range(ha, k)]

    n1 = jnp.zeros_like(s1)
    for a in range(k):
        n1 = jnp.where(rank1 == float(a), n_a[a], n1)
    n1_ref[0] = n1
    a1_ref[0] = jnp.exp(s1 - v1[0:1]) / z
    b2_ref[0] = rank2.astype(b2_ref.dtype)
    e2_ref[0] = jnp.exp(s2 - v2[0:1]).astype(e2_ref.dtype)


def _route(qp, subkeys):
    t = qp.shape[0]
    tr = PEER_TR
    nk = PEER_NKEYS
    shp = jax.ShapeDtypeStruct((PEER_HEADS, nk, t), F32)
    shp16 = jax.ShapeDtypeStruct((PEER_HEADS, nk, t), BF16)
    spec = pl.BlockSpec((1, nk, tr), lambda i, h: (h, 0, i))
    return pl.pallas_call(
        _route_kernel,
        out_shape=(shp, shp, shp16, shp16),
        grid=(t // tr, PEER_HEADS),
        in_specs=[
            pl.BlockSpec((tr, PEER_QDIM), lambda i, h: (i, h)),
            pl.BlockSpec((2, nk, PEER_QDIM // 2), lambda i, h: (0, 0, 0)),
        ],
        out_specs=(spec, spec, spec, spec),
        scratch_shapes=[pltpu.VMEM((PEER_TOPK, tr), F32), pltpu.VMEM((PEER_TOPK, tr), F32)],
        compiler_params=_cparams(("parallel", "parallel")),
        name="peer_route",
    )(qp, subkeys)


PEER_TM = 512
PEER_NB = 4


def _peer_kernel(h_ref, g_ref, u_ref, vt_ref, n1_ref, a1_ref, b2_ref, e2_ref, o_ref, hnt_ref, acc_ref):
    kk = pl.program_id(1)
    nk = PEER_NKEYS

    @pl.when(kk == 0)
    def _():
        h = h_ref[...]
        ms = jnp.mean(h * h, axis=-1, keepdims=True)
        hnt_ref[...] = (h * lax.rsqrt(ms + EPS) * g_ref[...]).T.astype(BF16)
        acc_ref[...] = jnp.zeros_like(acc_ref)

    act = jax.nn.gelu(jnp.dot(u_ref[...], hnt_ref[...], preferred_element_type=F32))
    tm = act.shape[1]
    rep = nk // 16

    def bcast_row(ref, hd, i1):
        row16 = jnp.broadcast_to(ref[hd, pl.ds(i1, 1), :], (16, tm)).astype(BF16)
        return jnp.concatenate([row16] * rep, axis=0)

    parts = []
    for j in range(PEER_NB):
        i1 = kk * PEER_NB + j
        w = jnp.zeros((nk, tm), BF16)
        for hd in range(PEER_HEADS):
            n1 = bcast_row(n1_ref, hd, i1)
            a1 = bcast_row(a1_ref, hd, i1)
            w = w + jnp.where(b2_ref[hd] < n1, e2_ref[hd] * a1, jnp.zeros((), BF16))
        parts.append(w * act[j * nk:(j + 1) * nk].astype(BF16))
    mt = jnp.concatenate(parts, axis=0)
    acc_ref[...] += jnp.dot(vt_ref[...], mt, preferred_element_type=F32)

    @pl.when(kk == pl.num_programs(1) - 1)
    def _():
        o_ref[...] = h_ref[...] + acc_ref[...].T


def _peer(h2d, gamma, u_bf16, vt_bf16, tables):
    t, d = h2d.shape
    tm = min(PEER_TM, t)
    nk = PEER_NKEYS
    ke = PEER_NB * nk
    tspec = pl.BlockSpec((PEER_HEADS, nk, tm), lambda i, k: (0, 0, i))
    return pl.pallas_call(
        _peer_kernel,
        out_shape=jax.ShapeDtypeStruct((t, d), F32),
        grid=(t // tm, nk // PEER_NB),
        in_specs=[
            pl.BlockSpec((tm, d), lambda i, k: (i, 0)),
            pl.BlockSpec((1, d), lambda i, k: (0, 0)),
            pl.BlockSpec((ke, d), lambda i, k: (k, 0)),
            pl.BlockSpec((d, ke), lambda i, k: (0, k)),
            tspec, tspec, tspec, tspec,
        ],
        out_specs=pl.BlockSpec((tm, d), lambda i, k: (i, 0)),
        scratch_shapes=[pltpu.VMEM((d, tm), BF16), pltpu.VMEM((d, tm), F32)],
        compiler_params=_cparams(("parallel", "arbitrary")),
        name="peer_experts",
    )(h2d, gamma.reshape(1, d), u_bf16, vt_bf16, *tables)


def _reorder_w_in(w_in):
    d = w_in.shape[0]
    nsa_w = NSA_HEADS * HEAD_DIM
    kv_w = NSA_GROUPS * HEAD_DIM
    sb_w = SB_HEADS * HEAD_DIM
    c0 = nsa_w
    c1 = c0 + 6 * kv_w
    c2 = c1 + 3 * NSA_HEADS
    c3 = c2 + 3 * sb_w
    gn = w_in[:, c1:c2].reshape(d, 3, NSA_GROUPS, NSA_HPG).transpose(0, 2, 1, 3)
    gn = jnp.pad(gn.reshape(d, NSA_GROUPS, 3 * NSA_HPG), ((0, 0), (0, 0), (0, 16 - 3 * NSA_HPG)))
    gn = jnp.pad(gn.reshape(d, NSA_GROUPS * 16), ((0, 0), (0, LANES - NSA_GROUPS * 16)))
    return jnp.concatenate([w_in[:, c3:], w_in[:, :c0], w_in[:, c0:c1], w_in[:, c2:c3], gn], axis=1).astype(BF16)


def _overlap_t(seq):
    ncp = seq // CMP_STRIDE
    nc = (seq - CMP_LEN) // CMP_STRIDE + 1
    ns = seq // SEL_BLOCK
    c_start = jnp.arange(ncp) * CMP_STRIDE
    j_start = jnp.arange(ns) * SEL_BLOCK
    ov = jnp.clip(jnp.minimum(c_start[None, :] + CMP_LEN, j_start[:, None] + SEL_BLOCK)
                  - jnp.maximum(c_start[None, :], j_start[:, None]), 0).astype(F32) / CMP_LEN
    return jnp.where(jnp.arange(ncp)[None, :] < nc, ov, 0.0).astype(BF16)


def _token_mixers(x, norm1_g, w_in, q_norm_g, k_norm_g, cmp_pos_k, cmp_pos_v, cmp_k_w1, cmp_k_w2,
                  cmp_v_w1, cmp_v_w2, w_branch_nsa, w_branch_sb, w_out):
    batch, seq, d = x.shape
    t = batch * seq
    x2d = x.reshape(t, d)
    scale = 1.0 / math.sqrt(HEAD_DIM)

    proj = _norm_matmul(x2d, norm1_g, _reorder_w_in(w_in), tm=512, tn=13 * LANES, name="in_proj")

    cos, sin = _rope_tables(jnp.arange(seq))
    qg = jnp.tile(q_norm_g.astype(F32), 2) * scale
    gammas = jnp.stack([qg] * 4 + [jnp.tile(k_norm_g[1].astype(F32), 2), jnp.tile(k_norm_g[2].astype(F32), 2)])
    qk = _prep_norm_rope(proj, cos, sin, gammas.reshape(6, 1, LANES), seq,
                         ts=min(1024, seq)).reshape(batch, seq, 6 * LANES)
    tposed = _prep_transpose(proj, batch, seq, ts=min(2048, seq))

    ncp = seq // CMP_STRIDE
    cosc, sinc = _rope_tables(jnp.arange(ncp) * CMP_STRIDE + CMP_LEN - 1)
    kc = _compress(proj[:, CB_KC * LANES:(CB_KC + 1) * LANES], cmp_pos_k, cmp_k_w1, cmp_k_w2, batch, seq,
                   key_args=(cosc, sinc, jnp.tile(k_norm_g[0].astype(F32), 2).reshape(1, LANES)))
    vct = _compress(proj[:, CB_VC * LANES:(CB_VC + 1) * LANES], cmp_pos_v, cmp_v_w1, cmp_v_w2, batch, seq)

    ont = _nsa(qk, kc, vct, tposed, _overlap_t(seq), batch, seq)
    ost = _sb(proj.reshape(batch, seq, N_CB * LANES), tposed, batch, seq)
    return _merge(ont, ost, proj, x2d, w_branch_nsa.astype(BF16), w_branch_sb.astype(BF16),
                  w_out.astype(BF16), batch, seq, tm=256)


def kernel(x, norm1_g, w_in, q_norm_g, k_norm_g, cmp_pos_k, cmp_pos_v, cmp_k_w1, cmp_k_w2, cmp_v_w1, cmp_v_w2,
           w_branch_nsa, w_branch_sb, w_out, norm2_g, peer_w_query, peer_subkeys, peer_u, peer_v):
    batch, seq, d = x.shape
    h = x.astype(F32)
    for l in range(norm1_g.shape[0]):
        h2d = _token_mixers(h, norm1_g[l], w_in[l], q_norm_g[l], k_norm_g[l], cmp_pos_k[l], cmp_pos_v[l],
                            cmp_k_w1[l], cmp_k_w2[l], cmp_v_w1[l], cmp_v_w2[l],
                            w_branch_nsa[l], w_branch_sb[l], w_out[l])
        qp = _norm_matmul(h2d, norm2_g[l], peer_w_query[l].astype(BF16), tm=512, tn=1024, name="peer_query")
        tables = _route(qp, peer_subkeys[l])
        h2d = _peer(h2d, norm2_g[l], peer_u[l].astype(BF16), peer_v[l].T.astype(BF16), tables)
        h = h2d.reshape(batch, seq, d)
    return h.astype(x.dtype)
```

```python
import functools
import math

import jax
import jax.numpy as jnp
from jax import lax
from jax.experimental import pallas as pl
from jax.experimental.pallas import tpu as pltpu

F32 = jnp.float32
BF16 = jnp.bfloat16

HEAD_DIM = 64
NSA_HEADS = 8
NSA_GROUPS = 2
NSA_HPG = NSA_HEADS // NSA_GROUPS
CMP_LEN = 32
CMP_STRIDE = 16
CMP_HIDDEN = 256
SEL_BLOCK = 64
SEL_TOPN = 16
WINDOW = 512
SB_HEADS = 8
ROPE_THETA = 10000.0
PEER_HEADS = 8
PEER_NKEYS = 128
PEER_QDIM = 256
PEER_TOPK = 16
EPS = 1e-6
NEG = -1e30

LANES = 128
VMEM_LIMIT = 56 * 1024 * 1024

CB_GATE = 0
CB_Q = 16
CB_KC, CB_VC, CB_KS, CB_VS, CB_KW, CB_VW = 20, 21, 22, 23, 24, 25
CB_QSB, CB_KSB, CB_VSB = 26, 30, 34
CB_GN = 38
N_CB = 39

SB_EXIT = 104.0


def _cparams(sem, vmem=None):
    return pltpu.CompilerParams(dimension_semantics=sem, vmem_limit_bytes=vmem or VMEM_LIMIT)


def _norm_matmul_kernel(x_ref, g_ref, w_ref, o_ref):
    x = x_ref[...]
    ms = jnp.mean(x * x, axis=-1, keepdims=True)
    xn = (x * lax.rsqrt(ms + EPS) * g_ref[...]).astype(BF16)
    o_ref[...] = jnp.dot(xn, w_ref[...], preferred_element_type=F32)


def _norm_matmul(x2d, gamma, w_bf16, tm, tn, name):
    t, d = x2d.shape
    n = w_bf16.shape[1]
    return pl.pallas_call(
        _norm_matmul_kernel,
        out_shape=jax.ShapeDtypeStruct((t, n), F32),
        grid=(n // tn, t // tm),
        in_specs=[
            pl.BlockSpec((tm, d), lambda j, i: (i, 0)),
            pl.BlockSpec((1, d), lambda j, i: (0, 0)),
            pl.BlockSpec((d, tn), lambda j, i: (0, j)),
        ],
        out_specs=pl.BlockSpec((tm, tn), lambda j, i: (i, j)),
        compiler_params=_cparams(("parallel", "parallel")),
        name=name,
    )(x2d, gamma.reshape(1, d), w_bf16)


def _group_mean_sq(x):
    ss = x * x
    r = lax.broadcasted_iota(jnp.int32, (LANES, LANES), 0) // HEAD_DIM
    c = lax.broadcasted_iota(jnp.int32, (LANES, LANES), 1) // HEAD_DIM
    ones_bd = jnp.where(r == c, 1.0, 0.0).astype(BF16)
    hi = ss.astype(BF16)
    lo = (ss - hi.astype(F32)).astype(BF16)
    tot = (jnp.dot(hi, ones_bd, preferred_element_type=F32)
           + jnp.dot(lo, ones_bd, preferred_element_type=F32))
    return tot * (1.0 / HEAD_DIM)


def _norm_rope(x, cos, sin_signed, gamma):
    xn = x * lax.rsqrt(_group_mean_sq(x) + EPS) * gamma
    lane = lax.broadcasted_iota(jnp.int32, xn.shape, 1)
    first_half = (lane % HEAD_DIM) < (HEAD_DIM // 2)
    swapped = jnp.where(first_half, pltpu.roll(xn, LANES - HEAD_DIM // 2, 1), pltpu.roll(xn, HEAD_DIM // 2, 1))
    return xn * cos + swapped * sin_signed


def _rope_tables(pos):
    half = HEAD_DIM // 2
    d = jnp.arange(LANES) % HEAD_DIM
    freqs = ROPE_THETA ** (-(d % half).astype(F32) / half)
    ang = pos.astype(F32)[:, None] * freqs[None, :]
    sign = jnp.where(d < half, -1.0, 1.0).astype(F32)
    return jnp.cos(ang), jnp.sin(ang) * sign[None, :]


def _norm_rope_kernel(x_ref, cos_ref, sin_ref, g_ref, o_ref):
    o_ref[...] = _norm_rope(x_ref[...], cos_ref[...], sin_ref[...], g_ref[0]).astype(o_ref.dtype)


def _prep_norm_rope(proj, cos, sin, gammas, seq, ts):
    t = proj.shape[0]
    n_slab = gammas.shape[0]
    s_tiles = seq // ts
    return pl.pallas_call(
        _norm_rope_kernel,
        out_shape=jax.ShapeDtypeStruct((t, n_slab * LANES), BF16),
        grid=(t // ts, n_slab),
        in_specs=[
            pl.BlockSpec((ts, LANES), lambda i, j: (i, CB_Q + j + 2 * (j // 4) + (j // 5))),
            pl.BlockSpec((ts, LANES), lambda i, j: (i % s_tiles, 0)),
            pl.BlockSpec((ts, LANES), lambda i, j: (i % s_tiles, 0)),
            pl.BlockSpec((1, 1, LANES), lambda i, j: (j, 0, 0)),
        ],
        out_specs=pl.BlockSpec((ts, LANES), lambda i, j: (i, j)),
        compiler_params=_cparams(("parallel", "parallel")),
        name="prep_norm_rope",
    )(proj, cos, sin, gammas)


T_SLABS = (CB_VS, CB_VW, CB_QSB, CB_QSB + 1, CB_QSB + 2, CB_QSB + 3,
           CB_VSB, CB_VSB + 1, CB_VSB + 2, CB_VSB + 3, CB_GN)
TS_VS, TS_VW, TS_QSB, TS_VSB, TS_GN = 0, 1, 2, 6, 10


def _transpose_kernel(cm_ref, x_ref, o_ref):
    del cm_ref
    o_ref[0] = x_ref[...].T.astype(o_ref.dtype)


def _prep_transpose(proj, batch, seq, ts):
    s_tiles = seq // ts
    colmap = jnp.asarray(T_SLABS, jnp.int32)
    return pl.pallas_call(
        _transpose_kernel,
        out_shape=jax.ShapeDtypeStruct((batch, len(T_SLABS) * LANES, seq), BF16),
        grid_spec=pltpu.PrefetchScalarGridSpec(
            num_scalar_prefetch=1,
            grid=(batch, s_tiles, len(T_SLABS)),
            in_specs=[pl.BlockSpec((ts, LANES), lambda b, i, j, cm: (b * s_tiles + i, cm[j]))],
            out_specs=pl.BlockSpec((1, LANES, ts), lambda b, i, j, cm: (b, j, i)),
        ),
        compiler_params=_cparams(("parallel", "parallel", "parallel")),
        name="prep_transpose",
    )(colmap, proj)


def _compress_kernel(a_ref, pa_ref, pb_ref, w1a_ref, w1b_ref, w2_ref, *rest, is_key):
    a = a_ref[0]
    ncp = a.shape[0]
    p = jnp.dot((a + pa_ref[...]).astype(BF16), w1a_ref[...], preferred_element_type=F32)
    q = jnp.dot((a + pb_ref[...]).astype(BF16), w1b_ref[...], preferred_element_type=F32)
    hid = jax.nn.gelu(p + pltpu.roll(q, ncp - 1, 0))
    if is_key:
        cos_ref, sin_ref, g_ref, o_ref = rest
        out = jnp.dot(hid.astype(BF16), w2_ref[...], preferred_element_type=F32)
        o_ref[0] = _norm_rope(out, cos_ref[...], sin_ref[...], g_ref[...]).astype(o_ref.dtype)
    else:
        (o_ref,) = rest
        nt_dims = (((1,), (1,)), ((), ()))
        out_t = lax.dot_general(w2_ref[...], hid.astype(BF16), nt_dims, preferred_element_type=F32)
        o_ref[0] = out_t.astype(o_ref.dtype)


def _expand_cmp_weights(w1, w2):
    eye = jnp.eye(NSA_GROUPS, dtype=F32)
    w1r = w1.reshape(2, CMP_STRIDE, HEAD_DIM, CMP_HIDDEN)
    ex = w1r[:, :, None, :, None, :] * eye[None, None, :, None, :, None]
    ex = ex.reshape(2, CMP_STRIDE * LANES, NSA_GROUPS * CMP_HIDDEN).astype(BF16)
    w2x = (w2[None, :, None, :] * eye[:, None, :, None]).reshape(NSA_GROUPS * CMP_HIDDEN, LANES).astype(BF16)
    return ex[0], ex[1], w2x


def _compress(tok_slab, pos_emb, w1, w2, batch, seq, key_args=None):
    ncp = seq // CMP_STRIDE
    a = tok_slab.reshape(batch, ncp, CMP_STRIDE * LANES)
    pos2 = jnp.tile(pos_emb.astype(F32)[:, None, :], (1, NSA_GROUPS, 1)).reshape(2, 1, CMP_STRIDE * LANES)
    w1a, w1b, w2x = _expand_cmp_weights(w1, w2)
    kdim = CMP_STRIDE * LANES
    hdim = NSA_GROUPS * CMP_HIDDEN
    if key_args is None:
        w2x = w2x.T
    in_specs = [
        pl.BlockSpec((1, ncp, kdim), lambda b: (b, 0, 0)),
        pl.BlockSpec((1, kdim), lambda b: (0, 0)),
        pl.BlockSpec((1, kdim), lambda b: (0, 0)),
        pl.BlockSpec((kdim, hdim), lambda b: (0, 0)),
        pl.BlockSpec((kdim, hdim), lambda b: (0, 0)),
        pl.BlockSpec(w2x.shape, lambda b: (0, 0)),
    ]
    args = [a, pos2[0], pos2[1], w1a, w1b, w2x]
    if key_args is not None:
        cos, sin, gamma = key_args
        in_specs += [pl.BlockSpec((ncp, LANES), lambda b: (0, 0)),
                     pl.BlockSpec((ncp, LANES), lambda b: (0, 0)),
                     pl.BlockSpec((1, LANES), lambda b: (0, 0))]
        args += [cos, sin, gamma]
        out_shape = jax.ShapeDtypeStruct((batch, ncp, LANES), BF16)
        out_spec = pl.BlockSpec((1, ncp, LANES), lambda b: (b, 0, 0))
    else:
        out_shape = jax.ShapeDtypeStruct((batch, LANES, ncp), BF16)
        out_spec = pl.BlockSpec((1, LANES, ncp), lambda b: (b, 0, 0))
    return pl.pallas_call(
        functools.partial(_compress_kernel, is_key=key_args is not None),
        out_shape=out_shape,
        grid=(batch,),
        in_specs=in_specs,
        out_specs=out_spec,
        compiler_params=_cparams(("parallel",)),
        name="compress_k" if key_args is not None else "compress_v",
    )(*args)


NSA_TQ = 128
NSA_KB = 1024


def _flash_step(k_tile, vt_tile, qtm, bias, carry):
    m, l, acc = carry
    s = jnp.dot(k_tile, qtm, preferred_element_type=F32) + bias
    m_new = jnp.maximum(m, jnp.max(s, axis=0, keepdims=True))
    alpha = jnp.exp(m - m_new)
    p = jnp.exp(s - m_new).astype(BF16)
    vt_ones = jnp.concatenate([vt_tile, jnp.ones((16, vt_tile.shape[1]), BF16)], axis=0)
    pv = jnp.dot(vt_ones, p, preferred_element_type=F32)
    l = alpha * l + pv[HEAD_DIM:HEAD_DIM + 1]
    acc = alpha * acc + pv[:HEAD_DIM]
    return m_new, l, acc


def _nsa_kernel(q_ref, kc_ref, vct_ref, ks_ref, kw_ref, vst_ref, vwt_ref, gate_ref, ovt_ref,
                o_ref, selb_ref, *, n_sel):
    g = pl.program_id(1)
    i = pl.program_id(2)
    tq, kb, hpg = NSA_TQ, NSA_KB, NSA_HPG
    n = hpg * tq
    t0 = i * tq
    grow = pl.multiple_of(g * HEAD_DIM, HEAD_DIM)

    qt = q_ref[0].astype(F32).T
    rowgrp = lax.broadcasted_iota(jnp.int32, (LANES, tq), 0) // HEAD_DIM
    parts = []
    for h in range(hpg):
        blk = qt[h * HEAD_DIM:(h + 1) * HEAD_DIM]
        parts.append(jnp.where(rowgrp == g, jnp.concatenate([blk, blk], axis=0), 0.0))
    qtm = jnp.concatenate(parts, axis=1).astype(BF16)

    tok1 = t0 + lax.broadcasted_iota(jnp.int32, (1, tq), 1)
    tokn = jnp.concatenate([tok1] * hpg, axis=1)

    ncp = kc_ref.shape[1]
    sc = jnp.dot(kc_ref[0], qtm, preferred_element_type=F32)
    cend = lax.broadcasted_iota(jnp.int32, (ncp, n), 0) * CMP_STRIDE + (CMP_LEN - 1)
    sc = jnp.where(cend <= tokn, sc, NEG)
    mc = jnp.max(sc, axis=0, keepdims=True)
    ec = jnp.exp(sc - mc)
    has_c = jnp.where(tokn >= CMP_LEN - 1, 1.0, 0.0)
    pc = ec * (has_c / jnp.sum(ec, axis=0, keepdims=True))
    o_cmp = jnp.dot(vct_ref[0, pl.ds(grow, HEAD_DIM), :], pc.astype(BF16), preferred_element_type=F32)

    psum = pc[:, 0:tq]
    for h in range(1, hpg):
        psum = psum + pc[:, h * tq:(h + 1) * tq]
    p_hi = psum.astype(BF16)
    p_lo = (psum - p_hi.astype(F32)).astype(BF16)
    imp = (jnp.dot(ovt_ref[...], p_hi, preferred_element_type=F32)
           + jnp.dot(ovt_ref[...], p_lo, preferred_element_type=F32))
    ns = imp.shape[0]
    jidx = lax.broadcasted_iota(jnp.int32, (ns, tq), 0)
    jf = jidx.astype(F32)
    tokb = jnp.broadcast_to(tok1, (ns, tq))
    cur = tokb // SEL_BLOCK
    forced = (jidx == 0) | (jidx == cur) | (jidx == cur - 1)
    valid = jidx * SEL_BLOCK <= tokb
    score = jnp.where(forced, 1e4, jnp.where(valid, imp, -1e4))
    selb = jnp.full((ns, tq), NEG, F32)
    for _ in range(n_sel):
        mx = jnp.max(score, axis=0, keepdims=True)
        first = jnp.min(jnp.where(score == mx, jf, float(ns)), axis=0, keepdims=True)
        pick = jf == first
        selb = jnp.where(pick, 0.0, selb)
        score = jnp.where(pick, -jnp.inf, score)
    selb_ref[...] = selb

    wlen = WINDOW + tq
    w0 = pl.multiple_of(jnp.maximum(t0 - WINDOW, 0), tq)
    wpos = w0 + lax.broadcasted_iota(jnp.int32, (wlen, tq), 0)
    wbias1 = jnp.where((wpos <= tok1) & (wpos > tok1 - WINDOW), 0.0, NEG)
    sw = (jnp.dot(kw_ref[0, pl.ds(w0, wlen), :], qtm, preferred_element_type=F32)
          + jnp.concatenate([wbias1] * hpg, axis=1))
    pw = jnp.exp(sw - jnp.max(sw, axis=0, keepdims=True)).astype(BF16)
    vw_ones = jnp.concatenate([vwt_ref[0, pl.ds(grow, HEAD_DIM), pl.ds(w0, wlen)],
                               jnp.ones((16, wlen), BF16)], axis=0)
    pvw = jnp.dot(vw_ones, pw, preferred_element_type=F32)
    o_win = pvw[:HEAD_DIM] / pvw[HEAD_DIM:HEAD_DIM + 1]

    krow = lax.broadcasted_iota(jnp.int32, (kb, tq), 0)
    init = (jnp.full((1, n), NEG, F32), jnp.zeros((1, n), F32), jnp.zeros((HEAD_DIM, n), F32))
    nblk = kb // SEL_BLOCK
    def sel_body(k, carry):
        k0 = pl.multiple_of(k * kb, kb)
        rows = [jnp.broadcast_to(selb_ref[pl.ds(k * nblk + r, 1), :], (SEL_BLOCK, tq)) for r in range(nblk)]
        bias1 = jnp.where(k0 + krow <= tok1, jnp.concatenate(rows, axis=0), NEG)
        bias = jnp.concatenate([bias1] * hpg, axis=1)
        return _flash_step(ks_ref[0, pl.ds(k0, kb), :], vst_ref[0, pl.ds(grow, HEAD_DIM), pl.ds(k0, kb)],
                           qtm, bias, carry)
    _, l_s, acc_s = lax.fori_loop(0, t0 // kb + 1, sel_body, init)

    gt = jax.nn.sigmoid(gate_ref[0, pl.ds(pl.multiple_of(g * 16, 16), 16), :].astype(F32))
    o_sel = acc_s / l_s
    for h in range(hpg):
        sl = slice(h * tq, (h + 1) * tq)
        o_h = (gt[h:h + 1] * o_cmp[:, sl] + gt[hpg + h:hpg + h + 1] * o_sel[:, sl]
               + gt[2 * hpg + h:2 * hpg + h + 1] * o_win[:, sl])
        o_ref[0, h * HEAD_DIM:(h + 1) * HEAD_DIM, :] = o_h.astype(o_ref.dtype)


def _nsa(qk, kc, vct, tposed, ovt, batch, seq):
    tq = NSA_TQ
    ncp = seq // CMP_STRIDE
    ns = seq // SEL_BLOCK
    n_sel = min(SEL_TOPN, ns)
    qw = NSA_HPG * HEAD_DIM
    return pl.pallas_call(
        functools.partial(_nsa_kernel, n_sel=n_sel),
        out_shape=jax.ShapeDtypeStruct((batch, NSA_HEADS * HEAD_DIM, seq), BF16),
        grid=(batch, NSA_GROUPS, seq // tq),
        in_specs=[
            pl.BlockSpec((1, tq, qw), lambda b, g, i: (b, i, g)),
            pl.BlockSpec((1, ncp, LANES), lambda b, g, i: (b, 0, 0)),
            pl.BlockSpec((1, LANES, ncp), lambda b, g, i: (b, 0, 0)),
            pl.BlockSpec((1, seq, LANES), lambda b, g, i: (b, 0, 4)),
            pl.BlockSpec((1, seq, LANES), lambda b, g, i: (b, 0, 5)),
            pl.BlockSpec((1, LANES, seq), lambda b, g, i: (b, TS_VS, 0)),
            pl.BlockSpec((1, LANES, seq), lambda b, g, i: (b, TS_VW, 0)),
            pl.BlockSpec((1, LANES, tq), lambda b, g, i: (b, TS_GN, i)),
            pl.BlockSpec((ns, ncp), lambda b, g, i: (0, 0)),
        ],
        out_specs=pl.BlockSpec((1, qw, tq), lambda b, g, i: (b, g, i)),
        scratch_shapes=[pltpu.VMEM((ns, tq), F32)],
        compiler_params=_cparams(("parallel", "parallel", "arbitrary")),
        name="nsa",
    )(qk, kc, vct, qk, qk, tposed, tposed, tposed, ovt)


SB_TQ = 256


def _sb_kernel(qt_ref, k_ref, vt_ref, o_ref):
    i = pl.program_id(2)
    tq = SB_TQ
    kb = tq
    n = 2 * tq
    scale = 1.0 / math.sqrt(HEAD_DIM)

    q2 = qt_ref[0].astype(F32) * scale
    rowgrp = lax.broadcasted_iota(jnp.int32, (LANES, tq), 0) // HEAD_DIM
    qtm = jnp.concatenate([jnp.where(rowgrp == 0, q2, 0.0), jnp.where(rowgrp == 1, q2, 0.0)],
                          axis=1).astype(BF16)

    r = lax.broadcasted_iota(jnp.int32, (kb, kb), 0)
    c = lax.broadcasted_iota(jnp.int32, (kb, kb), 1)
    later = jnp.where(c > r, 1.0, 0.0).astype(BF16)
    krow = lax.broadcasted_iota(jnp.int32, (kb, n), 0)
    tcol = lax.broadcasted_iota(jnp.int32, (kb, n), 1) % tq

    def body(st):
        k, carry, acc_a, acc_b, _ = st
        k0 = pl.multiple_of(k * kb, kb)
        z = jnp.dot(k_ref[0, pl.ds(k0, kb), :].astype(BF16), qtm, preferred_element_type=F32)
        sp = jnp.maximum(z, 0.0) + jnp.log1p(jnp.exp(-jnp.abs(z)))
        mask = (k0 + krow) < (i * tq + tcol)
        spm = jnp.where(mask, sp, 0.0)
        hi = spm.astype(BF16)
        lo = (spm - hi.astype(F32)).astype(BF16)
        after = (jnp.dot(later, hi, preferred_element_type=F32)
                 + jnp.dot(later, lo, preferred_element_type=F32))
        a = jnp.where(mask, jnp.exp(z - sp - after - carry), 0.0).astype(BF16)
        acc_a = acc_a + jnp.dot(vt_ref[0, :HEAD_DIM, pl.ds(k0, kb)], a[:, :tq], preferred_element_type=F32)
        acc_b = acc_b + jnp.dot(vt_ref[0, HEAD_DIM:, pl.ds(k0, kb)], a[:, tq:], preferred_element_type=F32)
        carry = carry + after[0:1] + spm[0:1]
        go = jnp.logical_and(k > 0, jnp.min(carry) <= SB_EXIT)
        return k - 1, carry, acc_a, acc_b, go

    zacc = jnp.zeros((HEAD_DIM, tq), F32)
    st = (i, jnp.zeros((1, n), F32), zacc, zacc, i >= 0)
    _, _, acc_a, acc_b, _ = lax.while_loop(lambda s: s[4], body, st)
    o_ref[0, :HEAD_DIM, :] = acc_a.astype(o_ref.dtype)
    o_ref[0, HEAD_DIM:, :] = acc_b.astype(o_ref.dtype)


def _sb(proj3, tposed, batch, seq):
    tq = SB_TQ
    return pl.pallas_call(
        _sb_kernel,
        out_shape=jax.ShapeDtypeStruct((batch, SB_HEADS * HEAD_DIM, seq), BF16),
        grid=(batch, SB_HEADS // 2, seq // tq),
        in_specs=[
            pl.BlockSpec((1, LANES, tq), lambda b, h, i: (b, TS_QSB + h, i)),
            pl.BlockSpec((1, seq, LANES), lambda b, h, i: (b, 0, CB_KSB + h)),
            pl.BlockSpec((1, LANES, seq), lambda b, h, i: (b, TS_VSB + h, 0)),
        ],
        out_specs=pl.BlockSpec((1, LANES, tq), lambda b, h, i: (b, h, i)),
        compiler_params=_cparams(("parallel", "parallel", "arbitrary")),
        name="stickbreaking",
    )(tposed, proj3, tposed)


def _merge_kernel(ont_ref, ost_ref, ga_ref, gb_ref, x_ref, wa_ref, wb_ref, wo_ref, o_ref):
    tn_dims = (((0,), (0,)), ((), ()))
    a = lax.dot_general(ont_ref[0], wa_ref[...], tn_dims, preferred_element_type=F32)
    b = lax.dot_general(ost_ref[0], wb_ref[...], tn_dims, preferred_element_type=F32)
    merged = jax.nn.sigmoid(ga_ref[...]) * a + jax.nn.sigmoid(gb_ref[...]) * b
    o_ref[...] = x_ref[...] + jnp.dot(merged.astype(BF16), wo_ref[...], preferred_element_type=F32)


def _merge(ont, ost, proj, x2d, wa, wb, wo, batch, seq, tm):
    d = x2d.shape[1]
    s_tiles = seq // tm
    hw = ont.shape[1]
    return pl.pallas_call(
        _merge_kernel,
        out_shape=jax.ShapeDtypeStruct(x2d.shape, F32),
        grid=(batch, s_tiles),
        in_specs=[
            pl.BlockSpec((1, hw, tm), lambda b, i: (b, 0, i)),
            pl.BlockSpec((1, hw, tm), lambda b, i: (b, 0, i)),
            pl.BlockSpec((tm, d), lambda b, i: (b * s_tiles + i, 0)),
            pl.BlockSpec((tm, d), lambda b, i: (b * s_tiles + i, 1)),
            pl.BlockSpec((tm, d), lambda b, i: (b * s_tiles + i, 0)),
            pl.BlockSpec((hw, d), lambda b, i: (0, 0)),
            pl.BlockSpec((hw, d), lambda b, i: (0, 0)),
            pl.BlockSpec((d, d), lambda b, i: (0, 0)),
        ],
        out_specs=pl.BlockSpec((tm, d), lambda b, i: (b * s_tiles + i, 0)),
        compiler_params=_cparams(("parallel", "parallel")),
        name="merge",
    )(ont, ost, proj, proj, x2d, wa, wb, wo)


PEER_TR = 256


def _topk_ranked(s, idx, k, vals_ref=None, first_ref=None, want_rank=True):
    rank = jnp.full(s.shape, float(k), F32) if want_rank else None
    for r in range(k):
        mx = jnp.max(s, axis=0, keepdims=True)
        first = jnp.min(jnp.where(s == mx, idx, 1e9), axis=0, keepdims=True)
        pick = idx == first
        if want_rank:
            rank = jnp.where(pick, float(r), rank)
        s = jnp.where(pick, -jnp.inf, s)
        if vals_ref is not None:
            vals_ref[r:r + 1, :] = mx
        if first_ref is not None:
            first_ref[r:r + 1, :] = first
    return rank


_CAND_GROUPS = ((0, 0, 16), (16, 1, 8), (24, 2, 8), (32, 3, 4), (36, 4, 4), (40, 5, 2), (42, 6, 2), (44, 7, 2))
_CAND_ROWS = 56


def _route_kernel(q_ref, sk_ref, n1_ref, a1_ref, b2_ref, e2_ref, v1_ref, v2_ref, f1_ref, cand_ref):
    k = PEER_TOPK
    half = PEER_QDIM // 2
    nt_dims = (((1,), (1,)), ((), ()))
    qh = q_ref[...].astype(BF16)
    s1 = lax.dot_general(sk_ref[0].astype(BF16), qh[:, :half], nt_dims, preferred_element_type=F32)
    s2 = lax.dot_general(sk_ref[1].astype(BF16), qh[:, half:], nt_dims, preferred_element_type=F32)
    tr = s1.shape[1]
    kidx = lax.broadcasted_iota(jnp.int32, s1.shape, 0).astype(F32)
    _topk_ranked(s1, kidx, k, v1_ref, f1_ref, want_rank=False)
    rank2 = _topk_ranked(s2, kidx, k, v2_ref)
    v1 = v1_ref[...]
    v2 = v2_ref[...]

    flat = lax.broadcasted_iota(jnp.int32, (_CAND_ROWS, tr), 0)
    row = flat
    for r0, a, nb in _CAND_GROUPS:
        cand_ref[r0:r0 + nb, :] = v1[a:a + 1] + v2[0:nb]
        flat = jnp.where((row >= r0) & (row < r0 + nb), row - r0 + a * k, flat)
    cand_ref[46:48, :] = jnp.full((2, tr), -jnp.inf, F32)
    cand_ref[48:56, :] = v1[8:16] + v2[0:1]
    flat = jnp.where(row >= 48, (row - 40) * k, flat).astype(F32)
    cand = cand_ref[...]
    crank = _topk_ranked(cand, flat, k)
    selc = jnp.where(crank < float(k), 1.0, 0.0)
    mx = v1[0:1] + v2[0:1]
    z = jnp.sum(jnp.where(crank < float(k), jnp.exp(cand - mx), 0.0), axis=0, keepdims=True)
    n_a = [jnp.sum(selc[r0:r0 + nb], axis=0, keepdims=True) for r0, _, nb in _CAND_GROUPS]
    n_a += [selc[48 + a - 8:48 + a - 7] for a in range(8, k)]

    f1 = f1_ref[...]
    n1 = jnp.zeros_like(s1)
    for a in range(k):
        n1 = jnp.where(kidx == f1[a:a + 1], n_a[a], n1)
    n1_ref[0] = n1
    a1_ref[0] = jnp.exp(s1 - v1[0:1]) / z
    b2_ref[0] = rank2.astype(b2_ref.dtype)
    e2_ref[0] = jnp.exp(s2 - v2[0:1]).astype(e2_ref.dtype)


def _route(qp, subkeys):
    t = qp.shape[0]
    tr = PEER_TR
    nk = PEER_NKEYS
    shp = jax.ShapeDtypeStruct((PEER_HEADS, nk, t), F32)
    shp16 = jax.ShapeDtypeStruct((PEER_HEADS, nk, t), BF16)
    spec = pl.BlockSpec((1, nk, tr), lambda i, h: (h, 0, i))
    return pl.pallas_call(
        _route_kernel,
        out_shape=(shp, shp, shp16, shp16),
        grid=(t // tr, PEER_HEADS),
        in_specs=[
            pl.BlockSpec((tr, PEER_QDIM), lambda i, h: (i, h)),
            pl.BlockSpec((2, nk, PEER_QDIM // 2), lambda i, h: (0, 0, 0)),
        ],
        out_specs=(spec, spec, spec, spec),
        scratch_shapes=[pltpu.VMEM((PEER_TOPK, tr), F32), pltpu.VMEM((PEER_TOPK, tr), F32),
                        pltpu.VMEM((PEER_TOPK, tr), F32), pltpu.VMEM((_CAND_ROWS, tr), F32)],
        compiler_params=_cparams(("parallel", "parallel")),
        name="peer_route",
    )(qp, subkeys)


PEER_TM = 512
PEER_NB = 8


def _peer_kernel(h_ref, g_ref, u_ref, vt_ref, n1_ref, a1_ref, b2_ref, e2_ref, o_ref, hnt_ref, acc_ref):
    kk = pl.program_id(1)
    nk = PEER_NKEYS

    @pl.when(kk == 0)
    def _():
        h = h_ref[...]
        ms = jnp.mean(h * h, axis=-1, keepdims=True)
        hnt_ref[...] = (h * lax.rsqrt(ms + EPS) * g_ref[...]).T.astype(BF16)
        acc_ref[...] = jnp.zeros_like(acc_ref)

    act = jax.nn.gelu(jnp.dot(u_ref[...], hnt_ref[...], preferred_element_type=F32))
    tm = act.shape[1]
    rep = nk // 16

    def bcast_row(ref, hd, i1):
        row16 = jnp.broadcast_to(ref[hd, pl.ds(i1, 1), :], (16, tm)).astype(BF16)
        return jnp.concatenate([row16] * rep, axis=0)

    parts = []
    for j in range(PEER_NB):
        i1 = kk * PEER_NB + j
        w = jnp.zeros((nk, tm), BF16)
        for hd in range(PEER_HEADS):
            n1 = bcast_row(n1_ref, hd, i1)
            a1 = bcast_row(a1_ref, hd, i1)
            w = w + jnp.where(b2_ref[hd] < n1, e2_ref[hd] * a1, jnp.zeros((), BF16))
        parts.append(w * act[j * nk:(j + 1) * nk].astype(BF16))
    mt = jnp.concatenate(parts, axis=0)
    acc_ref[...] += jnp.dot(vt_ref[...], mt, preferred_element_type=F32)

    @pl.when(kk == pl.num_programs(1) - 1)
    def _():
        o_ref[...] = h_ref[...] + acc_ref[...].T


def _peer(h2d, gamma, u_bf16, vt_bf16, tables):
    t, d = h2d.shape
    tm = min(PEER_TM, t)
    nk = PEER_NKEYS
    ke = PEER_NB * nk
    tspec = pl.BlockSpec((PEER_HEADS, nk, tm), lambda i, k: (0, 0, i))
    return pl.pallas_call(
        _peer_kernel,
        out_shape=jax.ShapeDtypeStruct((t, d), F32),
        grid=(t // tm, nk // PEER_NB),
        in_specs=[
            pl.BlockSpec((tm, d), lambda i, k: (i, 0)),
            pl.BlockSpec((1, d), lambda i, k: (0, 0)),
            pl.BlockSpec((ke, d), lambda i, k: (k, 0)),
            pl.BlockSpec((d, ke), lambda i, k: (0, k)),
            tspec, tspec, tspec, tspec,
        ],
        out_specs=pl.BlockSpec((tm, d), lambda i, k: (i, 0)),
        scratch_shapes=[pltpu.VMEM((d, tm), BF16), pltpu.VMEM((d, tm), F32)],
        compiler_params=_cparams(("parallel", "arbitrary")),
        name="peer_experts",
    )(h2d, gamma.reshape(1, d), u_bf16, vt_bf16, *tables)


def _reorder_w_in(w_in):
    d = w_in.shape[0]
    nsa_w = NSA_HEADS * HEAD_DIM
    kv_w = NSA_GROUPS * HEAD_DIM
    sb_w = SB_HEADS * HEAD_DIM
    c0 = nsa_w
    c1 = c0 + 6 * kv_w
    c2 = c1 + 3 * NSA_HEADS
    c3 = c2 + 3 * sb_w
    gn = w_in[:, c1:c2].reshape(d, 3, NSA_GROUPS, NSA_HPG).transpose(0, 2, 1, 3)
    gn = jnp.pad(gn.reshape(d, NSA_GROUPS, 3 * NSA_HPG), ((0, 0), (0, 0), (0, 16 - 3 * NSA_HPG)))
    gn = jnp.pad(gn.reshape(d, NSA_GROUPS * 16), ((0, 0), (0, LANES - NSA_GROUPS * 16)))
    return jnp.concatenate([w_in[:, c3:], w_in[:, :c0], w_in[:, c0:c1], w_in[:, c2:c3], gn], axis=1).astype(BF16)


def _overlap_t(seq):
    ncp = seq // CMP_STRIDE
    nc = (seq - CMP_LEN) // CMP_STRIDE + 1
    ns = seq // SEL_BLOCK
    c_start = jnp.arange(ncp) * CMP_STRIDE
    j_start = jnp.arange(ns) * SEL_BLOCK
    ov = jnp.clip(jnp.minimum(c_start[None, :] + CMP_LEN, j_start[:, None] + SEL_BLOCK)
                  - jnp.maximum(c_start[None, :], j_start[:, None]), 0).astype(F32) / CMP_LEN
    return jnp.where(jnp.arange(ncp)[None, :] < nc, ov, 0.0).astype(BF16)


def _token_mixers(x, norm1_g, w_in, q_norm_g, k_norm_g, cmp_pos_k, cmp_pos_v, cmp_k_w1, cmp_k_w2,
                  cmp_v_w1, cmp_v_w2, w_branch_nsa, w_branch_sb, w_out):
    batch, seq, d = x.shape
    t = batch * seq
    x2d = x.reshape(t, d)
    scale = 1.0 / math.sqrt(HEAD_DIM)

    proj = _norm_matmul(x2d, norm1_g, _reorder_w_in(w_in), tm=512, tn=13 * LANES, name="in_proj")

    cos, sin = _rope_tables(jnp.arange(seq))
    qg = jnp.tile(q_norm_g.astype(F32), 2) * scale
    gammas = jnp.stack([qg] * 4 + [jnp.tile(k_norm_g[1].astype(F32), 2), jnp.tile(k_norm_g[2].astype(F32), 2)])
    qk = _prep_norm_rope(proj, cos, sin, gammas.reshape(6, 1, LANES), seq,
                         ts=min(1024, seq)).reshape(batch, seq, 6 * LANES)
    tposed = _prep_transpose(proj, batch, seq, ts=min(2048, seq))

    ncp = seq // CMP_STRIDE
    cosc, sinc = _rope_tables(jnp.arange(ncp) * CMP_STRIDE + CMP_LEN - 1)
    kc = _compress(proj[:, CB_KC * LANES:(CB_KC + 1) * LANES], cmp_pos_k, cmp_k_w1, cmp_k_w2, batch, seq,
                   key_args=(cosc, sinc, jnp.tile(k_norm_g[0].astype(F32), 2).reshape(1, LANES)))
    vct = _compress(proj[:, CB_VC * LANES:(CB_VC + 1) * LANES], cmp_pos_v, cmp_v_w1, cmp_v_w2, batch, seq)

    ont = _nsa(qk, kc, vct, tposed, _overlap_t(seq), batch, seq)
    ost = _sb(proj.reshape(batch, seq, N_CB * LANES), tposed, batch, seq)
    return _merge(ont, ost, proj, x2d, w_branch_nsa.astype(BF16), w_branch_sb.astype(BF16),
                  w_out.astype(BF16), batch, seq, tm=256)


def kernel(x, norm1_g, w_in, q_norm_g, k_norm_g, cmp_pos_k, cmp_pos_v, cmp_k_w1, cmp_k_w2, cmp_v_w1, cmp_v_w2,
           w_branch_nsa, w_branch_sb, w_out, norm2_g, peer_w_query, peer_subkeys, peer_u, peer_v):
    batch, seq, d = x.shape
    h = x.astype(F32)
    for l in range(norm1_g.shape[0]):
        h2d = _token_mixers(h, norm1_g[l], w_in[l], q_norm_g[l], k_norm_g[l], cmp_pos_k[l], cmp_pos_v[l],
                            cmp_k_w1[l], cmp_k_w2[l], cmp_v_w1[l], cmp_v_w2[l],
                            w_branch_nsa[l], w_branch_sb[l], w_out[l])
        qp = _norm_matmul(h2d, norm2_g[l], peer_w_query[l].astype(BF16), tm=512, tn=1024, name="peer_query")
        tables = _route(qp, peer_subkeys[l])
        h2d = _peer(h2d, norm2_g[l], peer_u[l].astype(BF16), peer_v[l].T.astype(BF16), tables)
        h = h2d.reshape(batch, seq, d)
    return h.astype(x.dtype)
```

```python
import functools
import math

import jax
import jax.numpy as jnp
from jax import lax
from jax.experimental import pallas as pl
from jax.experimental.pallas import tpu as pltpu

F32 = jnp.float32
BF16 = jnp.bfloat16

HEAD_DIM = 64
NSA_HEADS = 8
NSA_GROUPS = 2
NSA_HPG = NSA_HEADS // NSA_GROUPS
CMP_LEN = 32
CMP_STRIDE = 16
CMP_HIDDEN = 256
SEL_BLOCK = 64
SEL_TOPN = 16
WINDOW = 512
SB_HEADS = 8
ROPE_THETA = 10000.0
PEER_HEADS = 8
PEER_NKEYS = 128
PEER_QDIM = 256
PEER_TOPK = 16
EPS = 1e-6
NEG = -1e30

LANES = 128
VMEM_LIMIT = 56 * 1024 * 1024

CB_GATE = 0
CB_Q = 16
CB_KC, CB_VC, CB_KS, CB_VS, CB_KW, CB_VW = 20, 21, 22, 23, 24, 25
CB_QSB, CB_KSB, CB_VSB = 26, 30, 34
CB_GN = 38
N_CB = 39

SB_EXIT = 104.0


def _cparams(sem, vmem=None):
    return pltpu.CompilerParams(dimension_semantics=sem, vmem_limit_bytes=vmem or VMEM_LIMIT)


def _norm_matmul_kernel(x_ref, g_ref, w_ref, o_ref):
    x = x_ref[...]
    ms = jnp.mean(x * x, axis=-1, keepdims=True)
    xn = (x * lax.rsqrt(ms + EPS) * g_ref[...]).astype(BF16)
    o_ref[...] = jnp.dot(xn, w_ref[...], preferred_element_type=F32)


def _norm_matmul(x2d, gamma, w_bf16, tm, tn, name):
    t, d = x2d.shape
    n = w_bf16.shape[1]
    return pl.pallas_call(
        _norm_matmul_kernel,
        out_shape=jax.ShapeDtypeStruct((t, n), F32),
        grid=(n // tn, t // tm),
        in_specs=[
            pl.BlockSpec((tm, d), lambda j, i: (i, 0)),
            pl.BlockSpec((1, d), lambda j, i: (0, 0)),
            pl.BlockSpec((d, tn), lambda j, i: (0, j)),
        ],
        out_specs=pl.BlockSpec((tm, tn), lambda j, i: (i, j)),
        compiler_params=_cparams(("parallel", "parallel")),
        name=name,
    )(x2d, gamma.reshape(1, d), w_bf16)


def _group_mean_sq(x):
    ss = x * x
    r = lax.broadcasted_iota(jnp.int32, (LANES, LANES), 0) // HEAD_DIM
    c = lax.broadcasted_iota(jnp.int32, (LANES, LANES), 1) // HEAD_DIM
    ones_bd = jnp.where(r == c, 1.0, 0.0).astype(BF16)
    hi = ss.astype(BF16)
    lo = (ss - hi.astype(F32)).astype(BF16)
    tot = (jnp.dot(hi, ones_bd, preferred_element_type=F32)
           + jnp.dot(lo, ones_bd, preferred_element_type=F32))
    return tot * (1.0 / HEAD_DIM)


def _norm_rope(x, cos, sin_signed, gamma):
    xn = x * lax.rsqrt(_group_mean_sq(x) + EPS) * gamma
    lane = lax.broadcasted_iota(jnp.int32, xn.shape, 1)
    first_half = (lane % HEAD_DIM) < (HEAD_DIM // 2)
    swapped = jnp.where(first_half, pltpu.roll(xn, LANES - HEAD_DIM // 2, 1), pltpu.roll(xn, HEAD_DIM // 2, 1))
    return xn * cos + swapped * sin_signed


def _rope_tables(pos):
    half = HEAD_DIM // 2
    d = jnp.arange(LANES) % HEAD_DIM
    freqs = ROPE_THETA ** (-(d % half).astype(F32) / half)
    ang = pos.astype(F32)[:, None] * freqs[None, :]
    sign = jnp.where(d < half, -1.0, 1.0).astype(F32)
    return jnp.cos(ang), jnp.sin(ang) * sign[None, :]


def _norm_rope_kernel(x_ref, cos_ref, sin_ref, g_ref, o_ref):
    o_ref[...] = _norm_rope(x_ref[...], cos_ref[...], sin_ref[...], g_ref[0]).astype(o_ref.dtype)


def _prep_norm_rope(proj, cos, sin, gammas, seq, ts):
    t = proj.shape[0]
    n_slab = gammas.shape[0]
    s_tiles = seq // ts
    return pl.pallas_call(
        _norm_rope_kernel,
        out_shape=jax.ShapeDtypeStruct((t, n_slab * LANES), BF16),
        grid=(t // ts, n_slab),
        in_specs=[
            pl.BlockSpec((ts, LANES), lambda i, j: (i, CB_Q + j + 2 * (j // 4) + (j // 5))),
            pl.BlockSpec((ts, LANES), lambda i, j: (i % s_tiles, 0)),
            pl.BlockSpec((ts, LANES), lambda i, j: (i % s_tiles, 0)),
            pl.BlockSpec((1, 1, LANES), lambda i, j: (j, 0, 0)),
        ],
        out_specs=pl.BlockSpec((ts, LANES), lambda i, j: (i, j)),
        compiler_params=_cparams(("parallel", "parallel")),
        name="prep_norm_rope",
    )(proj, cos, sin, gammas)


T_SLABS = (CB_VS, CB_VW, CB_QSB, CB_QSB + 1, CB_QSB + 2, CB_QSB + 3,
           CB_VSB, CB_VSB + 1, CB_VSB + 2, CB_VSB + 3, CB_GN)
TS_VS, TS_VW, TS_QSB, TS_VSB, TS_GN = 0, 1, 2, 6, 10


def _transpose_kernel(cm_ref, x_ref, o_ref):
    del cm_ref
    o_ref[0] = x_ref[...].T.astype(o_ref.dtype)


def _prep_transpose(proj, batch, seq, ts):
    s_tiles = seq // ts
    colmap = jnp.asarray(T_SLABS, jnp.int32)
    return pl.pallas_call(
        _transpose_kernel,
        out_shape=jax.ShapeDtypeStruct((batch, len(T_SLABS) * LANES, seq), BF16),
        grid_spec=pltpu.PrefetchScalarGridSpec(
            num_scalar_prefetch=1,
            grid=(batch, s_tiles, len(T_SLABS)),
            in_specs=[pl.BlockSpec((ts, LANES), lambda b, i, j, cm: (b * s_tiles + i, cm[j]))],
            out_specs=pl.BlockSpec((1, LANES, ts), lambda b, i, j, cm: (b, j, i)),
        ),
        compiler_params=_cparams(("parallel", "parallel", "parallel")),
        name="prep_transpose",
    )(colmap, proj)


def _compress_kernel(a_ref, pa_ref, pb_ref, w1a_ref, w1b_ref, w2_ref, *rest, is_key):
    a = a_ref[0]
    ncp = a.shape[0]
    p = jnp.dot((a + pa_ref[...]).astype(BF16), w1a_ref[...], preferred_element_type=F32)
    q = jnp.dot((a + pb_ref[...]).astype(BF16), w1b_ref[...], preferred_element_type=F32)
    hid = jax.nn.gelu(p + pltpu.roll(q, ncp - 1, 0))
    if is_key:
        cos_ref, sin_ref, g_ref, o_ref = rest
        out = jnp.dot(hid.astype(BF16), w2_ref[...], preferred_element_type=F32)
        o_ref[0] = _norm_rope(out, cos_ref[...], sin_ref[...], g_ref[...]).astype(o_ref.dtype)
    else:
        (o_ref,) = rest
        nt_dims = (((1,), (1,)), ((), ()))
        out_t = lax.dot_general(w2_ref[...], hid.astype(BF16), nt_dims, preferred_element_type=F32)
        o_ref[0] = out_t.astype(o_ref.dtype)


def _expand_cmp_weights(w1, w2):
    eye = jnp.eye(NSA_GROUPS, dtype=F32)
    w1r = w1.reshape(2, CMP_STRIDE, HEAD_DIM, CMP_HIDDEN)
    ex = w1r[:, :, None, :, None, :] * eye[None, None, :, None, :, None]
    ex = ex.reshape(2, CMP_STRIDE * LANES, NSA_GROUPS * CMP_HIDDEN).astype(BF16)
    w2x = (w2[None, :, None, :] * eye[:, None, :, None]).reshape(NSA_GROUPS * CMP_HIDDEN, LANES).astype(BF16)
    return ex[0], ex[1], w2x


def _compress(tok_slab, pos_emb, w1, w2, batch, seq, key_args=None):
    ncp = seq // CMP_STRIDE
    a = tok_slab.reshape(batch, ncp, CMP_STRIDE * LANES)
    pos2 = jnp.tile(pos_emb.astype(F32)[:, None, :], (1, NSA_GROUPS, 1)).reshape(2, 1, CMP_STRIDE * LANES)
    w1a, w1b, w2x = _expand_cmp_weights(w1, w2)
    kdim = CMP_STRIDE * LANES
    hdim = NSA_GROUPS * CMP_HIDDEN
    if key_args is None:
        w2x = w2x.T
    in_specs = [
        pl.BlockSpec((1, ncp, kdim), lambda b: (b, 0, 0)),
        pl.BlockSpec((1, kdim), lambda b: (0, 0)),
        pl.BlockSpec((1, kdim), lambda b: (0, 0)),
        pl.BlockSpec((kdim, hdim), lambda b: (0, 0)),
        pl.BlockSpec((kdim, hdim), lambda b: (0, 0)),
        pl.BlockSpec(w2x.shape, lambda b: (0, 0)),
    ]
    args = [a, pos2[0], pos2[1], w1a, w1b, w2x]
    if key_args is not None:
        cos, sin, gamma = key_args
        in_specs += [pl.BlockSpec((ncp, LANES), lambda b: (0, 0)),
                     pl.BlockSpec((ncp, LANES), lambda b: (0, 0)),
                     pl.BlockSpec((1, LANES), lambda b: (0, 0))]
        args += [cos, sin, gamma]
        out_shape = jax.ShapeDtypeStruct((batch, ncp, LANES), BF16)
        out_spec = pl.BlockSpec((1, ncp, LANES), lambda b: (b, 0, 0))
    else:
        out_shape = jax.ShapeDtypeStruct((batch, LANES, ncp), BF16)
        out_spec = pl.BlockSpec((1, LANES, ncp), lambda b: (b, 0, 0))
    return pl.pallas_call(
        functools.partial(_compress_kernel, is_key=key_args is not None),
        out_shape=out_shape,
        grid=(batch,),
        in_specs=in_specs,
        out_specs=out_spec,
        compiler_params=_cparams(("parallel",)),
        name="compress_k" if key_args is not None else "compress_v",
    )(*args)


NSA_TQ = 256
NSA_KB = 1024


def _flash_step(k_tile, vt_tile, qtm, bias, carry):
    m, l, acc = carry
    s = jnp.dot(k_tile, qtm, preferred_element_type=F32) + bias
    m_new = jnp.maximum(m, jnp.max(s, axis=0, keepdims=True))
    alpha = jnp.exp(m - m_new)
    p = jnp.exp(s - m_new).astype(BF16)
    vt_ones = jnp.concatenate([vt_tile, jnp.ones((16, vt_tile.shape[1]), BF16)], axis=0)
    pv = jnp.dot(vt_ones, p, preferred_element_type=F32)
    l = alpha * l + pv[HEAD_DIM:HEAD_DIM + 1]
    acc = alpha * acc + pv[:HEAD_DIM]
    return m_new, l, acc


def _nsa_kernel(q_ref, kc_ref, vct_ref, ks_ref, kw_ref, vst_ref, vwt_ref, gate_ref, ovt_ref,
                o_ref, selb_ref, *, n_sel):
    g = pl.program_id(1)
    i = pl.program_id(2)
    tq, kb, hpg = NSA_TQ, NSA_KB, NSA_HPG
    n = hpg * tq
    t0 = i * tq
    grow = pl.multiple_of(g * HEAD_DIM, HEAD_DIM)

    qt = q_ref[0].astype(F32).T
    rowgrp = lax.broadcasted_iota(jnp.int32, (LANES, tq), 0) // HEAD_DIM
    parts = []
    for h in range(hpg):
        blk = qt[h * HEAD_DIM:(h + 1) * HEAD_DIM]
        parts.append(jnp.where(rowgrp == g, jnp.concatenate([blk, blk], axis=0), 0.0))
    qtm = jnp.concatenate(parts, axis=1).astype(BF16)

    tok1 = t0 + lax.broadcasted_iota(jnp.int32, (1, tq), 1)
    tokn = jnp.concatenate([tok1] * hpg, axis=1)

    ncp = kc_ref.shape[1]
    sc = jnp.dot(kc_ref[0], qtm, preferred_element_type=F32)
    cend = lax.broadcasted_iota(jnp.int32, (ncp, n), 0) * CMP_STRIDE + (CMP_LEN - 1)
    sc = jnp.where(cend <= tokn, sc, NEG)
    mc = jnp.max(sc, axis=0, keepdims=True)
    ec = jnp.exp(sc - mc)
    has_c = jnp.where(tokn >= CMP_LEN - 1, 1.0, 0.0)
    pc = ec * (has_c / jnp.sum(ec, axis=0, keepdims=True))
    o_cmp = jnp.dot(vct_ref[0, pl.ds(grow, HEAD_DIM), :], pc.astype(BF16), preferred_element_type=F32)

    psum = pc[:, 0:tq]
    for h in range(1, hpg):
        psum = psum + pc[:, h * tq:(h + 1) * tq]
    p_hi = psum.astype(BF16)
    p_lo = (psum - p_hi.astype(F32)).astype(BF16)
    imp = (jnp.dot(ovt_ref[...], p_hi, preferred_element_type=F32)
           + jnp.dot(ovt_ref[...], p_lo, preferred_element_type=F32))
    ns = imp.shape[0]
    jidx = lax.broadcasted_iota(jnp.int32, (ns, tq), 0)
    jf = jidx.astype(F32)
    tokb = jnp.broadcast_to(tok1, (ns, tq))
    cur = tokb // SEL_BLOCK
    forced = (jidx == 0) | (jidx == cur) | (jidx == cur - 1)
    valid = jidx * SEL_BLOCK <= tokb
    score = jnp.where(forced, 1e4, jnp.where(valid, imp, -1e4))
    selb = jnp.full((ns, tq), NEG, F32)
    for _ in range(n_sel):
        mx = jnp.max(score, axis=0, keepdims=True)
        first = jnp.min(jnp.where(score == mx, jf, float(ns)), axis=0, keepdims=True)
        pick = jf == first
        selb = jnp.where(pick, 0.0, selb)
        score = jnp.where(pick, -jnp.inf, score)
    selb_ref[...] = selb

    wlen = WINDOW + tq
    w0 = pl.multiple_of(jnp.maximum(t0 - WINDOW, 0), tq)
    wpos = w0 + lax.broadcasted_iota(jnp.int32, (wlen, tq), 0)
    wbias1 = jnp.where((wpos <= tok1) & (wpos > tok1 - WINDOW), 0.0, NEG)
    sw = (jnp.dot(kw_ref[0, pl.ds(w0, wlen), :], qtm, preferred_element_type=F32)
          + jnp.concatenate([wbias1] * hpg, axis=1))
    pw = jnp.exp(sw - jnp.max(sw, axis=0, keepdims=True)).astype(BF16)
    vw_ones = jnp.concatenate([vwt_ref[0, pl.ds(grow, HEAD_DIM), pl.ds(w0, wlen)],
                               jnp.ones((16, wlen), BF16)], axis=0)
    pvw = jnp.dot(vw_ones, pw, preferred_element_type=F32)
    o_win = pvw[:HEAD_DIM] / pvw[HEAD_DIM:HEAD_DIM + 1]

    krow = lax.broadcasted_iota(jnp.int32, (kb, tq), 0)
    init = (jnp.full((1, n), NEG, F32), jnp.zeros((1, n), F32), jnp.zeros((HEAD_DIM, n), F32))
    nblk = kb // SEL_BLOCK
    def sel_body(k, carry):
        k0 = pl.multiple_of(k * kb, kb)
        rows = [jnp.broadcast_to(selb_ref[pl.ds(k * nblk + r, 1), :], (SEL_BLOCK, tq)) for r in range(nblk)]
        bias1 = jnp.where(k0 + krow <= tok1, jnp.concatenate(rows, axis=0), NEG)
        bias = jnp.concatenate([bias1] * hpg, axis=1)
        return _flash_step(ks_ref[0, pl.ds(k0, kb), :], vst_ref[0, pl.ds(grow, HEAD_DIM), pl.ds(k0, kb)],
                           qtm, bias, carry)
    _, l_s, acc_s = lax.fori_loop(0, t0 // kb + 1, sel_body, init)

    gt = jax.nn.sigmoid(gate_ref[0, pl.ds(pl.multiple_of(g * 16, 16), 16), :].astype(F32))
    o_sel = acc_s / l_s
    for h in range(hpg):
        sl = slice(h * tq, (h + 1) * tq)
        o_h = (gt[h:h + 1] * o_cmp[:, sl] + gt[hpg + h:hpg + h + 1] * o_sel[:, sl]
               + gt[2 * hpg + h:2 * hpg + h + 1] * o_win[:, sl])
        o_ref[0, h * HEAD_DIM:(h + 1) * HEAD_DIM, :] = o_h.astype(o_ref.dtype)


def _nsa(qk, kc, vct, tposed, ovt, batch, seq):
    tq = NSA_TQ
    ncp = seq // CMP_STRIDE
    ns = seq // SEL_BLOCK
    n_sel = min(SEL_TOPN, ns)
    qw = NSA_HPG * HEAD_DIM
    return pl.pallas_call(
        functools.partial(_nsa_kernel, n_sel=n_sel),
        out_shape=jax.ShapeDtypeStruct((batch, NSA_HEADS * HEAD_DIM, seq), BF16),
        grid=(batch, NSA_GROUPS, seq // tq),
        in_specs=[
            pl.BlockSpec((1, tq, qw), lambda b, g, i: (b, i, g)),
            pl.BlockSpec((1, ncp, LANES), lambda b, g, i: (b, 0, 0)),
            pl.BlockSpec((1, LANES, ncp), lambda b, g, i: (b, 0, 0)),
            pl.BlockSpec((1, seq, LANES), lambda b, g, i: (b, 0, 4)),
            pl.BlockSpec((1, seq, LANES), lambda b, g, i: (b, 0, 5)),
            pl.BlockSpec((1, LANES, seq), lambda b, g, i: (b, TS_VS, 0)),
            pl.BlockSpec((1, LANES, seq), lambda b, g, i: (b, TS_VW, 0)),
            pl.BlockSpec((1, LANES, tq), lambda b, g, i: (b, TS_GN, i)),
            pl.BlockSpec((ns, ncp), lambda b, g, i: (0, 0)),
        ],
        out_specs=pl.BlockSpec((1, qw, tq), lambda b, g, i: (b, g, i)),
        scratch_shapes=[pltpu.VMEM((ns, tq), F32)],
        compiler_params=_cparams(("parallel", "parallel", "arbitrary")),
        name="nsa",
    )(qk, kc, vct, qk, qk, tposed, tposed, tposed, ovt)


SB_TQ = 256


def _sb_kernel(qt_ref, k_ref, vt_ref, o_ref):
    i = pl.program_id(2)
    tq = SB_TQ
    kb = tq
    n = 2 * tq
    scale = 1.0 / math.sqrt(HEAD_DIM)

    q2 = qt_ref[0].astype(F32) * scale
    rowgrp = lax.broadcasted_iota(jnp.int32, (LANES, tq), 0) // HEAD_DIM
    qtm = jnp.concatenate([jnp.where(rowgrp == 0, q2, 0.0), jnp.where(rowgrp == 1, q2, 0.0)],
                          axis=1).astype(BF16)

    r = lax.broadcasted_iota(jnp.int32, (kb, kb), 0)
    c = lax.broadcasted_iota(jnp.int32, (kb, kb), 1)
    later = jnp.where(c > r, 1.0, 0.0).astype(BF16)
    krow = lax.broadcasted_iota(jnp.int32, (kb, n), 0)
    tcol = lax.broadcasted_iota(jnp.int32, (kb, n), 1) % tq

    def body(st):
        k, carry, acc_a, acc_b, _ = st
        k0 = pl.multiple_of(k * kb, kb)
        z = jnp.dot(k_ref[0, pl.ds(k0, kb), :].astype(BF16), qtm, preferred_element_type=F32)
        sp = jnp.maximum(z, 0.0) + jnp.log1p(jnp.exp(-jnp.abs(z)))
        mask = (k0 + krow) < (i * tq + tcol)
        spm = jnp.where(mask, sp, 0.0)
        hi = spm.astype(BF16)
        lo = (spm - hi.astype(F32)).astype(BF16)
        after = (jnp.dot(later, hi, preferred_element_type=F32)
                 + jnp.dot(later, lo, preferred_element_type=F32))
        a = jnp.where(mask, jnp.exp(z - sp - after - carry), 0.0).astype(BF16)
        acc_a = acc_a + jnp.dot(vt_ref[0, :HEAD_DIM, pl.ds(k0, kb)], a[:, :tq], preferred_element_type=F32)
        acc_b = acc_b + jnp.dot(vt_ref[0, HEAD_DIM:, pl.ds(k0, kb)], a[:, tq:], preferred_element_type=F32)
        carry = carry + after[0:1] + spm[0:1]
        go = jnp.logical_and(k > 0, jnp.min(carry) <= SB_EXIT)
        return k - 1, carry, acc_a, acc_b, go

    zacc = jnp.zeros((HEAD_DIM, tq), F32)
    st = (i, jnp.zeros((1, n), F32), zacc, zacc, i >= 0)
    _, _, acc_a, acc_b, _ = lax.while_loop(lambda s: s[4], body, st)
    o_ref[0, :HEAD_DIM, :] = acc_a.astype(o_ref.dtype)
    o_ref[0, HEAD_DIM:, :] = acc_b.astype(o_ref.dtype)


def _sb(proj3, tposed, batch, seq):
    tq = SB_TQ
    return pl.pallas_call(
        _sb_kernel,
        out_shape=jax.ShapeDtypeStruct((batch, SB_HEADS * HEAD_DIM, seq), BF16),
        grid=(batch, SB_HEADS // 2, seq // tq),
        in_specs=[
            pl.BlockSpec((1, LANES, tq), lambda b, h, i: (b, TS_QSB + h, i)),
            pl.BlockSpec((1, seq, LANES), lambda b, h, i: (b, 0, CB_KSB + h)),
            pl.BlockSpec((1, LANES, seq), lambda b, h, i: (b, TS_VSB + h, 0)),
        ],
        out_specs=pl.BlockSpec((1, LANES, tq), lambda b, h, i: (b, h, i)),
        compiler_params=_cparams(("parallel", "parallel", "arbitrary")),
        name="stickbreaking",
    )(tposed, proj3, tposed)


def _merge_kernel(ont_ref, ost_ref, ga_ref, gb_ref, x_ref, wa_ref, wb_ref, wo_ref, o_ref):
    tn_dims = (((0,), (0,)), ((), ()))
    a = lax.dot_general(ont_ref[0], wa_ref[...], tn_dims, preferred_element_type=F32)
    b = lax.dot_general(ost_ref[0], wb_ref[...], tn_dims, preferred_element_type=F32)
    merged = jax.nn.sigmoid(ga_ref[...]) * a + jax.nn.sigmoid(gb_ref[...]) * b
    o_ref[...] = x_ref[...] + jnp.dot(merged.astype(BF16), wo_ref[...], preferred_element_type=F32)


def _merge(ont, ost, proj, x2d, wa, wb, wo, batch, seq, tm):
    d = x2d.shape[1]
    s_tiles = seq // tm
    hw = ont.shape[1]
    return pl.pallas_call(
        _merge_kernel,
        out_shape=jax.ShapeDtypeStruct(x2d.shape, F32),
        grid=(batch, s_tiles),
        in_specs=[
            pl.BlockSpec((1, hw, tm), lambda b, i: (b, 0, i)),
            pl.BlockSpec((1, hw, tm), lambda b, i: (b, 0, i)),
            pl.BlockSpec((tm, d), lambda b, i: (b * s_tiles + i, 0)),
            pl.BlockSpec((tm, d), lambda b, i: (b * s_tiles + i, 1)),
            pl.BlockSpec((tm, d), lambda b, i: (b * s_tiles + i, 0)),
            pl.BlockSpec((hw, d), lambda b, i: (0, 0)),
            pl.BlockSpec((hw, d), lambda b, i: (0, 0)),
            pl.BlockSpec((d, d), lambda b, i: (0, 0)),
        ],
        out_specs=pl.BlockSpec((tm, d), lambda b, i: (b * s_tiles + i, 0)),
        compiler_params=_cparams(("parallel", "parallel")),
        name="merge",
    )(ont, ost, proj, proj, x2d, wa, wb, wo)


PEER_TR = 256


def _topk_ranked(s, idx, k, vals_ref=None, first_ref=None, want_rank=True):
    rows, n = s.shape
    sub = 8
    s = s.reshape(rows // sub, sub, n)
    idx = idx.reshape(rows // sub, sub, n)

    def all_sublanes(x, op):
        for sh in (4, 2, 1):
            x = op(x, pltpu.roll(x, sh, 0))
        return x

    rank = jnp.full(s.shape, float(k), F32) if want_rank else None
    for r in range(k):
        mx = all_sublanes(jnp.max(s, axis=0), jnp.maximum)
        first = all_sublanes(jnp.min(jnp.where(s == mx[None], idx, 1e9), axis=0), jnp.minimum)
        pick = idx == first[None]
        if want_rank:
            rank = jnp.where(pick, float(r), rank)
        s = jnp.where(pick, -jnp.inf, s)
        if vals_ref is not None:
            vals_ref[r:r + 1, :] = mx[0:1]
        if first_ref is not None:
            first_ref[r:r + 1, :] = first[0:1]
    return rank.reshape(rows, n) if want_rank else None


_CAND_GROUPS = ((0, 0, 16), (16, 1, 8), (24, 2, 8), (32, 3, 4), (36, 4, 4), (40, 5, 2), (42, 6, 2), (44, 7, 2))
_CAND_ROWS = 56


def _route_kernel(q_ref, sk_ref, n1_ref, a1_ref, b2_ref, e2_ref, v1_ref, v2_ref, f1_ref, cand_ref):
    k = PEER_TOPK
    half = PEER_QDIM // 2
    nt_dims = (((1,), (1,)), ((), ()))
    qh = q_ref[...].astype(BF16)
    s1 = lax.dot_general(sk_ref[0].astype(BF16), qh[:, :half], nt_dims, preferred_element_type=F32)
    s2 = lax.dot_general(sk_ref[1].astype(BF16), qh[:, half:], nt_dims, preferred_element_type=F32)
    tr = s1.shape[1]
    kidx = lax.broadcasted_iota(jnp.int32, s1.shape, 0).astype(F32)
    _topk_ranked(s1, kidx, k, v1_ref, f1_ref, want_rank=False)
    rank2 = _topk_ranked(s2, kidx, k, v2_ref)
    v1 = v1_ref[...]
    v2 = v2_ref[...]

    flat = lax.broadcasted_iota(jnp.int32, (_CAND_ROWS, tr), 0)
    row = flat
    for r0, a, nb in _CAND_GROUPS:
        cand_ref[r0:r0 + nb, :] = v1[a:a + 1] + v2[0:nb]
        flat = jnp.where((row >= r0) & (row < r0 + nb), row - r0 + a * k, flat)
    cand_ref[46:48, :] = jnp.full((2, tr), -jnp.inf, F32)
    cand_ref[48:56, :] = v1[8:16] + v2[0:1]
    flat = jnp.where(row >= 48, (row - 40) * k, flat).astype(F32)
    cand = cand_ref[...]
    crank = _topk_ranked(cand, flat, k)
    selc = jnp.where(crank < float(k), 1.0, 0.0)
    mx = v1[0:1] + v2[0:1]
    z = jnp.sum(jnp.where(crank < float(k), jnp.exp(cand - mx), 0.0), axis=0, keepdims=True)
    n_a = [jnp.sum(selc[r0:r0 + nb], axis=0, keepdims=True) for r0, _, nb in _CAND_GROUPS]
    n_a += [selc[48 + a - 8:48 + a - 7] for a in range(8, k)]

    f1 = f1_ref[...]
    n1 = jnp.zeros_like(s1)
    for a in range(k):
        n1 = jnp.where(kidx == f1[a:a + 1], n_a[a], n1)
    n1_ref[0] = n1
    a1_ref[0] = jnp.exp(s1 - v1[0:1]) / z
    b2_ref[0] = rank2.astype(b2_ref.dtype)
    e2_ref[0] = jnp.exp(s2 - v2[0:1]).astype(e2_ref.dtype)


def _route(qp, subkeys):
    t = qp.shape[0]
    tr = PEER_TR
    nk = PEER_NKEYS
    shp = jax.ShapeDtypeStruct((PEER_HEADS, nk, t), F32)
    shp16 = jax.ShapeDtypeStruct((PEER_HEADS, nk, t), BF16)
    spec = pl.BlockSpec((1, nk, tr), lambda i, h: (h, 0, i))
    return pl.pallas_call(
        _route_kernel,
        out_shape=(shp, shp, shp16, shp16),
        grid=(t // tr, PEER_HEADS),
        in_specs=[
            pl.BlockSpec((tr, PEER_QDIM), lambda i, h: (i, h)),
            pl.BlockSpec((2, nk, PEER_QDIM // 2), lambda i, h: (0, 0, 0)),
        ],
        out_specs=(spec, spec, spec, spec),
        scratch_shapes=[pltpu.VMEM((PEER_TOPK, tr), F32), pltpu.VMEM((PEER_TOPK, tr), F32),
                        pltpu.VMEM((PEER_TOPK, tr), F32), pltpu.VMEM((_CAND_ROWS, tr), F32)],
        compiler_params=_cparams(("parallel", "parallel")),
        name="peer_route",
    )(qp, subkeys)


PEER_TM = 512
PEER_NB = 8


def _peer_kernel(h_ref, g_ref, u_ref, vt_ref, n1_ref, a1_ref, b2_ref, e2_ref, o_ref, hnt_ref, acc_ref):
    kk = pl.program_id(1)
    nk = PEER_NKEYS

    @pl.when(kk == 0)
    def _():
        h = h_ref[...]
        ms = jnp.mean(h * h, axis=-1, keepdims=True)
        hnt_ref[...] = (h * lax.rsqrt(ms + EPS) * g_ref[...]).T.astype(BF16)
        acc_ref[...] = jnp.zeros_like(acc_ref)

    act = jax.nn.gelu(jnp.dot(u_ref[...], hnt_ref[...], preferred_element_type=F32).astype(BF16))
    tm = act.shape[1]
    rep = nk // 16

    def bcast_row(ref, hd, i1):
        row16 = jnp.broadcast_to(ref[hd, pl.ds(i1, 1), :], (16, tm)).astype(BF16)
        return jnp.concatenate([row16] * rep, axis=0)

    parts = []
    for j in range(PEER_NB):
        i1 = kk * PEER_NB + j
        w = jnp.zeros((nk, tm), BF16)
        for hd in range(PEER_HEADS):
            n1 = bcast_row(n1_ref, hd, i1)
            a1 = bcast_row(a1_ref, hd, i1)
            w = w + jnp.where(b2_ref[hd] < n1, e2_ref[hd] * a1, jnp.zeros((), BF16))
        parts.append(w * act[j * nk:(j + 1) * nk])
    mt = jnp.concatenate(parts, axis=0)
    acc_ref[...] += jnp.dot(vt_ref[...], mt, preferred_element_type=F32)

    @pl.when(kk == pl.num_programs(1) - 1)
    def _():
        o_ref[...] = h_ref[...] + acc_ref[...].T


def _peer(h2d, gamma, u_bf16, vt_bf16, tables):
    t, d = h2d.shape
    tm = min(PEER_TM, t)
    nk = PEER_NKEYS
    ke = PEER_NB * nk
    tspec = pl.BlockSpec((PEER_HEADS, nk, tm), lambda i, k: (0, 0, i))
    return pl.pallas_call(
        _peer_kernel,
        out_shape=jax.ShapeDtypeStruct((t, d), F32),
        grid=(t // tm, nk // PEER_NB),
        in_specs=[
            pl.BlockSpec((tm, d), lambda i, k: (i, 0)),
            pl.BlockSpec((1, d), lambda i, k: (0, 0)),
            pl.BlockSpec((ke, d), lambda i, k: (k, 0)),
            pl.BlockSpec((d, ke), lambda i, k: (0, k)),
            tspec, tspec, tspec, tspec,
        ],
        out_specs=pl.BlockSpec((tm, d), lambda i, k: (i, 0)),
        scratch_shapes=[pltpu.VMEM((d, tm), BF16), pltpu.VMEM((d, tm), F32)],
        compiler_params=_cparams(("parallel", "arbitrary")),
        name="peer_experts",
    )(h2d, gamma.reshape(1, d), u_bf16, vt_bf16, *tables)


def _reorder_w_in(w_in):
    d = w_in.shape[0]
    nsa_w = NSA_HEADS * HEAD_DIM
    kv_w = NSA_GROUPS * HEAD_DIM
    sb_w = SB_HEADS * HEAD_DIM
    c0 = nsa_w
    c1 = c0 + 6 * kv_w
    c2 = c1 + 3 * NSA_HEADS
    c3 = c2 + 3 * sb_w
    gn = w_in[:, c1:c2].reshape(d, 3, NSA_GROUPS, NSA_HPG).transpose(0, 2, 1, 3)
    gn = jnp.pad(gn.reshape(d, NSA_GROUPS, 3 * NSA_HPG), ((0, 0), (0, 0), (0, 16 - 3 * NSA_HPG)))
    gn = jnp.pad(gn.reshape(d, NSA_GROUPS * 16), ((0, 0), (0, LANES - NSA_GROUPS * 16)))
    return jnp.concatenate([w_in[:, c3:], w_in[:, :c0], w_in[:, c0:c1], w_in[:, c2:c3], gn], axis=1).astype(BF16)


def _overlap_t(seq):
    ncp = seq // CMP_STRIDE
    nc = (seq - CMP_LEN) // CMP_STRIDE + 1
    ns = seq // SEL_BLOCK
    c_start = jnp.arange(ncp) * CMP_STRIDE
    j_start = jnp.arange(ns) * SEL_BLOCK
    ov = jnp.clip(jnp.minimum(c_start[None, :] + CMP_LEN, j_start[:, None] + SEL_BLOCK)
                  - jnp.maximum(c_start[None, :], j_start[:, None]), 0).astype(F32) / CMP_LEN
    return jnp.where(jnp.arange(ncp)[None, :] < nc, ov, 0.0).astype(BF16)


def _token_mixers(x, norm1_g, w_in, q_norm_g, k_norm_g, cmp_pos_k, cmp_pos_v, cmp_k_w1, cmp_k_w2,
                  cmp_v_w1, cmp_v_w2, w_branch_nsa, w_branch_sb, w_out):
    batch, seq, d = x.shape
    t = batch * seq
    x2d = x.reshape(t, d)
    scale = 1.0 / math.sqrt(HEAD_DIM)

    proj = _norm_matmul(x2d, norm1_g, _reorder_w_in(w_in), tm=1024, tn=13 * LANES, name="in_proj")

    cos, sin = _rope_tables(jnp.arange(seq))
    qg = jnp.tile(q_norm_g.astype(F32), 2) * scale
    gammas = jnp.stack([qg] * 4 + [jnp.tile(k_norm_g[1].astype(F32), 2), jnp.tile(k_norm_g[2].astype(F32), 2)])
    qk = _prep_norm_rope(proj, cos, sin, gammas.reshape(6, 1, LANES), seq,
                         ts=min(1024, seq)).reshape(batch, seq, 6 * LANES)
    tposed = _prep_transpose(proj, batch, seq, ts=min(2048, seq))

    ncp = seq // CMP_STRIDE
    cosc, sinc = _rope_tables(jnp.arange(ncp) * CMP_STRIDE + CMP_LEN - 1)
    kc = _compress(proj[:, CB_KC * LANES:(CB_KC + 1) * LANES], cmp_pos_k, cmp_k_w1, cmp_k_w2, batch, seq,
                   key_args=(cosc, sinc, jnp.tile(k_norm_g[0].astype(F32), 2).reshape(1, LANES)))
    vct = _compress(proj[:, CB_VC * LANES:(CB_VC + 1) * LANES], cmp_pos_v, cmp_v_w1, cmp_v_w2, batch, seq)

    ont = _nsa(qk, kc, vct, tposed, _overlap_t(seq), batch, seq)
    ost = _sb(proj.reshape(batch, seq, N_CB * LANES), tposed, batch, seq)
    return _merge(ont, ost, proj, x2d, w_branch_nsa.astype(BF16), w_branch_sb.astype(BF16),
                  w_out.astype(BF16), batch, seq, tm=512)


def kernel(x, norm1_g, w_in, q_norm_g, k_norm_g, cmp_pos_k, cmp_pos_v, cmp_k_w1, cmp_k_w2, cmp_v_w1, cmp_v_w2,
           w_branch_nsa, w_branch_sb, w_out, norm2_g, peer_w_query, peer_subkeys, peer_u, peer_v):
    batch, seq, d = x.shape
    h = x.astype(F32)
    for l in range(norm1_g.shape[0]):
        h2d = _token_mixers(h, norm1_g[l], w_in[l], q_norm_g[l], k_norm_g[l], cmp_pos_k[l], cmp_pos_v[l],
                            cmp_k_w1[l], cmp_k_w2[l], cmp_v_w1[l], cmp_v_w2[l],
                            w_branch_nsa[l], w_branch_sb[l], w_out[l])
        qp = _norm_matmul(h2d, norm2_g[l], peer_w_query[l].astype(BF16), tm=1024, tn=1024, name="peer_query")
        tables = _route(qp, peer_subkeys[l])
        h2d = _peer(h2d, norm2_g[l], peer_u[l].astype(BF16), peer_v[l].T.astype(BF16), tables)
        h = h2d.reshape(batch, seq, d)
    return h.astype(x.dtype)
```

```python
import functools
import math

import jax
import jax.numpy as jnp
from jax import lax
from jax.experimental import pallas as pl
from jax.experimental.pallas import tpu as pltpu

F32 = jnp.float32
BF16 = jnp.bfloat16

HEAD_DIM = 64
NSA_HEADS = 8
NSA_GROUPS = 2
NSA_HPG = NSA_HEADS // NSA_GROUPS
CMP_LEN = 32
CMP_STRIDE = 16
CMP_HIDDEN = 256
SEL_BLOCK = 64
SEL_TOPN = 16
WINDOW = 512
SB_HEADS = 8
ROPE_THETA = 10000.0
PEER_HEADS = 8
PEER_NKEYS = 128
PEER_QDIM = 256
PEER_TOPK = 16
EPS = 1e-6
NEG = -1e30

LANES = 128
VMEM_LIMIT = 56 * 1024 * 1024

CB_GATE = 0
CB_Q = 16
CB_KC, CB_VC, CB_KS, CB_VS, CB_KW, CB_VW = 20, 21, 22, 23, 24, 25
CB_QSB, CB_KSB, CB_VSB = 26, 30, 34
CB_GN = 38
N_CB = 39

SB_EXIT = 104.0


def _cparams(sem, vmem=None):
    return pltpu.CompilerParams(dimension_semantics=sem, vmem_limit_bytes=vmem or VMEM_LIMIT)


def _norm_matmul_kernel(x_ref, g_ref, w_ref, o_ref):
    x = x_ref[...]
    ms = jnp.mean(x * x, axis=-1, keepdims=True)
    xn = (x * lax.rsqrt(ms + EPS) * g_ref[...]).astype(BF16)
    o_ref[...] = jnp.dot(xn, w_ref[...], preferred_element_type=F32)


def _norm_matmul(x2d, gamma, w_bf16, tm, tn, name):
    t, d = x2d.shape
    n = w_bf16.shape[1]
    return pl.pallas_call(
        _norm_matmul_kernel,
        out_shape=jax.ShapeDtypeStruct((t, n), F32),
        grid=(n // tn, t // tm),
        in_specs=[
            pl.BlockSpec((tm, d), lambda j, i: (i, 0)),
            pl.BlockSpec((1, d), lambda j, i: (0, 0)),
            pl.BlockSpec((d, tn), lambda j, i: (0, j)),
        ],
        out_specs=pl.BlockSpec((tm, tn), lambda j, i: (i, j)),
        compiler_params=_cparams(("parallel", "parallel")),
        name=name,
    )(x2d, gamma.reshape(1, d), w_bf16)


def _group_mean_sq(x):
    ss = x * x
    r = lax.broadcasted_iota(jnp.int32, (LANES, LANES), 0) // HEAD_DIM
    c = lax.broadcasted_iota(jnp.int32, (LANES, LANES), 1) // HEAD_DIM
    ones_bd = jnp.where(r == c, 1.0, 0.0).astype(BF16)
    hi = ss.astype(BF16)
    lo = (ss - hi.astype(F32)).astype(BF16)
    tot = (jnp.dot(hi, ones_bd, preferred_element_type=F32)
           + jnp.dot(lo, ones_bd, preferred_element_type=F32))
    return tot * (1.0 / HEAD_DIM)


def _norm_rope(x, cos, sin_signed, gamma):
    xn = x * lax.rsqrt(_group_mean_sq(x) + EPS) * gamma
    lane = lax.broadcasted_iota(jnp.int32, xn.shape, 1)
    first_half = (lane % HEAD_DIM) < (HEAD_DIM // 2)
    swapped = jnp.where(first_half, pltpu.roll(xn, LANES - HEAD_DIM // 2, 1), pltpu.roll(xn, HEAD_DIM // 2, 1))
    return xn * cos + swapped * sin_signed


def _rope_tables(pos):
    half = HEAD_DIM // 2
    d = jnp.arange(LANES) % HEAD_DIM
    freqs = ROPE_THETA ** (-(d % half).astype(F32) / half)
    ang = pos.astype(F32)[:, None] * freqs[None, :]
    sign = jnp.where(d < half, -1.0, 1.0).astype(F32)
    return jnp.cos(ang), jnp.sin(ang) * sign[None, :]


def _norm_rope_kernel(x_ref, cos_ref, sin_ref, g_ref, o_ref):
    o_ref[...] = _norm_rope(x_ref[...], cos_ref[...], sin_ref[...], g_ref[0]).astype(o_ref.dtype)


def _prep_norm_rope(proj, cos, sin, gammas, seq, ts):
    t = proj.shape[0]
    n_slab = gammas.shape[0]
    s_tiles = seq // ts
    return pl.pallas_call(
        _norm_rope_kernel,
        out_shape=jax.ShapeDtypeStruct((t, n_slab * LANES), BF16),
        grid=(t // ts, n_slab),
        in_specs=[
            pl.BlockSpec((ts, LANES), lambda i, j: (i, CB_Q + j + 2 * (j // 4) + (j // 5))),
            pl.BlockSpec((ts, LANES), lambda i, j: (i % s_tiles, 0)),
            pl.BlockSpec((ts, LANES), lambda i, j: (i % s_tiles, 0)),
            pl.BlockSpec((1, 1, LANES), lambda i, j: (j, 0, 0)),
        ],
        out_specs=pl.BlockSpec((ts, LANES), lambda i, j: (i, j)),
        compiler_params=_cparams(("parallel", "parallel")),
        name="prep_norm_rope",
    )(proj, cos, sin, gammas)


T_SLABS = (CB_VS, CB_VW, CB_QSB, CB_QSB + 1, CB_QSB + 2, CB_QSB + 3,
           CB_VSB, CB_VSB + 1, CB_VSB + 2, CB_VSB + 3, CB_GN)
TS_VS, TS_VW, TS_QSB, TS_VSB, TS_GN = 0, 1, 2, 6, 10


def _transpose_kernel(cm_ref, x_ref, o_ref):
    del cm_ref
    o_ref[0] = x_ref[...].T.astype(o_ref.dtype)


def _prep_transpose(proj, batch, seq, ts):
    s_tiles = seq // ts
    colmap = jnp.asarray(T_SLABS, jnp.int32)
    return pl.pallas_call(
        _transpose_kernel,
        out_shape=jax.ShapeDtypeStruct((batch, len(T_SLABS) * LANES, seq), BF16),
        grid_spec=pltpu.PrefetchScalarGridSpec(
            num_scalar_prefetch=1,
            grid=(batch, s_tiles, len(T_SLABS)),
            in_specs=[pl.BlockSpec((ts, LANES), lambda b, i, j, cm: (b * s_tiles + i, cm[j]))],
            out_specs=pl.BlockSpec((1, LANES, ts), lambda b, i, j, cm: (b, j, i)),
        ),
        compiler_params=_cparams(("parallel", "parallel", "parallel")),
        name="prep_transpose",
    )(colmap, proj)


def _compress_kernel(a_ref, pa_ref, pb_ref, w1a_ref, w1b_ref, w2_ref, *rest, is_key):
    a = a_ref[0]
    ncp = a.shape[0]
    p = jnp.dot((a + pa_ref[...]).astype(BF16), w1a_ref[...], preferred_element_type=F32)
    q = jnp.dot((a + pb_ref[...]).astype(BF16), w1b_ref[...], preferred_element_type=F32)
    hid = jax.nn.gelu(p + pltpu.roll(q, ncp - 1, 0))
    if is_key:
        cos_ref, sin_ref, g_ref, o_ref = rest
        out = jnp.dot(hid.astype(BF16), w2_ref[...], preferred_element_type=F32)
        o_ref[0] = _norm_rope(out, cos_ref[...], sin_ref[...], g_ref[...]).astype(o_ref.dtype)
    else:
        (o_ref,) = rest
        nt_dims = (((1,), (1,)), ((), ()))
        out_t = lax.dot_general(w2_ref[...], hid.astype(BF16), nt_dims, preferred_element_type=F32)
        o_ref[0] = out_t.astype(o_ref.dtype)


def _expand_cmp_weights(w1, w2):
    eye = jnp.eye(NSA_GROUPS, dtype=F32)
    w1r = w1.reshape(2, CMP_STRIDE, HEAD_DIM, CMP_HIDDEN)
    ex = w1r[:, :, None, :, None, :] * eye[None, None, :, None, :, None]
    ex = ex.reshape(2, CMP_STRIDE * LANES, NSA_GROUPS * CMP_HIDDEN).astype(BF16)
    w2x = (w2[None, :, None, :] * eye[:, None, :, None]).reshape(NSA_GROUPS * CMP_HIDDEN, LANES).astype(BF16)
    return ex[0], ex[1], w2x


def _compress(tok_slab, pos_emb, w1, w2, batch, seq, key_args=None):
    ncp = seq // CMP_STRIDE
    a = tok_slab.reshape(batch, ncp, CMP_STRIDE * LANES)
    pos2 = jnp.tile(pos_emb.astype(F32)[:, None, :], (1, NSA_GROUPS, 1)).reshape(2, 1, CMP_STRIDE * LANES)
    w1a, w1b, w2x = _expand_cmp_weights(w1, w2)
    kdim = CMP_STRIDE * LANES
    hdim = NSA_GROUPS * CMP_HIDDEN
    if key_args is None:
        w2x = w2x.T
    in_specs = [
        pl.BlockSpec((1, ncp, kdim), lambda b: (b, 0, 0)),
        pl.BlockSpec((1, kdim), lambda b: (0, 0)),
        pl.BlockSpec((1, kdim), lambda b: (0, 0)),
        pl.BlockSpec((kdim, hdim), lambda b: (0, 0)),
        pl.BlockSpec((kdim, hdim), lambda b: (0, 0)),
        pl.BlockSpec(w2x.shape, lambda b: (0, 0)),
    ]
    args = [a, pos2[0], pos2[1], w1a, w1b, w2x]
    if key_args is not None:
        cos, sin, gamma = key_args
        in_specs += [pl.BlockSpec((ncp, LANES), lambda b: (0, 0)),
                     pl.BlockSpec((ncp, LANES), lambda b: (0, 0)),
                     pl.BlockSpec((1, LANES), lambda b: (0, 0))]
        args += [cos, sin, gamma]
        out_shape = jax.ShapeDtypeStruct((batch, ncp, LANES), BF16)
        out_spec = pl.BlockSpec((1, ncp, LANES), lambda b: (b, 0, 0))
    else:
        out_shape = jax.ShapeDtypeStruct((batch, LANES, ncp), BF16)
        out_spec = pl.BlockSpec((1, LANES, ncp), lambda b: (b, 0, 0))
    return pl.pallas_call(
        functools.partial(_compress_kernel, is_key=key_args is not None),
        out_shape=out_shape,
        grid=(batch,),
        in_specs=in_specs,
        out_specs=out_spec,
        compiler_params=_cparams(("parallel",)),
        name="compress_k" if key_args is not None else "compress_v",
    )(*args)


NSA_TQ = 256
NSA_KB = 1024


def _flash_step(k_tile, vt_tile, qtm, bias, carry):
    m, l, acc = carry
    s = jnp.dot(k_tile, qtm, preferred_element_type=F32) + bias
    m_new = jnp.maximum(m, jnp.max(s, axis=0, keepdims=True))
    alpha = jnp.exp(m - m_new)
    p = jnp.exp(s - m_new).astype(BF16)
    vt_ones = jnp.concatenate([vt_tile, jnp.ones((16, vt_tile.shape[1]), BF16)], axis=0)
    pv = jnp.dot(vt_ones, p, preferred_element_type=F32)
    l = alpha * l + pv[HEAD_DIM:HEAD_DIM + 1]
    acc = alpha * acc + pv[:HEAD_DIM]
    return m_new, l, acc


def _nsa_kernel(q_ref, kc_ref, vct_ref, ks_ref, kw_ref, vst_ref, vwt_ref, gate_ref, ovt_ref,
                o_ref, selb_ref, *, n_sel):
    g = pl.program_id(1)
    i = pl.program_id(2)
    tq, kb, hpg = NSA_TQ, NSA_KB, NSA_HPG
    n = hpg * tq
    t0 = i * tq
    grow = pl.multiple_of(g * HEAD_DIM, HEAD_DIM)

    qt = q_ref[0].astype(F32).T
    rowgrp = lax.broadcasted_iota(jnp.int32, (LANES, tq), 0) // HEAD_DIM
    parts = []
    for h in range(hpg):
        blk = qt[h * HEAD_DIM:(h + 1) * HEAD_DIM]
        parts.append(jnp.where(rowgrp == g, jnp.concatenate([blk, blk], axis=0), 0.0))
    qtm = jnp.concatenate(parts, axis=1).astype(BF16)

    tok1 = t0 + lax.broadcasted_iota(jnp.int32, (1, tq), 1)
    tokn = jnp.concatenate([tok1] * hpg, axis=1)

    ncp = kc_ref.shape[1]
    sc = jnp.dot(kc_ref[0], qtm, preferred_element_type=F32)
    cend = lax.broadcasted_iota(jnp.int32, (ncp, n), 0) * CMP_STRIDE + (CMP_LEN - 1)
    sc = jnp.where(cend <= tokn, sc, NEG)
    mc = jnp.max(sc, axis=0, keepdims=True)
    ec = jnp.exp(sc - mc)
    has_c = jnp.where(tokn >= CMP_LEN - 1, 1.0, 0.0)
    pc = ec * (has_c / jnp.sum(ec, axis=0, keepdims=True))
    o_cmp = jnp.dot(vct_ref[0, pl.ds(grow, HEAD_DIM), :], pc.astype(BF16), preferred_element_type=F32)

    psum = pc[:, 0:tq]
    for h in range(1, hpg):
        psum = psum + pc[:, h * tq:(h + 1) * tq]
    p_hi = psum.astype(BF16)
    p_lo = (psum - p_hi.astype(F32)).astype(BF16)
    imp = (jnp.dot(ovt_ref[...], p_hi, preferred_element_type=F32)
           + jnp.dot(ovt_ref[...], p_lo, preferred_element_type=F32))
    ns = imp.shape[0]
    jidx = lax.broadcasted_iota(jnp.int32, (ns, tq), 0)
    jf = jidx.astype(F32)
    tokb = jnp.broadcast_to(tok1, (ns, tq))
    cur = tokb // SEL_BLOCK
    forced = (jidx == 0) | (jidx == cur) | (jidx == cur - 1)
    valid = jidx * SEL_BLOCK <= tokb
    score = jnp.where(forced, 1e4, jnp.where(valid, imp, -1e4))
    selb = jnp.full((ns, tq), NEG, F32)
    for _ in range(n_sel):
        mx = jnp.max(score, axis=0, keepdims=True)
        first = jnp.min(jnp.where(score == mx, jf, float(ns)), axis=0, keepdims=True)
        pick = jf == first
        selb = jnp.where(pick, 0.0, selb)
        score = jnp.where(pick, -jnp.inf, score)
    selb_ref[...] = selb

    wlen = WINDOW + tq
    w0 = pl.multiple_of(jnp.maximum(t0 - WINDOW, 0), tq)
    wpos = w0 + lax.broadcasted_iota(jnp.int32, (wlen, tq), 0)
    wbias1 = jnp.where((wpos <= tok1) & (wpos > tok1 - WINDOW), 0.0, NEG)
    sw = (jnp.dot(kw_ref[0, pl.ds(w0, wlen), :], qtm, preferred_element_type=F32)
          + jnp.concatenate([wbias1] * hpg, axis=1))
    pw = jnp.exp(sw - jnp.max(sw, axis=0, keepdims=True)).astype(BF16)
    vw_ones = jnp.concatenate([vwt_ref[0, pl.ds(grow, HEAD_DIM), pl.ds(w0, wlen)],
                               jnp.ones((16, wlen), BF16)], axis=0)
    pvw = jnp.dot(vw_ones, pw, preferred_element_type=F32)
    o_win = pvw[:HEAD_DIM] / pvw[HEAD_DIM:HEAD_DIM + 1]

    krow = lax.broadcasted_iota(jnp.int32, (kb, tq), 0)
    init = (jnp.full((1, n), NEG, F32), jnp.zeros((1, n), F32), jnp.zeros((HEAD_DIM, n), F32))
    nblk = kb // SEL_BLOCK
    def sel_body(k, carry):
        k0 = pl.multiple_of(k * kb, kb)
        rows = [jnp.broadcast_to(selb_ref[pl.ds(k * nblk + r, 1), :], (SEL_BLOCK, tq)) for r in range(nblk)]
        bias1 = jnp.where(k0 + krow <= tok1, jnp.concatenate(rows, axis=0), NEG)
        bias = jnp.concatenate([bias1] * hpg, axis=1)
        return _flash_step(ks_ref[0, pl.ds(k0, kb), :], vst_ref[0, pl.ds(grow, HEAD_DIM), pl.ds(k0, kb)],
                           qtm, bias, carry)
    _, l_s, acc_s = lax.fori_loop(0, t0 // kb + 1, sel_body, init)

    gt = jax.nn.sigmoid(gate_ref[0, pl.ds(pl.multiple_of(g * 16, 16), 16), :].astype(F32))
    o_sel = acc_s / l_s
    for h in range(hpg):
        sl = slice(h * tq, (h + 1) * tq)
        o_h = (gt[h:h + 1] * o_cmp[:, sl] + gt[hpg + h:hpg + h + 1] * o_sel[:, sl]
               + gt[2 * hpg + h:2 * hpg + h + 1] * o_win[:, sl])
        o_ref[0, h * HEAD_DIM:(h + 1) * HEAD_DIM, :] = o_h.astype(o_ref.dtype)


def _nsa(qk, kc, vct, tposed, ovt, batch, seq):
    tq = NSA_TQ
    ncp = seq // CMP_STRIDE
    ns = seq // SEL_BLOCK
    n_sel = min(SEL_TOPN, ns)
    qw = NSA_HPG * HEAD_DIM
    return pl.pallas_call(
        functools.partial(_nsa_kernel, n_sel=n_sel),
        out_shape=jax.ShapeDtypeStruct((batch, NSA_HEADS * HEAD_DIM, seq), BF16),
        grid=(batch, NSA_GROUPS, seq // tq),
        in_specs=[
            pl.BlockSpec((1, tq, qw), lambda b, g, i: (b, i, g)),
            pl.BlockSpec((1, ncp, LANES), lambda b, g, i: (b, 0, 0)),
            pl.BlockSpec((1, LANES, ncp), lambda b, g, i: (b, 0, 0)),
            pl.BlockSpec((1, seq, LANES), lambda b, g, i: (b, 0, 4)),
            pl.BlockSpec((1, seq, LANES), lambda b, g, i: (b, 0, 5)),
            pl.BlockSpec((1, LANES, seq), lambda b, g, i: (b, TS_VS, 0)),
            pl.BlockSpec((1, LANES, seq), lambda b, g, i: (b, TS_VW, 0)),
            pl.BlockSpec((1, LANES, tq), lambda b, g, i: (b, TS_GN, i)),
            pl.BlockSpec((ns, ncp), lambda b, g, i: (0, 0)),
        ],
        out_specs=pl.BlockSpec((1, qw, tq), lambda b, g, i: (b, g, i)),
        scratch_shapes=[pltpu.VMEM((ns, tq), F32)],
        compiler_params=_cparams(("parallel", "parallel", "arbitrary")),
        name="nsa",
    )(qk, kc, vct, qk, qk, tposed, tposed, tposed, ovt)


SB_TQ = 256


def _sb_kernel(qt_ref, k_ref, vt_ref, o_ref):
    i = pl.program_id(2)
    tq = SB_TQ
    kb = tq
    n = 2 * tq
    scale = 1.0 / math.sqrt(HEAD_DIM)

    q2 = qt_ref[0].astype(F32) * scale
    rowgrp = lax.broadcasted_iota(jnp.int32, (LANES, tq), 0) // HEAD_DIM
    qtm = jnp.concatenate([jnp.where(rowgrp == 0, q2, 0.0), jnp.where(rowgrp == 1, q2, 0.0)],
                          axis=1).astype(BF16)

    r = lax.broadcasted_iota(jnp.int32, (kb, kb), 0)
    c = lax.broadcasted_iota(jnp.int32, (kb, kb), 1)
    later = jnp.where(c > r, 1.0, 0.0).astype(BF16)
    krow = lax.broadcasted_iota(jnp.int32, (kb, n), 0)
    tcol = lax.broadcasted_iota(jnp.int32, (kb, n), 1) % tq

    def body(st):
        k, carry, acc_a, acc_b, _ = st
        k0 = pl.multiple_of(k * kb, kb)
        z = jnp.dot(k_ref[0, pl.ds(k0, kb), :].astype(BF16), qtm, preferred_element_type=F32)
        sp = jnp.maximum(z, 0.0) + jnp.log1p(jnp.exp(-jnp.abs(z)))
        mask = (k0 + krow) < (i * tq + tcol)
        spm = jnp.where(mask, sp, 0.0)
        hi = spm.astype(BF16)
        lo = (spm - hi.astype(F32)).astype(BF16)
        after = (jnp.dot(later, hi, preferred_element_type=F32)
                 + jnp.dot(later, lo, preferred_element_type=F32))
        a = jnp.where(mask, jnp.exp(z - sp - after - carry), 0.0).astype(BF16)
        acc_a = acc_a + jnp.dot(vt_ref[0, :HEAD_DIM, pl.ds(k0, kb)], a[:, :tq], preferred_element_type=F32)
        acc_b = acc_b + jnp.dot(vt_ref[0, HEAD_DIM:, pl.ds(k0, kb)], a[:, tq:], preferred_element_type=F32)
        carry = carry + after[0:1] + spm[0:1]
        go = jnp.logical_and(k > 0, jnp.min(carry) <= SB_EXIT)
        return k - 1, carry, acc_a, acc_b, go

    zacc = jnp.zeros((HEAD_DIM, tq), F32)
    st = (i, jnp.zeros((1, n), F32), zacc, zacc, i >= 0)
    _, _, acc_a, acc_b, _ = lax.while_loop(lambda s: s[4], body, st)
    o_ref[0, :HEAD_DIM, :] = acc_a.astype(o_ref.dtype)
    o_ref[0, HEAD_DIM:, :] = acc_b.astype(o_ref.dtype)


def _sb(proj3, tposed, batch, seq):
    tq = SB_TQ
    return pl.pallas_call(
        _sb_kernel,
        out_shape=jax.ShapeDtypeStruct((batch, SB_HEADS * HEAD_DIM, seq), BF16),
        grid=(batch, SB_HEADS // 2, seq // tq),
        in_specs=[
            pl.BlockSpec((1, LANES, tq), lambda b, h, i: (b, TS_QSB + h, i)),
            pl.BlockSpec((1, seq, LANES), lambda b, h, i: (b, 0, CB_KSB + h)),
            pl.BlockSpec((1, LANES, seq), lambda b, h, i: (b, TS_VSB + h, 0)),
        ],
        out_specs=pl.BlockSpec((1, LANES, tq), lambda b, h, i: (b, h, i)),
        compiler_params=_cparams(("parallel", "parallel", "arbitrary")),
        name="stickbreaking",
    )(tposed, proj3, tposed)


def _merge_kernel(ont_ref, ost_ref, ga_ref, gb_ref, x_ref, wa_ref, wb_ref, wo_ref, o_ref):
    tn_dims = (((0,), (0,)), ((), ()))
    a = lax.dot_general(ont_ref[0], wa_ref[...], tn_dims, preferred_element_type=F32)
    b = lax.dot_general(ost_ref[0], wb_ref[...], tn_dims, preferred_element_type=F32)
    merged = jax.nn.sigmoid(ga_ref[...]) * a + jax.nn.sigmoid(gb_ref[...]) * b
    o_ref[...] = x_ref[...] + jnp.dot(merged.astype(BF16), wo_ref[...], preferred_element_type=F32)


def _merge(ont, ost, proj, x2d, wa, wb, wo, batch, seq, tm):
    d = x2d.shape[1]
    s_tiles = seq // tm
    hw = ont.shape[1]
    return pl.pallas_call(
        _merge_kernel,
        out_shape=jax.ShapeDtypeStruct(x2d.shape, F32),
        grid=(batch, s_tiles),
        in_specs=[
            pl.BlockSpec((1, hw, tm), lambda b, i: (b, 0, i)),
            pl.BlockSpec((1, hw, tm), lambda b, i: (b, 0, i)),
            pl.BlockSpec((tm, d), lambda b, i: (b * s_tiles + i, 0)),
            pl.BlockSpec((tm, d), lambda b, i: (b * s_tiles + i, 1)),
            pl.BlockSpec((tm, d), lambda b, i: (b * s_tiles + i, 0)),
            pl.BlockSpec((hw, d), lambda b, i: (0, 0)),
            pl.BlockSpec((hw, d), lambda b, i: (0, 0)),
            pl.BlockSpec((d, d), lambda b, i: (0, 0)),
        ],
        out_specs=pl.BlockSpec((tm, d), lambda b, i: (b * s_tiles + i, 0)),
        compiler_params=_cparams(("parallel", "parallel")),
        name="merge",
    )(ont, ost, proj, proj, x2d, wa, wb, wo)


PEER_TR = 256


def _topk_ranked(s, idx, k, vals_ref=None, first_ref=None, want_rank=True):
    rows, n = s.shape
    sub = 8
    s = s.reshape(rows // sub, sub, n)
    idx = idx.reshape(rows // sub, sub, n)

    def all_sublanes(x, op):
        for sh in (4, 2, 1):
            x = op(x, pltpu.roll(x, sh, 0))
        return x

    rank = jnp.full(s.shape, float(k), F32) if want_rank else None
    for r in range(k):
        mx = all_sublanes(jnp.max(s, axis=0), jnp.maximum)
        first = all_sublanes(jnp.min(jnp.where(s == mx[None], idx, 1e9), axis=0), jnp.minimum)
        pick = idx == first[None]
        if want_rank:
            rank = jnp.where(pick, float(r), rank)
        s = jnp.where(pick, -jnp.inf, s)
        if vals_ref is not None:
            vals_ref[r:r + 1, :] = mx[0:1]
        if first_ref is not None:
            first_ref[r:r + 1, :] = first[0:1]
    return rank.reshape(rows, n) if want_rank else None


_CAND_GROUPS = ((0, 0, 16), (16, 1, 8), (24, 2, 8), (32, 3, 4), (36, 4, 4), (40, 5, 2), (42, 6, 2), (44, 7, 2))
_CAND_ROWS = 56


def _route_kernel(q_ref, sk_ref, n1_ref, a1_ref, b2_ref, e2_ref, v1_ref, v2_ref, f1_ref, cand_ref):
    k = PEER_TOPK
    half = PEER_QDIM // 2
    nt_dims = (((1,), (1,)), ((), ()))
    qh = q_ref[...].astype(BF16)
    s1 = lax.dot_general(sk_ref[0].astype(BF16), qh[:, :half], nt_dims, preferred_element_type=F32)
    s2 = lax.dot_general(sk_ref[1].astype(BF16), qh[:, half:], nt_dims, preferred_element_type=F32)
    tr = s1.shape[1]
    kidx = lax.broadcasted_iota(jnp.int32, s1.shape, 0).astype(F32)
    _topk_ranked(s1, kidx, k, v1_ref, f1_ref, want_rank=False)
    rank2 = _topk_ranked(s2, kidx, k, v2_ref)
    v1 = v1_ref[...]
    v2 = v2_ref[...]

    flat = lax.broadcasted_iota(jnp.int32, (_CAND_ROWS, tr), 0)
    row = flat
    for r0, a, nb in _CAND_GROUPS:
        cand_ref[r0:r0 + nb, :] = v1[a:a + 1] + v2[0:nb]
        flat = jnp.where((row >= r0) & (row < r0 + nb), row - r0 + a * k, flat)
    cand_ref[46:48, :] = jnp.full((2, tr), -jnp.inf, F32)
    cand_ref[48:56, :] = v1[8:16] + v2[0:1]
    flat = jnp.where(row >= 48, (row - 40) * k, flat).astype(F32)
    cand = cand_ref[...]
    crank = _topk_ranked(cand, flat, k)
    selc = jnp.where(crank < float(k), 1.0, 0.0)
    mx = v1[0:1] + v2[0:1]
    z = jnp.sum(jnp.where(crank < float(k), jnp.exp(cand - mx), 0.0), axis=0, keepdims=True)
    n_a = [jnp.sum(selc[r0:r0 + nb], axis=0, keepdims=True) for r0, _, nb in _CAND_GROUPS]
    n_a += [selc[48 + a - 8:48 + a - 7] for a in range(8, k)]

    f1 = f1_ref[...]
    n1 = jnp.zeros_like(s1)
    for a in range(k):
        n1 = jnp.where(kidx == f1[a:a + 1], n_a[a], n1)
    n1_ref[0] = n1
    a1_ref[0] = jnp.exp(s1 - v1[0:1]) / z
    b2_ref[0] = rank2.astype(b2_ref.dtype)
    e2_ref[0] = jnp.exp(s2 - v2[0:1]).astype(e2_ref.dtype)


def _route(qp, subkeys):
    t = qp.shape[0]
    tr = PEER_TR
    nk = PEER_NKEYS
    shp = jax.ShapeDtypeStruct((PEER_HEADS, nk, t), F32)
    shp16 = jax.ShapeDtypeStruct((PEER_HEADS, nk, t), BF16)
    spec = pl.BlockSpec((1, nk, tr), lambda i, h: (h, 0, i))
    return pl.pallas_call(
        _route_kernel,
        out_shape=(shp, shp, shp16, shp16),
        grid=(t // tr, PEER_HEADS),
        in_specs=[
            pl.BlockSpec((tr, PEER_QDIM), lambda i, h: (i, h)),
            pl.BlockSpec((2, nk, PEER_QDIM // 2), lambda i, h: (0, 0, 0)),
        ],
        out_specs=(spec, spec, spec, spec),
        scratch_shapes=[pltpu.VMEM((PEER_TOPK, tr), F32), pltpu.VMEM((PEER_TOPK, tr), F32),
                        pltpu.VMEM((PEER_TOPK, tr), F32), pltpu.VMEM((_CAND_ROWS, tr), F32)],
        compiler_params=_cparams(("parallel", "parallel")),
        name="peer_route",
    )(qp, subkeys)


PEER_TM = 512
PEER_NB = 8


def _peer_kernel(h_ref, g_ref, u_ref, vt_ref, n1_ref, a1_ref, b2_ref, e2_ref, o_ref, hnt_ref, acc_ref):
    kk = pl.program_id(1)
    nk = PEER_NKEYS

    @pl.when(kk == 0)
    def _():
        h = h_ref[...]
        ms = jnp.mean(h * h, axis=-1, keepdims=True)
        hnt_ref[...] = (h * lax.rsqrt(ms + EPS) * g_ref[...]).T.astype(BF16)
        acc_ref[...] = jnp.zeros_like(acc_ref)

    act = jax.nn.gelu(jnp.dot(u_ref[...], hnt_ref[...], preferred_element_type=F32)).astype(BF16)
    tm = act.shape[1]
    rep = nk // 16

    def bcast_row(ref, hd, i1):
        row16 = jnp.broadcast_to(ref[hd, pl.ds(i1, 1), :], (16, tm)).astype(BF16)
        return jnp.concatenate([row16] * rep, axis=0)

    parts = []
    for j in range(PEER_NB):
        i1 = kk * PEER_NB + j
        w = jnp.zeros((nk, tm), BF16)
        for hd in range(PEER_HEADS):
            n1 = bcast_row(n1_ref, hd, i1)
            a1 = bcast_row(a1_ref, hd, i1)
            w = w + jnp.where(b2_ref[hd] < n1, e2_ref[hd] * a1, jnp.zeros((), BF16))
        parts.append(w * act[j * nk:(j + 1) * nk])
    mt = jnp.concatenate(parts, axis=0)
    acc_ref[...] += jnp.dot(vt_ref[...], mt, preferred_element_type=F32)

    @pl.when(kk == pl.num_programs(1) - 1)
    def _():
        o_ref[...] = h_ref[...] + acc_ref[...].T


def _peer(h2d, gamma, u_bf16, vt_bf16, tables):
    t, d = h2d.shape
    tm = min(PEER_TM, t)
    nk = PEER_NKEYS
    ke = PEER_NB * nk
    tspec = pl.BlockSpec((PEER_HEADS, nk, tm), lambda i, k: (0, 0, i))
    return pl.pallas_call(
        _peer_kernel,
        out_shape=jax.ShapeDtypeStruct((t, d), F32),
        grid=(t // tm, nk // PEER_NB),
        in_specs=[
            pl.BlockSpec((tm, d), lambda i, k: (i, 0)),
            pl.BlockSpec((1, d), lambda i, k: (0, 0)),
            pl.BlockSpec((ke, d), lambda i, k: (k, 0)),
            pl.BlockSpec((d, ke), lambda i, k: (0, k)),
            tspec, tspec, tspec, tspec,
        ],
        out_specs=pl.BlockSpec((tm, d), lambda i, k: (i, 0)),
        scratch_shapes=[pltpu.VMEM((d, tm), BF16), pltpu.VMEM((d, tm), F32)],
        compiler_params=_cparams(("parallel", "arbitrary")),
        name="peer_experts",
    )(h2d, gamma.reshape(1, d), u_bf16, vt_bf16, *tables)


def _reorder_w_in(w_in):
    d = w_in.shape[0]
    nsa_w = NSA_HEADS * HEAD_DIM
    kv_w = NSA_GROUPS * HEAD_DIM
    sb_w = SB_HEADS * HEAD_DIM
    c0 = nsa_w
    c1 = c0 + 6 * kv_w
    c2 = c1 + 3 * NSA_HEADS
    c3 = c2 + 3 * sb_w
    gn = w_in[:, c1:c2].reshape(d, 3, NSA_GROUPS, NSA_HPG).transpose(0, 2, 1, 3)
    gn = jnp.pad(gn.reshape(d, NSA_GROUPS, 3 * NSA_HPG), ((0, 0), (0, 0), (0, 16 - 3 * NSA_HPG)))
    gn = jnp.pad(gn.reshape(d, NSA_GROUPS * 16), ((0, 0), (0, LANES - NSA_GROUPS * 16)))
    return jnp.concatenate([w_in[:, c3:], w_in[:, :c0], w_in[:, c0:c1], w_in[:, c2:c3], gn], axis=1).astype(BF16)


def _overlap_t(seq):
    ncp = seq // CMP_STRIDE
    nc = (seq - CMP_LEN) // CMP_STRIDE + 1
    ns = seq // SEL_BLOCK
    c_start = jnp.arange(ncp) * CMP_STRIDE
    j_start = jnp.arange(ns) * SEL_BLOCK
    ov = jnp.clip(jnp.minimum(c_start[None, :] + CMP_LEN, j_start[:, None] + SEL_BLOCK)
                  - jnp.maximum(c_start[None, :], j_start[:, None]), 0).astype(F32) / CMP_LEN
    return jnp.where(jnp.arange(ncp)[None, :] < nc, ov, 0.0).astype(BF16)


def _token_mixers(x, norm1_g, w_in, q_norm_g, k_norm_g, cmp_pos_k, cmp_pos_v, cmp_k_w1, cmp_k_w2,
                  cmp_v_w1, cmp_v_w2, w_branch_nsa, w_branch_sb, w_out):
    batch, seq, d = x.shape
    t = batch * seq
    x2d = x.reshape(t, d)
    scale = 1.0 / math.sqrt(HEAD_DIM)

    proj = _norm_matmul(x2d, norm1_g, _reorder_w_in(w_in), tm=1024, tn=13 * LANES, name="in_proj")

    cos, sin = _rope_tables(jnp.arange(seq))
    qg = jnp.tile(q_norm_g.astype(F32), 2) * scale
    gammas = jnp.stack([qg] * 4 + [jnp.tile(k_norm_g[1].astype(F32), 2), jnp.tile(k_norm_g[2].astype(F32), 2)])
    qk = _prep_norm_rope(proj, cos, sin, gammas.reshape(6, 1, LANES), seq,
                         ts=min(1024, seq)).reshape(batch, seq, 6 * LANES)
    tposed = _prep_transpose(proj, batch, seq, ts=min(2048, seq))

    ncp = seq // CMP_STRIDE
    cosc, sinc = _rope_tables(jnp.arange(ncp) * CMP_STRIDE + CMP_LEN - 1)
    kc = _compress(proj[:, CB_KC * LANES:(CB_KC + 1) * LANES], cmp_pos_k, cmp_k_w1, cmp_k_w2, batch, seq,
                   key_args=(cosc, sinc, jnp.tile(k_norm_g[0].astype(F32), 2).reshape(1, LANES)))
    vct = _compress(proj[:, CB_VC * LANES:(CB_VC + 1) * LANES], cmp_pos_v, cmp_v_w1, cmp_v_w2, batch, seq)

    ont = _nsa(qk, kc, vct, tposed, _overlap_t(seq), batch, seq)
    ost = _sb(proj.reshape(batch, seq, N_CB * LANES), tposed, batch, seq)
    return _merge(ont, ost, proj, x2d, w_branch_nsa.astype(BF16), w_branch_sb.astype(BF16),
                  w_out.astype(BF16), batch, seq, tm=512)


def kernel(x, norm1_g, w_in, q_norm_g, k_norm_g, cmp_pos_k, cmp_pos_v, cmp_k_w1, cmp_k_w2, cmp_v_w1, cmp_v_w2,
           w_branch_nsa, w_branch_sb, w_out, norm2_g, peer_w_query, peer_subkeys, peer_u, peer_v):
    batch, seq, d = x.shape
    h = x.astype(F32)
    for l in range(norm1_g.shape[0]):
        h2d = _token_mixers(h, norm1_g[l], w_in[l], q_norm_g[l], k_norm_g[l], cmp_pos_k[l], cmp_pos_v[l],
                            cmp_k_w1[l], cmp_k_w2[l], cmp_v_w1[l], cmp_v_w2[l],
                            w_branch_nsa[l], w_branch_sb[l], w_out[l])
        qp = _norm_matmul(h2d, norm2_g[l], peer_w_query[l].astype(BF16), tm=1024, tn=1024, name="peer_query")
        tables = _route(qp, peer_subkeys[l])
        h2d = _peer(h2d, norm2_g[l], peer_u[l].astype(BF16), peer_v[l].T.astype(BF16), tables)
        h = h2d.reshape(batch, seq, d)
    return h.astype(x.dtype)
```

```python
import functools
import math

import jax
import jax.numpy as jnp
from jax import lax
from jax.experimental import pallas as pl
from jax.experimental.pallas import tpu as pltpu

F32 = jnp.float32
BF16 = jnp.bfloat16

HEAD_DIM = 64
NSA_HEADS = 8
NSA_GROUPS = 2
NSA_HPG = NSA_HEADS // NSA_GROUPS
CMP_LEN = 32
CMP_STRIDE = 16
CMP_HIDDEN = 256
SEL_BLOCK = 64
SEL_TOPN = 16
WINDOW = 512
SB_HEADS = 8
ROPE_THETA = 10000.0
PEER_HEADS = 8
PEER_NKEYS = 128
PEER_QDIM = 256
PEER_TOPK = 16
EPS = 1e-6
NEG = -1e30

LANES = 128
VMEM_LIMIT = 56 * 1024 * 1024

CB_GATE = 0
CB_Q = 16
CB_KC, CB_VC, CB_KS, CB_VS, CB_KW, CB_VW = 20, 21, 22, 23, 24, 25
CB_QSB, CB_KSB, CB_VSB = 26, 30, 34
CB_GN = 38
N_CB = 39

SB_EXIT = 104.0


def _cparams(sem, vmem=None):
    return pltpu.CompilerParams(dimension_semantics=sem, vmem_limit_bytes=vmem or VMEM_LIMIT)


def _norm_matmul_kernel(x_ref, g_ref, w_ref, o_ref):
    x = x_ref[...]
    ms = jnp.mean(x * x, axis=-1, keepdims=True)
    xn = (x * lax.rsqrt(ms + EPS) * g_ref[...]).astype(BF16)
    o_ref[...] = jnp.dot(xn, w_ref[...], preferred_element_type=F32)


def _norm_matmul(x2d, gamma, w_bf16, tm, tn, name):
    t, d = x2d.shape
    n = w_bf16.shape[1]
    return pl.pallas_call(
        _norm_matmul_kernel,
        out_shape=jax.ShapeDtypeStruct((t, n), F32),
        grid=(n // tn, t // tm),
        in_specs=[
            pl.BlockSpec((tm, d), lambda j, i: (i, 0)),
            pl.BlockSpec((1, d), lambda j, i: (0, 0)),
            pl.BlockSpec((d, tn), lambda j, i: (0, j)),
        ],
        out_specs=pl.BlockSpec((tm, tn), lambda j, i: (i, j)),
        compiler_params=_cparams(("parallel", "parallel")),
        name=name,
    )(x2d, gamma.reshape(1, d), w_bf16)


def _group_mean_sq(x):
    ss = x * x
    r = lax.broadcasted_iota(jnp.int32, (LANES, LANES), 0) // HEAD_DIM
    c = lax.broadcasted_iota(jnp.int32, (LANES, LANES), 1) // HEAD_DIM
    ones_bd = jnp.where(r == c, 1.0, 0.0).astype(BF16)
    hi = ss.astype(BF16)
    lo = (ss - hi.astype(F32)).astype(BF16)
    tot = (jnp.dot(hi, ones_bd, preferred_element_type=F32)
           + jnp.dot(lo, ones_bd, preferred_element_type=F32))
    return tot * (1.0 / HEAD_DIM)


def _norm_rope(x, cos, sin_signed, gamma):
    xn = x * lax.rsqrt(_group_mean_sq(x) + EPS) * gamma
    lane = lax.broadcasted_iota(jnp.int32, xn.shape, 1)
    first_half = (lane % HEAD_DIM) < (HEAD_DIM // 2)
    swapped = jnp.where(first_half, pltpu.roll(xn, LANES - HEAD_DIM // 2, 1), pltpu.roll(xn, HEAD_DIM // 2, 1))
    return xn * cos + swapped * sin_signed


def _rope_tables(pos):
    half = HEAD_DIM // 2
    d = jnp.arange(LANES) % HEAD_DIM
    freqs = ROPE_THETA ** (-(d % half).astype(F32) / half)
    ang = pos.astype(F32)[:, None] * freqs[None, :]
    sign = jnp.where(d < half, -1.0, 1.0).astype(F32)
    return jnp.cos(ang), jnp.sin(ang) * sign[None, :]


def _norm_rope_kernel(x_ref, cos_ref, sin_ref, g_ref, o_ref):
    o_ref[...] = _norm_rope(x_ref[...], cos_ref[...], sin_ref[...], g_ref[0]).astype(o_ref.dtype)


def _prep_norm_rope(proj, cos, sin, gammas, seq, ts):
    t = proj.shape[0]
    n_slab = gammas.shape[0]
    s_tiles = seq // ts
    return pl.pallas_call(
        _norm_rope_kernel,
        out_shape=jax.ShapeDtypeStruct((t, n_slab * LANES), BF16),
        grid=(t // ts, n_slab),
        in_specs=[
            pl.BlockSpec((ts, LANES), lambda i, j: (i, CB_Q + j + 2 * (j // 4) + (j // 5))),
            pl.BlockSpec((ts, LANES), lambda i, j: (i % s_tiles, 0)),
            pl.BlockSpec((ts, LANES), lambda i, j: (i % s_tiles, 0)),
            pl.BlockSpec((1, 1, LANES), lambda i, j: (j, 0, 0)),
        ],
        out_specs=pl.BlockSpec((ts, LANES), lambda i, j: (i, j)),
        compiler_params=_cparams(("parallel", "parallel")),
        name="prep_norm_rope",
    )(proj, cos, sin, gammas)


T_SLABS = (CB_VS, CB_VW, CB_QSB, CB_QSB + 1, CB_QSB + 2, CB_QSB + 3,
           CB_VSB, CB_VSB + 1, CB_VSB + 2, CB_VSB + 3, CB_GN)
TS_VS, TS_VW, TS_QSB, TS_VSB, TS_GN = 0, 1, 2, 6, 10


def _transpose_kernel(cm_ref, x_ref, o_ref):
    del cm_ref
    o_ref[0] = x_ref[...].T.astype(o_ref.dtype)


def _prep_transpose(proj, batch, seq, ts):
    s_tiles = seq // ts
    colmap = jnp.asarray(T_SLABS, jnp.int32)
    return pl.pallas_call(
        _transpose_kernel,
        out_shape=jax.ShapeDtypeStruct((batch, len(T_SLABS) * LANES, seq), BF16),
        grid_spec=pltpu.PrefetchScalarGridSpec(
            num_scalar_prefetch=1,
            grid=(batch, s_tiles, len(T_SLABS)),
            in_specs=[pl.BlockSpec((ts, LANES), lambda b, i, j, cm: (b * s_tiles + i, cm[j]))],
            out_specs=pl.BlockSpec((1, LANES, ts), lambda b, i, j, cm: (b, j, i)),
        ),
        compiler_params=_cparams(("parallel", "parallel", "parallel")),
        name="prep_transpose",
    )(colmap, proj)


def _compress_kernel(a_ref, pa_ref, pb_ref, w1a_ref, w1b_ref, w2_ref, *rest, is_key):
    a = a_ref[0]
    ncp = a.shape[0]
    p = jnp.dot((a + pa_ref[...]).astype(BF16), w1a_ref[...], preferred_element_type=F32)
    q = jnp.dot((a + pb_ref[...]).astype(BF16), w1b_ref[...], preferred_element_type=F32)
    hid = jax.nn.gelu(p + pltpu.roll(q, ncp - 1, 0))
    if is_key:
        cos_ref, sin_ref, g_ref, o_ref = rest
        out = jnp.dot(hid.astype(BF16), w2_ref[...], preferred_element_type=F32)
        o_ref[0] = _norm_rope(out, cos_ref[...], sin_ref[...], g_ref[...]).astype(o_ref.dtype)
    else:
        (o_ref,) = rest
        nt_dims = (((1,), (1,)), ((), ()))
        out_t = lax.dot_general(w2_ref[...], hid.astype(BF16), nt_dims, preferred_element_type=F32)
        o_ref[0] = out_t.astype(o_ref.dtype)


def _expand_cmp_weights(w1, w2):
    eye = jnp.eye(NSA_GROUPS, dtype=F32)
    w1r = w1.reshape(2, CMP_STRIDE, HEAD_DIM, CMP_HIDDEN)
    ex = w1r[:, :, None, :, None, :] * eye[None, None, :, None, :, None]
    ex = ex.reshape(2, CMP_STRIDE * LANES, NSA_GROUPS * CMP_HIDDEN).astype(BF16)
    w2x = (w2[None, :, None, :] * eye[:, None, :, None]).reshape(NSA_GROUPS * CMP_HIDDEN, LANES).astype(BF16)
    return ex[0], ex[1], w2x


def _compress(tok_slab, pos_emb, w1, w2, batch, seq, key_args=None):
    ncp = seq // CMP_STRIDE
    a = tok_slab.reshape(batch, ncp, CMP_STRIDE * LANES)
    pos2 = jnp.tile(pos_emb.astype(F32)[:, None, :], (1, NSA_GROUPS, 1)).reshape(2, 1, CMP_STRIDE * LANES)
    w1a, w1b, w2x = _expand_cmp_weights(w1, w2)
    kdim = CMP_STRIDE * LANES
    hdim = NSA_GROUPS * CMP_HIDDEN
    if key_args is None:
        w2x = w2x.T
    in_specs = [
        pl.BlockSpec((1, ncp, kdim), lambda b: (b, 0, 0)),
        pl.BlockSpec((1, kdim), lambda b: (0, 0)),
        pl.BlockSpec((1, kdim), lambda b: (0, 0)),
        pl.BlockSpec((kdim, hdim), lambda b: (0, 0)),
        pl.BlockSpec((kdim, hdim), lambda b: (0, 0)),
        pl.BlockSpec(w2x.shape, lambda b: (0, 0)),
    ]
    args = [a, pos2[0], pos2[1], w1a, w1b, w2x]
    if key_args is not None:
        cos, sin, gamma = key_args
        in_specs += [pl.BlockSpec((ncp, LANES), lambda b: (0, 0)),
                     pl.BlockSpec((ncp, LANES), lambda b: (0, 0)),
                     pl.BlockSpec((1, LANES), lambda b: (0, 0))]
        args += [cos, sin, gamma]
        out_shape = jax.ShapeDtypeStruct((batch, ncp, LANES), BF16)
        out_spec = pl.BlockSpec((1, ncp, LANES), lambda b: (b, 0, 0))
    else:
        out_shape = jax.ShapeDtypeStruct((batch, LANES, ncp), BF16)
        out_spec = pl.BlockSpec((1, LANES, ncp), lambda b: (b, 0, 0))
    return pl.pallas_call(
        functools.partial(_compress_kernel, is_key=key_args is not None),
        out_shape=out_shape,
        grid=(batch,),
        in_specs=in_specs,
        out_specs=out_spec,
        compiler_params=_cparams(("parallel",)),
        name="compress_k" if key_args is not None else "compress_v",
    )(*args)


NSA_TQ = 256
NSA_KB = 1024


def _flash_step(k_tile, vt_tile, qtm, bias, carry):
    m, l, acc = carry
    s = jnp.dot(k_tile, qtm, preferred_element_type=F32) + bias
    m_new = jnp.maximum(m, jnp.max(s, axis=0, keepdims=True))
    alpha = jnp.exp(m - m_new)
    p = jnp.exp(s - m_new).astype(BF16)
    vt_ones = jnp.concatenate([vt_tile, jnp.ones((16, vt_tile.shape[1]), BF16)], axis=0)
    pv = jnp.dot(vt_ones, p, preferred_element_type=F32)
    l = alpha * l + pv[HEAD_DIM:HEAD_DIM + 1]
    acc = alpha * acc + pv[:HEAD_DIM]
    return m_new, l, acc


def _nsa_kernel(q_ref, kc_ref, vct_ref, ks_ref, kw_ref, vst_ref, vwt_ref, gate_ref, ovt_ref,
                o_ref, selb_ref, *, n_sel):
    g = pl.program_id(1)
    i = pl.program_id(2)
    tq, kb, hpg = NSA_TQ, NSA_KB, NSA_HPG
    n = hpg * tq
    t0 = i * tq
    grow = pl.multiple_of(g * HEAD_DIM, HEAD_DIM)

    qt = q_ref[0].astype(F32).T
    rowgrp = lax.broadcasted_iota(jnp.int32, (LANES, tq), 0) // HEAD_DIM
    parts = []
    for h in range(hpg):
        blk = qt[h * HEAD_DIM:(h + 1) * HEAD_DIM]
        parts.append(jnp.where(rowgrp == g, jnp.concatenate([blk, blk], axis=0), 0.0))
    qtm = jnp.concatenate(parts, axis=1).astype(BF16)

    tok1 = t0 + lax.broadcasted_iota(jnp.int32, (1, tq), 1)
    tokn = jnp.concatenate([tok1] * hpg, axis=1)

    ncp = kc_ref.shape[1]
    sc = jnp.dot(kc_ref[0], qtm, preferred_element_type=F32)
    cend = lax.broadcasted_iota(jnp.int32, (ncp, n), 0) * CMP_STRIDE + (CMP_LEN - 1)
    sc = jnp.where(cend <= tokn, sc, NEG)
    mc = jnp.max(sc, axis=0, keepdims=True)
    ec = jnp.exp(sc - mc)
    has_c = jnp.where(tokn >= CMP_LEN - 1, 1.0, 0.0)
    pc = ec * (has_c / jnp.sum(ec, axis=0, keepdims=True))
    o_cmp = jnp.dot(vct_ref[0, pl.ds(grow, HEAD_DIM), :], pc.astype(BF16), preferred_element_type=F32)

    psum = pc[:, 0:tq]
    for h in range(1, hpg):
        psum = psum + pc[:, h * tq:(h + 1) * tq]
    p_hi = psum.astype(BF16)
    p_lo = (psum - p_hi.astype(F32)).astype(BF16)
    imp = (jnp.dot(ovt_ref[...], p_hi, preferred_element_type=F32)
           + jnp.dot(ovt_ref[...], p_lo, preferred_element_type=F32))
    ns = imp.shape[0]
    jidx = lax.broadcasted_iota(jnp.int32, (ns, tq), 0)
    jf = jidx.astype(F32)
    tokb = jnp.broadcast_to(tok1, (ns, tq))
    cur = tokb // SEL_BLOCK
    forced = (jidx == 0) | (jidx == cur) | (jidx == cur - 1)
    valid = jidx * SEL_BLOCK <= tokb
    score = jnp.where(forced, 1e4, jnp.where(valid, imp, -1e4))
    selb = jnp.full((ns, tq), NEG, F32)
    for _ in range(n_sel):
        mx = jnp.max(score, axis=0, keepdims=True)
        first = jnp.min(jnp.where(score == mx, jf, float(ns)), axis=0, keepdims=True)
        pick = jf == first
        selb = jnp.where(pick, 0.0, selb)
        score = jnp.where(pick, -jnp.inf, score)
    selb_ref[...] = selb

    wlen = WINDOW + tq
    w0 = pl.multiple_of(jnp.maximum(t0 - WINDOW, 0), tq)
    wpos = w0 + lax.broadcasted_iota(jnp.int32, (wlen, tq), 0)
    wbias1 = jnp.where((wpos <= tok1) & (wpos > tok1 - WINDOW), 0.0, NEG)
    sw = (jnp.dot(kw_ref[0, pl.ds(w0, wlen), :], qtm, preferred_element_type=F32)
          + jnp.concatenate([wbias1] * hpg, axis=1))
    pw = jnp.exp(sw - jnp.max(sw, axis=0, keepdims=True)).astype(BF16)
    vw_ones = jnp.concatenate([vwt_ref[0, pl.ds(grow, HEAD_DIM), pl.ds(w0, wlen)],
                               jnp.ones((16, wlen), BF16)], axis=0)
    pvw = jnp.dot(vw_ones, pw, preferred_element_type=F32)
    o_win = pvw[:HEAD_DIM] / pvw[HEAD_DIM:HEAD_DIM + 1]

    krow = lax.broadcasted_iota(jnp.int32, (kb, tq), 0)
    init = (jnp.full((1, n), NEG, F32), jnp.zeros((1, n), F32), jnp.zeros((HEAD_DIM, n), F32))
    nblk = kb // SEL_BLOCK
    def sel_body(k, carry):
        k0 = pl.multiple_of(k * kb, kb)
        rows = [jnp.broadcast_to(selb_ref[pl.ds(k * nblk + r, 1), :], (SEL_BLOCK, tq)) for r in range(nblk)]
        bias1 = jnp.where(k0 + krow <= tok1, jnp.concatenate(rows, axis=0), NEG)
        bias = jnp.concatenate([bias1] * hpg, axis=1)
        return _flash_step(ks_ref[0, pl.ds(k0, kb), :], vst_ref[0, pl.ds(grow, HEAD_DIM), pl.ds(k0, kb)],
                           qtm, bias, carry)
    _, l_s, acc_s = lax.fori_loop(0, t0 // kb + 1, sel_body, init)

    gt = jax.nn.sigmoid(gate_ref[0, pl.ds(pl.multiple_of(g * 16, 16), 16), :].astype(F32))
    o_sel = acc_s / l_s
    for h in range(hpg):
        sl = slice(h * tq, (h + 1) * tq)
        o_h = (gt[h:h + 1] * o_cmp[:, sl] + gt[hpg + h:hpg + h + 1] * o_sel[:, sl]
               + gt[2 * hpg + h:2 * hpg + h + 1] * o_win[:, sl])
        o_ref[0, h * HEAD_DIM:(h + 1) * HEAD_DIM, :] = o_h.astype(o_ref.dtype)


def _nsa(qk, kc, vct, tposed, ovt, batch, seq):
    tq = NSA_TQ
    ncp = seq // CMP_STRIDE
    ns = seq // SEL_BLOCK
    n_sel = min(SEL_TOPN, ns)
    qw = NSA_HPG * HEAD_DIM
    return pl.pallas_call(
        functools.partial(_nsa_kernel, n_sel=n_sel),
        out_shape=jax.ShapeDtypeStruct((batch, NSA_HEADS * HEAD_DIM, seq), BF16),
        grid=(batch, NSA_GROUPS, seq // tq),
        in_specs=[
            pl.BlockSpec((1, tq, qw), lambda b, g, i: (b, i, g)),
            pl.BlockSpec((1, ncp, LANES), lambda b, g, i: (b, 0, 0)),
            pl.BlockSpec((1, LANES, ncp), lambda b, g, i: (b, 0, 0)),
            pl.BlockSpec((1, seq, LANES), lambda b, g, i: (b, 0, 4)),
            pl.BlockSpec((1, seq, LANES), lambda b, g, i: (b, 0, 5)),
            pl.BlockSpec((1, LANES, seq), lambda b, g, i: (b, TS_VS, 0)),
            pl.BlockSpec((1, LANES, seq), lambda b, g, i: (b, TS_VW, 0)),
            pl.BlockSpec((1, LANES, tq), lambda b, g, i: (b, TS_GN, i)),
            pl.BlockSpec((ns, ncp), lambda b, g, i: (0, 0)),
        ],
        out_specs=pl.BlockSpec((1, qw, tq), lambda b, g, i: (b, g, i)),
        scratch_shapes=[pltpu.VMEM((ns, tq), F32)],
        compiler_params=_cparams(("parallel", "parallel", "arbitrary")),
        name="nsa",
    )(qk, kc, vct, qk, qk, tposed, tposed, tposed, ovt)


SB_TQ = 256


SB_SLABS = 2


def _sb_kernel(qt_ref, k_ref, vt_ref, o_ref):
    i = pl.program_id(2)
    tq = SB_TQ
    kb = tq
    nh = 2 * SB_SLABS
    n = nh * tq
    scale = 1.0 / math.sqrt(HEAD_DIM)

    rowgrp = lax.broadcasted_iota(jnp.int32, (LANES, tq), 0) // HEAD_DIM
    qtm = []
    for p in range(SB_SLABS):
        q2 = qt_ref[0, p * LANES:(p + 1) * LANES, :].astype(F32) * scale
        qtm.append(jnp.concatenate([jnp.where(rowgrp == 0, q2, 0.0), jnp.where(rowgrp == 1, q2, 0.0)],
                                   axis=1).astype(BF16))

    r = lax.broadcasted_iota(jnp.int32, (kb, kb), 0)
    c = lax.broadcasted_iota(jnp.int32, (kb, kb), 1)
    later = jnp.where(c > r, 1.0, 0.0).astype(BF16)
    krow = lax.broadcasted_iota(jnp.int32, (kb, n), 0)
    tcol = lax.broadcasted_iota(jnp.int32, (kb, n), 1) % tq

    def body(st):
        k, carry, accs, _ = st
        k0 = pl.multiple_of(k * kb, kb)
        kt = k_ref[0, pl.ds(k0, kb), :].astype(BF16)
        z = jnp.concatenate([jnp.dot(kt[:, p * LANES:(p + 1) * LANES], qtm[p], preferred_element_type=F32)
                             for p in range(SB_SLABS)], axis=1)
        sp = jnp.maximum(z, 0.0) + jnp.log(1.0 + jnp.exp(-jnp.abs(z)))
        mask = (k0 + krow) < (i * tq + tcol)
        spm = jnp.where(mask, sp, 0.0)
        hi = spm.astype(BF16)
        lo = (spm - hi.astype(F32)).astype(BF16)
        after = (jnp.dot(later, hi, preferred_element_type=F32)
                 + jnp.dot(later, lo, preferred_element_type=F32))
        a = jnp.where(mask, jnp.exp(z - sp - after - carry), 0.0).astype(BF16)
        accs = tuple(accs[h] + jnp.dot(vt_ref[0, h * HEAD_DIM:(h + 1) * HEAD_DIM, pl.ds(k0, kb)],
                                       a[:, h * tq:(h + 1) * tq], preferred_element_type=F32)
                     for h in range(nh))
        carry = carry + after[0:1] + spm[0:1]
        go = jnp.logical_and(k > 0, jnp.min(carry) <= SB_EXIT)
        return k - 1, carry, accs, go

    zacc = jnp.zeros((HEAD_DIM, tq), F32)
    st = (i, jnp.zeros((1, n), F32), (zacc,) * nh, i >= 0)
    _, _, accs, _ = lax.while_loop(lambda s: s[3], body, st)
    for h in range(nh):
        o_ref[0, h * HEAD_DIM:(h + 1) * HEAD_DIM, :] = accs[h].astype(o_ref.dtype)


def _sb(proj3, tposed, batch, seq):
    tq = SB_TQ
    w = SB_SLABS * LANES
    assert TS_QSB % SB_SLABS == 0 and TS_VSB % SB_SLABS == 0 and CB_KSB % SB_SLABS == 0
    return pl.pallas_call(
        _sb_kernel,
        out_shape=jax.ShapeDtypeStruct((batch, SB_HEADS * HEAD_DIM, seq), BF16),
        grid=(batch, SB_HEADS // (2 * SB_SLABS), seq // tq),
        in_specs=[
            pl.BlockSpec((1, w, tq), lambda b, h, i: (b, TS_QSB // SB_SLABS + h, i)),
            pl.BlockSpec((1, seq, w), lambda b, h, i: (b, 0, CB_KSB // SB_SLABS + h)),
            pl.BlockSpec((1, w, seq), lambda b, h, i: (b, TS_VSB // SB_SLABS + h, 0)),
        ],
        out_specs=pl.BlockSpec((1, w, tq), lambda b, h, i: (b, h, i)),
        compiler_params=_cparams(("parallel", "parallel", "arbitrary")),
        name="stickbreaking",
    )(tposed, proj3, tposed)


def _merge_kernel(ont_ref, ost_ref, ga_ref, gb_ref, x_ref, wa_ref, wb_ref, wo_ref, o_ref):
    tn_dims = (((0,), (0,)), ((), ()))
    a = lax.dot_general(ont_ref[0], wa_ref[...], tn_dims, preferred_element_type=F32)
    b = lax.dot_general(ost_ref[0], wb_ref[...], tn_dims, preferred_element_type=F32)
    merged = jax.nn.sigmoid(ga_ref[...]) * a + jax.nn.sigmoid(gb_ref[...]) * b
    o_ref[...] = x_ref[...] + jnp.dot(merged.astype(BF16), wo_ref[...], preferred_element_type=F32)


def _merge(ont, ost, proj, x2d, wa, wb, wo, batch, seq, tm):
    d = x2d.shape[1]
    s_tiles = seq // tm
    hw = ont.shape[1]
    return pl.pallas_call(
        _merge_kernel,
        out_shape=jax.ShapeDtypeStruct(x2d.shape, F32),
        grid=(batch, s_tiles),
        in_specs=[
            pl.BlockSpec((1, hw, tm), lambda b, i: (b, 0, i)),
            pl.BlockSpec((1, hw, tm), lambda b, i: (b, 0, i)),
            pl.BlockSpec((tm, d), lambda b, i: (b * s_tiles + i, 0)),
            pl.BlockSpec((tm, d), lambda b, i: (b * s_tiles + i, 1)),
            pl.BlockSpec((tm, d), lambda b, i: (b * s_tiles + i, 0)),
            pl.BlockSpec((hw, d), lambda b, i: (0, 0)),
            pl.BlockSpec((hw, d), lambda b, i: (0, 0)),
            pl.BlockSpec((d, d), lambda b, i: (0, 0)),
        ],
        out_specs=pl.BlockSpec((tm, d), lambda b, i: (b * s_tiles + i, 0)),
        compiler_params=_cparams(("parallel", "parallel")),
        name="merge",
    )(ont, ost, proj, proj, x2d, wa, wb, wo)


PEER_TR = 256


def _topk_ranked(s, idx, k, vals_ref=None, first_ref=None, want_rank=True):
    rows, n = s.shape
    sub = 8
    s = s.reshape(rows // sub, sub, n)
    idx = idx.reshape(rows // sub, sub, n)

    def all_sublanes(x, op):
        for sh in (4, 2, 1):
            x = op(x, pltpu.roll(x, sh, 0))
        return x

    rank = jnp.full(s.shape, float(k), F32) if want_rank else None
    for r in range(k):
        mx = all_sublanes(jnp.max(s, axis=0), jnp.maximum)
        first = all_sublanes(jnp.min(jnp.where(s == mx[None], idx, 1e9), axis=0), jnp.minimum)
        pick = idx == first[None]
        if want_rank:
            rank = jnp.where(pick, float(r), rank)
        s = jnp.where(pick, -jnp.inf, s)
        if vals_ref is not None:
            vals_ref[r:r + 1, :] = mx[0:1]
        if first_ref is not None:
            first_ref[r:r + 1, :] = first[0:1]
    return rank.reshape(rows, n) if want_rank else None


_CAND_GROUPS = ((0, 0, 16), (16, 1, 8), (24, 2, 8), (32, 3, 4), (36, 4, 4), (40, 5, 2), (42, 6, 2), (44, 7, 2))
_CAND_ROWS = 56


def _route_kernel(q_ref, sk_ref, n1_ref, a1_ref, b2_ref, e2_ref, v1_ref, v2_ref, f1_ref, cand_ref):
    k = PEER_TOPK
    half = PEER_QDIM // 2
    nt_dims = (((1,), (1,)), ((), ()))
    qh = q_ref[...].astype(BF16)
    s1 = lax.dot_general(sk_ref[0].astype(BF16), qh[:, :half], nt_dims, preferred_element_type=F32)
    s2 = lax.dot_general(sk_ref[1].astype(BF16), qh[:, half:], nt_dims, preferred_element_type=F32)
    tr = s1.shape[1]
    kidx = lax.broadcasted_iota(jnp.int32, s1.shape, 0).astype(F32)
    _topk_ranked(s1, kidx, k, v1_ref, f1_ref, want_rank=False)
    rank2 = _topk_ranked(s2, kidx, k, v2_ref)
    v1 = v1_ref[...]
    v2 = v2_ref[...]

    flat = lax.broadcasted_iota(jnp.int32, (_CAND_ROWS, tr), 0)
    row = flat
    for r0, a, nb in _CAND_GROUPS:
        cand_ref[r0:r0 + nb, :] = v1[a:a + 1] + v2[0:nb]
        flat = jnp.where((row >= r0) & (row < r0 + nb), row - r0 + a * k, flat)
    cand_ref[46:48, :] = jnp.full((2, tr), -jnp.inf, F32)
    cand_ref[48:56, :] = v1[8:16] + v2[0:1]
    flat = jnp.where(row >= 48, (row - 40) * k, flat).astype(F32)
    cand = cand_ref[...]
    crank = _topk_ranked(cand, flat, k)
    selc = jnp.where(crank < float(k), 1.0, 0.0)
    mx = v1[0:1] + v2[0:1]
    z = jnp.sum(jnp.where(crank < float(k), jnp.exp(cand - mx), 0.0), axis=0, keepdims=True)
    n_a = [jnp.sum(selc[r0:r0 + nb], axis=0, keepdims=True) for r0, _, nb in _CAND_GROUPS]
    n_a += [selc[48 + a - 8:48 + a - 7] for a in range(8, k)]

    f1 = f1_ref[...]
    n1 = jnp.zeros_like(s1)
    for a in range(k):
        n1 = jnp.where(kidx == f1[a:a + 1], n_a[a], n1)
    n1_ref[0] = n1
    a1_ref[0] = jnp.exp(s1 - v1[0:1]) / z
    b2_ref[0] = rank2.astype(b2_ref.dtype)
    e2_ref[0] = jnp.exp(s2 - v2[0:1]).astype(e2_ref.dtype)


def _route(qp, subkeys):
    t = qp.shape[0]
    tr = PEER_TR
    nk = PEER_NKEYS
    shp = jax.ShapeDtypeStruct((PEER_HEADS, nk, t), F32)
    shp16 = jax.ShapeDtypeStruct((PEER_HEADS, nk, t), BF16)
    spec = pl.BlockSpec((1, nk, tr), lambda i, h: (h, 0, i))
    return pl.pallas_call(
        _route_kernel,
        out_shape=(shp, shp, shp16, shp16),
        grid=(t // tr, PEER_HEADS),
        in_specs=[
            pl.BlockSpec((tr, PEER_QDIM), lambda i, h: (i, h)),
            pl.BlockSpec((2, nk, PEER_QDIM // 2), lambda i, h: (0, 0, 0)),
        ],
        out_specs=(spec, spec, spec, spec),
        scratch_shapes=[pltpu.VMEM((PEER_TOPK, tr), F32), pltpu.VMEM((PEER_TOPK, tr), F32),
                        pltpu.VMEM((PEER_TOPK, tr), F32), pltpu.VMEM((_CAND_ROWS, tr), F32)],
        compiler_params=_cparams(("parallel", "parallel")),
        name="peer_route",
    )(qp, subkeys)


PEER_TM = 512
PEER_NB = 8


def _peer_kernel(h_ref, g_ref, u_ref, vt_ref, n1_ref, a1_ref, b2_ref, e2_ref, o_ref, hnt_ref, acc_ref):
    kk = pl.program_id(1)
    nk = PEER_NKEYS

    @pl.when(kk == 0)
    def _():
        h = h_ref[...]
        ms = jnp.mean(h * h, axis=-1, keepdims=True)
        hnt_ref[...] = (h * lax.rsqrt(ms + EPS) * g_ref[...]).T.astype(BF16)
        acc_ref[...] = jnp.zeros_like(acc_ref)

    act = jax.nn.gelu(jnp.dot(u_ref[...], hnt_ref[...], preferred_element_type=F32)).astype(BF16)
    tm = act.shape[1]
    rep = nk // 16

    def bcast_row(ref, hd, i1):
        row16 = jnp.broadcast_to(ref[hd, pl.ds(i1, 1), :], (16, tm)).astype(BF16)
        return jnp.concatenate([row16] * rep, axis=0)

    parts = []
    for j in range(PEER_NB):
        i1 = kk * PEER_NB + j
        w = None
        for hd in range(PEER_HEADS):
            n1 = bcast_row(n1_ref, hd, i1)
            a1 = bcast_row(a1_ref, hd, i1)
            w_hd = jnp.where(b2_ref[hd] < n1, e2_ref[hd] * a1, jnp.zeros((), BF16))
            w = w_hd if w is None else w + w_hd
        parts.append(w * act[j * nk:(j + 1) * nk])
    mt = jnp.concatenate(parts, axis=0)
    acc_ref[...] += jnp.dot(vt_ref[...], mt, preferred_element_type=F32)

    @pl.when(kk == pl.num_programs(1) - 1)
    def _():
        o_ref[...] = h_ref[...] + acc_ref[...].T


def _peer(h2d, gamma, u_bf16, vt_bf16, tables):
    t, d = h2d.shape
    tm = min(PEER_TM, t)
    nk = PEER_NKEYS
    ke = PEER_NB * nk
    tspec = pl.BlockSpec((PEER_HEADS, nk, tm), lambda i, k: (0, 0, i))
    return pl.pallas_call(
        _peer_kernel,
        out_shape=jax.ShapeDtypeStruct((t, d), F32),
        grid=(t // tm, nk // PEER_NB),
        in_specs=[
            pl.BlockSpec((tm, d), lambda i, k: (i, 0)),
            pl.BlockSpec((1, d), lambda i, k: (0, 0)),
            pl.BlockSpec((ke, d), lambda i, k: (k, 0)),
            pl.BlockSpec((d, ke), lambda i, k: (0, k)),
            tspec, tspec, tspec, tspec,
        ],
        out_specs=pl.BlockSpec((tm, d), lambda i, k: (i, 0)),
        scratch_shapes=[pltpu.VMEM((d, tm), BF16), pltpu.VMEM((d, tm), F32)],
        compiler_params=_cparams(("parallel", "arbitrary")),
        name="peer_experts",
    )(h2d, gamma.reshape(1, d), u_bf16, vt_bf16, *tables)


def _reorder_w_in(w_in):
    d = w_in.shape[0]
    nsa_w = NSA_HEADS * HEAD_DIM
    kv_w = NSA_GROUPS * HEAD_DIM
    sb_w = SB_HEADS * HEAD_DIM
    c0 = nsa_w
    c1 = c0 + 6 * kv_w
    c2 = c1 + 3 * NSA_HEADS
    c3 = c2 + 3 * sb_w
    gn = w_in[:, c1:c2].reshape(d, 3, NSA_GROUPS, NSA_HPG).transpose(0, 2, 1, 3)
    gn = jnp.pad(gn.reshape(d, NSA_GROUPS, 3 * NSA_HPG), ((0, 0), (0, 0), (0, 16 - 3 * NSA_HPG)))
    gn = jnp.pad(gn.reshape(d, NSA_GROUPS * 16), ((0, 0), (0, LANES - NSA_GROUPS * 16)))
    return jnp.concatenate([w_in[:, c3:], w_in[:, :c0], w_in[:, c0:c1], w_in[:, c2:c3], gn], axis=1).astype(BF16)


def _overlap_t(seq):
    ncp = seq // CMP_STRIDE
    nc = (seq - CMP_LEN) // CMP_STRIDE + 1
    ns = seq // SEL_BLOCK
    c_start = jnp.arange(ncp) * CMP_STRIDE
    j_start = jnp.arange(ns) * SEL_BLOCK
    ov = jnp.clip(jnp.minimum(c_start[None, :] + CMP_LEN, j_start[:, None] + SEL_BLOCK)
                  - jnp.maximum(c_start[None, :], j_start[:, None]), 0).astype(F32) / CMP_LEN
    return jnp.where(jnp.arange(ncp)[None, :] < nc, ov, 0.0).astype(BF16)


def _token_mixers(x, norm1_g, w_in, q_norm_g, k_norm_g, cmp_pos_k, cmp_pos_v, cmp_k_w1, cmp_k_w2,
                  cmp_v_w1, cmp_v_w2, w_branch_nsa, w_branch_sb, w_out):
    batch, seq, d = x.shape
    t = batch * seq
    x2d = x.reshape(t, d)
    scale = 1.0 / math.sqrt(HEAD_DIM)

    proj = _norm_matmul(x2d, norm1_g, _reorder_w_in(w_in), tm=1024, tn=13 * LANES, name="in_proj")

    cos, sin = _rope_tables(jnp.arange(seq))
    qg = jnp.tile(q_norm_g.astype(F32), 2) * scale
    gammas = jnp.stack([qg] * 4 + [jnp.tile(k_norm_g[1].astype(F32), 2), jnp.tile(k_norm_g[2].astype(F32), 2)])
    qk = _prep_norm_rope(proj, cos, sin, gammas.reshape(6, 1, LANES), seq,
                         ts=min(1024, seq)).reshape(batch, seq, 6 * LANES)
    tposed = _prep_transpose(proj, batch, seq, ts=min(2048, seq))

    ncp = seq // CMP_STRIDE
    cosc, sinc = _rope_tables(jnp.arange(ncp) * CMP_STRIDE + CMP_LEN - 1)
    kc = _compress(proj[:, CB_KC * LANES:(CB_KC + 1) * LANES], cmp_pos_k, cmp_k_w1, cmp_k_w2, batch, seq,
                   key_args=(cosc, sinc, jnp.tile(k_norm_g[0].astype(F32), 2).reshape(1, LANES)))
    vct = _compress(proj[:, CB_VC * LANES:(CB_VC + 1) * LANES], cmp_pos_v, cmp_v_w1, cmp_v_w2, batch, seq)

    ont = _nsa(qk, kc, vct, tposed, _overlap_t(seq), batch, seq)
    ost = _sb(proj.reshape(batch, seq, N_CB * LANES), tposed, batch, seq)
    return _merge(ont, ost, proj, x2d, w_branch_nsa.astype(BF16), w_branch_sb.astype(BF16),
                  w_out.astype(BF16), batch, seq, tm=512)


def kernel(x, norm1_g, w_in, q_norm_g, k_norm_g, cmp_pos_k, cmp_pos_v, cmp_k_w1, cmp_k_w2, cmp_v_w1, cmp_v_w2,
           w_branch_nsa, w_branch_sb, w_out, norm2_g, peer_w_query, peer_subkeys, peer_u, peer_v):
    batch, seq, d = x.shape
    h = x.astype(F32)
    for l in range(norm1_g.shape[0]):
        h2d = _token_mixers(h, norm1_g[l], w_in[l], q_norm_g[l], k_norm_g[l], cmp_pos_k[l], cmp_pos_v[l],
                            cmp_k_w1[l], cmp_k_w2[l], cmp_v_w1[l], cmp_v_w2[l],
                            w_branch_nsa[l], w_branch_sb[l], w_out[l])
        qp = _norm_matmul(h2d, norm2_g[l], peer_w_query[l].astype(BF16), tm=1024, tn=1024, name="peer_query")
        tables = _route(qp, peer_subkeys[l])
        h2d = _peer(h2d, norm2_g[l], peer_u[l].astype(BF16), peer_v[l].T.astype(BF16), tables)
        h = h2d.reshape(batch, seq, d)
    return h.astype(x.dtype)
```

```python
import functools
import math

import jax
import jax.numpy as jnp
from jax import lax
from jax.experimental import pallas as pl
from jax.experimental.pallas import tpu as pltpu

F32 = jnp.float32
BF16 = jnp.bfloat16

HEAD_DIM = 64
NSA_HEADS = 8
NSA_GROUPS = 2
NSA_HPG = NSA_HEADS // NSA_GROUPS
CMP_LEN = 32
CMP_STRIDE = 16
CMP_HIDDEN = 256
SEL_BLOCK = 64
SEL_TOPN = 16
WINDOW = 512
SB_HEADS = 8
ROPE_THETA = 10000.0
PEER_HEADS = 8
PEER_NKEYS = 128
PEER_QDIM = 256
PEER_TOPK = 16
EPS = 1e-6
NEG = -1e30

LANES = 128
VMEM_LIMIT = 56 * 1024 * 1024

CB_GATE = 0
CB_Q = 16
CB_KC, CB_VC, CB_KS, CB_VS, CB_KW, CB_VW = 20, 21, 22, 23, 24, 25
CB_QSB, CB_KSB, CB_VSB = 26, 30, 34
CB_GN = 38
N_CB = 39

SB_EXIT = 104.0


def _cparams(sem, vmem=None):
    return pltpu.CompilerParams(dimension_semantics=sem, vmem_limit_bytes=vmem or VMEM_LIMIT)


def _norm_matmul_kernel(x_ref, g_ref, w_ref, o_ref):
    x = x_ref[...]
    ms = jnp.mean(x * x, axis=-1, keepdims=True)
    xn = (x * lax.rsqrt(ms + EPS) * g_ref[...]).astype(BF16)
    o_ref[...] = jnp.dot(xn, w_ref[...], preferred_element_type=F32)


def _norm_matmul(x2d, gamma, w_bf16, tm, tn, name):
    t, d = x2d.shape
    n = w_bf16.shape[1]
    return pl.pallas_call(
        _norm_matmul_kernel,
        out_shape=jax.ShapeDtypeStruct((t, n), F32),
        grid=(n // tn, t // tm),
        in_specs=[
            pl.BlockSpec((tm, d), lambda j, i: (i, 0)),
            pl.BlockSpec((1, d), lambda j, i: (0, 0)),
            pl.BlockSpec((d, tn), lambda j, i: (0, j)),
        ],
        out_specs=pl.BlockSpec((tm, tn), lambda j, i: (i, j)),
        compiler_params=_cparams(("parallel", "parallel")),
        name=name,
    )(x2d, gamma.reshape(1, d), w_bf16)


def _group_mean_sq(x):
    ss = x * x
    r = lax.broadcasted_iota(jnp.int32, (LANES, LANES), 0) // HEAD_DIM
    c = lax.broadcasted_iota(jnp.int32, (LANES, LANES), 1) // HEAD_DIM
    ones_bd = jnp.where(r == c, 1.0, 0.0).astype(BF16)
    hi = ss.astype(BF16)
    lo = (ss - hi.astype(F32)).astype(BF16)
    tot = (jnp.dot(hi, ones_bd, preferred_element_type=F32)
           + jnp.dot(lo, ones_bd, preferred_element_type=F32))
    return tot * (1.0 / HEAD_DIM)


def _norm_rope(x, cos, sin_signed, gamma):
    xn = x * lax.rsqrt(_group_mean_sq(x) + EPS) * gamma
    lane = lax.broadcasted_iota(jnp.int32, xn.shape, 1)
    first_half = (lane % HEAD_DIM) < (HEAD_DIM // 2)
    swapped = jnp.where(first_half, pltpu.roll(xn, LANES - HEAD_DIM // 2, 1), pltpu.roll(xn, HEAD_DIM // 2, 1))
    return xn * cos + swapped * sin_signed


def _rope_tables(pos):
    half = HEAD_DIM // 2
    d = jnp.arange(LANES) % HEAD_DIM
    freqs = ROPE_THETA ** (-(d % half).astype(F32) / half)
    ang = pos.astype(F32)[:, None] * freqs[None, :]
    sign = jnp.where(d < half, -1.0, 1.0).astype(F32)
    return jnp.cos(ang), jnp.sin(ang) * sign[None, :]


def _norm_rope_kernel(x_ref, cos_ref, sin_ref, g_ref, o_ref):
    o_ref[...] = _norm_rope(x_ref[...], cos_ref[...], sin_ref[...], g_ref[0]).astype(o_ref.dtype)


def _prep_norm_rope(proj, cos, sin, gammas, seq, ts):
    t = proj.shape[0]
    n_slab = gammas.shape[0]
    s_tiles = seq // ts
    return pl.pallas_call(
        _norm_rope_kernel,
        out_shape=jax.ShapeDtypeStruct((t, n_slab * LANES), BF16),
        grid=(t // ts, n_slab),
        in_specs=[
            pl.BlockSpec((ts, LANES), lambda i, j: (i, CB_Q + j + 2 * (j // 4) + (j // 5))),
            pl.BlockSpec((ts, LANES), lambda i, j: (i % s_tiles, 0)),
            pl.BlockSpec((ts, LANES), lambda i, j: (i % s_tiles, 0)),
            pl.BlockSpec((1, 1, LANES), lambda i, j: (j, 0, 0)),
        ],
        out_specs=pl.BlockSpec((ts, LANES), lambda i, j: (i, j)),
        compiler_params=_cparams(("parallel", "parallel")),
        name="prep_norm_rope",
    )(proj, cos, sin, gammas)


T_SLABS = (CB_VS, CB_VW, CB_QSB, CB_QSB + 1, CB_QSB + 2, CB_QSB + 3,
           CB_VSB, CB_VSB + 1, CB_VSB + 2, CB_VSB + 3, CB_GN)
TS_VS, TS_VW, TS_QSB, TS_VSB, TS_GN = 0, 1, 2, 6, 10


def _transpose_kernel(cm_ref, x_ref, o_ref):
    del cm_ref
    o_ref[0] = x_ref[...].T.astype(o_ref.dtype)


def _prep_transpose(proj, batch, seq, ts):
    s_tiles = seq // ts
    colmap = jnp.asarray(T_SLABS, jnp.int32)
    return pl.pallas_call(
        _transpose_kernel,
        out_shape=jax.ShapeDtypeStruct((batch, len(T_SLABS) * LANES, seq), BF16),
        grid_spec=pltpu.PrefetchScalarGridSpec(
            num_scalar_prefetch=1,
            grid=(batch, s_tiles, len(T_SLABS)),
            in_specs=[pl.BlockSpec((ts, LANES), lambda b, i, j, cm: (b * s_tiles + i, cm[j]))],
            out_specs=pl.BlockSpec((1, LANES, ts), lambda b, i, j, cm: (b, j, i)),
        ),
        compiler_params=_cparams(("parallel", "parallel", "parallel")),
        name="prep_transpose",
    )(colmap, proj)


def _compress_kernel(a_ref, pa_ref, pb_ref, w1a_ref, w1b_ref, w2_ref, *rest, is_key):
    a = a_ref[0]
    ncp = a.shape[0]
    p = jnp.dot((a + pa_ref[...]).astype(BF16), w1a_ref[...], preferred_element_type=F32)
    q = jnp.dot((a + pb_ref[...]).astype(BF16), w1b_ref[...], preferred_element_type=F32)
    hid = jax.nn.gelu(p + pltpu.roll(q, ncp - 1, 0))
    if is_key:
        cos_ref, sin_ref, g_ref, o_ref = rest
        out = jnp.dot(hid.astype(BF16), w2_ref[...], preferred_element_type=F32)
        o_ref[0] = _norm_rope(out, cos_ref[...], sin_ref[...], g_ref[...]).astype(o_ref.dtype)
    else:
        (o_ref,) = rest
        nt_dims = (((1,), (1,)), ((), ()))
        out_t = lax.dot_general(w2_ref[...], hid.astype(BF16), nt_dims, preferred_element_type=F32)
        o_ref[0] = out_t.astype(o_ref.dtype)


def _expand_cmp_weights(w1, w2):
    eye = jnp.eye(NSA_GROUPS, dtype=F32)
    w1r = w1.reshape(2, CMP_STRIDE, HEAD_DIM, CMP_HIDDEN)
    ex = w1r[:, :, None, :, None, :] * eye[None, None, :, None, :, None]
    ex = ex.reshape(2, CMP_STRIDE * LANES, NSA_GROUPS * CMP_HIDDEN).astype(BF16)
    w2x = (w2[None, :, None, :] * eye[:, None, :, None]).reshape(NSA_GROUPS * CMP_HIDDEN, LANES).astype(BF16)
    return ex[0], ex[1], w2x


def _compress(tok_slab, pos_emb, w1, w2, batch, seq, key_args=None):
    ncp = seq // CMP_STRIDE
    a = tok_slab.reshape(batch, ncp, CMP_STRIDE * LANES)
    pos2 = jnp.tile(pos_emb.astype(F32)[:, None, :], (1, NSA_GROUPS, 1)).reshape(2, 1, CMP_STRIDE * LANES)
    w1a, w1b, w2x = _expand_cmp_weights(w1, w2)
    kdim = CMP_STRIDE * LANES
    hdim = NSA_GROUPS * CMP_HIDDEN
    if key_args is None:
        w2x = w2x.T
    in_specs = [
        pl.BlockSpec((1, ncp, kdim), lambda b: (b, 0, 0)),
        pl.BlockSpec((1, kdim), lambda b: (0, 0)),
        pl.BlockSpec((1, kdim), lambda b: (0, 0)),
        pl.BlockSpec((kdim, hdim), lambda b: (0, 0)),
        pl.BlockSpec((kdim, hdim), lambda b: (0, 0)),
        pl.BlockSpec(w2x.shape, lambda b: (0, 0)),
    ]
    args = [a, pos2[0], pos2[1], w1a, w1b, w2x]
    if key_args is not None:
        cos, sin, gamma = key_args
        in_specs += [pl.BlockSpec((ncp, LANES), lambda b: (0, 0)),
                     pl.BlockSpec((ncp, LANES), lambda b: (0, 0)),
                     pl.BlockSpec((1, LANES), lambda b: (0, 0))]
        args += [cos, sin, gamma]
        out_shape = jax.ShapeDtypeStruct((batch, ncp, LANES), BF16)
        out_spec = pl.BlockSpec((1, ncp, LANES), lambda b: (b, 0, 0))
    else:
        out_shape = jax.ShapeDtypeStruct((batch, LANES, ncp), BF16)
        out_spec = pl.BlockSpec((1, LANES, ncp), lambda b: (b, 0, 0))
    return pl.pallas_call(
        functools.partial(_compress_kernel, is_key=key_args is not None),
        out_shape=out_shape,
        grid=(batch,),
        in_specs=in_specs,
        out_specs=out_spec,
        compiler_params=_cparams(("parallel",)),
        name="compress_k" if key_args is not None else "compress_v",
    )(*args)


NSA_TQ = 256
NSA_KB = 1024


def _flash_step(k_tile, vt_tile, qtm, bias, carry):
    m, l, acc = carry
    s = jnp.dot(k_tile, qtm, preferred_element_type=F32) + bias
    m_new = jnp.maximum(m, jnp.max(s, axis=0, keepdims=True))
    alpha = jnp.exp(m - m_new)
    p = jnp.exp(s - m_new).astype(BF16)
    vt_ones = jnp.concatenate([vt_tile, jnp.ones((16, vt_tile.shape[1]), BF16)], axis=0)
    pv = jnp.dot(vt_ones, p, preferred_element_type=F32)
    l = alpha * l + pv[HEAD_DIM:HEAD_DIM + 1]
    acc = alpha * acc + pv[:HEAD_DIM]
    return m_new, l, acc


def _nsa_kernel(q_ref, kc_ref, vct_ref, ks_ref, kw_ref, vst_ref, vwt_ref, gate_ref, ovt_ref,
                o_ref, selb_ref, *, n_sel):
    g = pl.program_id(1)
    i = pl.program_id(2)
    tq, kb, hpg = NSA_TQ, NSA_KB, NSA_HPG
    n = hpg * tq
    t0 = i * tq
    grow = pl.multiple_of(g * HEAD_DIM, HEAD_DIM)

    qt = q_ref[0].astype(F32).T
    rowgrp = lax.broadcasted_iota(jnp.int32, (LANES, tq), 0) // HEAD_DIM
    parts = []
    for h in range(hpg):
        blk = qt[h * HEAD_DIM:(h + 1) * HEAD_DIM]
        parts.append(jnp.where(rowgrp == g, jnp.concatenate([blk, blk], axis=0), 0.0))
    qtm = jnp.concatenate(parts, axis=1).astype(BF16)

    tok1 = t0 + lax.broadcasted_iota(jnp.int32, (1, tq), 1)
    tokn = jnp.concatenate([tok1] * hpg, axis=1)

    ncp = kc_ref.shape[1]
    sc = jnp.dot(kc_ref[0], qtm, preferred_element_type=F32)
    cend = lax.broadcasted_iota(jnp.int32, (ncp, n), 0) * CMP_STRIDE + (CMP_LEN - 1)
    sc = jnp.where(cend <= tokn, sc, NEG)
    mc = jnp.max(sc, axis=0, keepdims=True)
    ec = jnp.exp(sc - mc)
    has_c = jnp.where(tokn >= CMP_LEN - 1, 1.0, 0.0)
    pc = ec * (has_c / jnp.sum(ec, axis=0, keepdims=True))
    o_cmp = jnp.dot(vct_ref[0, pl.ds(grow, HEAD_DIM), :], pc.astype(BF16), preferred_element_type=F32)

    psum = pc[:, 0:tq]
    for h in range(1, hpg):
        psum = psum + pc[:, h * tq:(h + 1) * tq]
    p_hi = psum.astype(BF16)
    p_lo = (psum - p_hi.astype(F32)).astype(BF16)
    imp = (jnp.dot(ovt_ref[...], p_hi, preferred_element_type=F32)
           + jnp.dot(ovt_ref[...], p_lo, preferred_element_type=F32))
    ns = imp.shape[0]
    jidx = lax.broadcasted_iota(jnp.int32, (ns, tq), 0)
    jf = jidx.astype(F32)
    tokb = jnp.broadcast_to(tok1, (ns, tq))
    cur = tokb // SEL_BLOCK
    forced = (jidx == 0) | (jidx == cur) | (jidx == cur - 1)
    valid = jidx * SEL_BLOCK <= tokb
    score = jnp.where(forced, 1e4, jnp.where(valid, imp, -1e4))
    selb = jnp.full((ns, tq), NEG, F32)
    for _ in range(n_sel):
        mx = jnp.max(score, axis=0, keepdims=True)
        first = jnp.min(jnp.where(score == mx, jf, float(ns)), axis=0, keepdims=True)
        pick = jf == first
        selb = jnp.where(pick, 0.0, selb)
        score = jnp.where(pick, -jnp.inf, score)
    selb_ref[...] = selb

    wlen = WINDOW + tq
    w0 = pl.multiple_of(jnp.maximum(t0 - WINDOW, 0), tq)
    wpos = w0 + lax.broadcasted_iota(jnp.int32, (wlen, tq), 0)
    wbias1 = jnp.where((wpos <= tok1) & (wpos > tok1 - WINDOW), 0.0, NEG)
    sw = (jnp.dot(kw_ref[0, pl.ds(w0, wlen), :], qtm, preferred_element_type=F32)
          + jnp.concatenate([wbias1] * hpg, axis=1))
    pw = jnp.exp(sw - jnp.max(sw, axis=0, keepdims=True)).astype(BF16)
    vw_ones = jnp.concatenate([vwt_ref[0, pl.ds(grow, HEAD_DIM), pl.ds(w0, wlen)],
                               jnp.ones((16, wlen), BF16)], axis=0)
    pvw = jnp.dot(vw_ones, pw, preferred_element_type=F32)
    o_win = pvw[:HEAD_DIM] / pvw[HEAD_DIM:HEAD_DIM + 1]

    krow = lax.broadcasted_iota(jnp.int32, (kb, tq), 0)
    init = (jnp.full((1, n), NEG, F32), jnp.zeros((1, n), F32), jnp.zeros((HEAD_DIM, n), F32))
    nblk = kb // SEL_BLOCK
    def sel_body(k, carry):
        k0 = pl.multiple_of(k * kb, kb)
        rows = [jnp.broadcast_to(selb_ref[pl.ds(k * nblk + r, 1), :], (SEL_BLOCK, tq)) for r in range(nblk)]
        bias1 = jnp.where(k0 + krow <= tok1, jnp.concatenate(rows, axis=0), NEG)
        bias = jnp.concatenate([bias1] * hpg, axis=1)
        return _flash_step(ks_ref[0, pl.ds(k0, kb), :], vst_ref[0, pl.ds(grow, HEAD_DIM), pl.ds(k0, kb)],
                           qtm, bias, carry)
    _, l_s, acc_s = lax.fori_loop(0, t0 // kb + 1, sel_body, init)

    gt = jax.nn.sigmoid(gate_ref[0, pl.ds(pl.multiple_of(g * 16, 16), 16), :].astype(F32))
    o_sel = acc_s / l_s
    for h in range(hpg):
        sl = slice(h * tq, (h + 1) * tq)
        o_h = (gt[h:h + 1] * o_cmp[:, sl] + gt[hpg + h:hpg + h + 1] * o_sel[:, sl]
               + gt[2 * hpg + h:2 * hpg + h + 1] * o_win[:, sl])
        o_ref[0, h * HEAD_DIM:(h + 1) * HEAD_DIM, :] = o_h.astype(o_ref.dtype)


def _nsa(qk, kc, vct, tposed, ovt, batch, seq):
    tq = NSA_TQ
    ncp = seq // CMP_STRIDE
    ns = seq // SEL_BLOCK
    n_sel = min(SEL_TOPN, ns)
    qw = NSA_HPG * HEAD_DIM
    return pl.pallas_call(
        functools.partial(_nsa_kernel, n_sel=n_sel),
        out_shape=jax.ShapeDtypeStruct((batch, NSA_HEADS * HEAD_DIM, seq), BF16),
        grid=(batch, NSA_GROUPS, seq // tq),
        in_specs=[
            pl.BlockSpec((1, tq, qw), lambda b, g, i: (b, i, g)),
            pl.BlockSpec((1, ncp, LANES), lambda b, g, i: (b, 0, 0)),
            pl.BlockSpec((1, LANES, ncp), lambda b, g, i: (b, 0, 0)),
            pl.BlockSpec((1, seq, LANES), lambda b, g, i: (b, 0, 4)),
            pl.BlockSpec((1, seq, LANES), lambda b, g, i: (b, 0, 5)),
            pl.BlockSpec((1, LANES, seq), lambda b, g, i: (b, TS_VS, 0)),
            pl.BlockSpec((1, LANES, seq), lambda b, g, i: (b, TS_VW, 0)),
            pl.BlockSpec((1, LANES, tq), lambda b, g, i: (b, TS_GN, i)),
            pl.BlockSpec((ns, ncp), lambda b, g, i: (0, 0)),
        ],
        out_specs=pl.BlockSpec((1, qw, tq), lambda b, g, i: (b, g, i)),
        scratch_shapes=[pltpu.VMEM((ns, tq), F32)],
        compiler_params=_cparams(("parallel", "parallel", "arbitrary")),
        name="nsa",
    )(qk, kc, vct, qk, qk, tposed, tposed, tposed, ovt)


SB_TQ = 256


SB_SLABS = 2


def _sb_kernel(qt_ref, k_ref, vt_ref, o_ref):
    i = pl.program_id(2)
    tq = SB_TQ
    kb = tq
    nh = 2 * SB_SLABS
    n = nh * tq
    scale = 1.0 / math.sqrt(HEAD_DIM)

    rowgrp = lax.broadcasted_iota(jnp.int32, (LANES, tq), 0) // HEAD_DIM
    qtm = []
    for p in range(SB_SLABS):
        q2 = qt_ref[0, p * LANES:(p + 1) * LANES, :].astype(F32) * scale
        qtm.append(jnp.concatenate([jnp.where(rowgrp == 0, q2, 0.0), jnp.where(rowgrp == 1, q2, 0.0)],
                                   axis=1).astype(BF16))

    r = lax.broadcasted_iota(jnp.int32, (kb, kb), 0)
    c = lax.broadcasted_iota(jnp.int32, (kb, kb), 1)
    later = jnp.where(c > r, 1.0, 0.0).astype(BF16)
    krow = lax.broadcasted_iota(jnp.int32, (kb, n), 0)
    tcol = lax.broadcasted_iota(jnp.int32, (kb, n), 1) % tq

    def body(st):
        k, carry, accs, _ = st
        k0 = pl.multiple_of(k * kb, kb)
        kt = k_ref[0, pl.ds(k0, kb), :].astype(BF16)
        z = jnp.concatenate([jnp.dot(kt[:, p * LANES:(p + 1) * LANES], qtm[p], preferred_element_type=F32)
                             for p in range(SB_SLABS)], axis=1)
        sp = jnp.maximum(z, 0.0) + jnp.log(1.0 + jnp.exp(-jnp.abs(z)))
        mask = (k0 + krow) < (i * tq + tcol)
        spm = jnp.where(mask, sp, 0.0)
        hi = spm.astype(BF16)
        lo = (spm - hi.astype(F32)).astype(BF16)
        after = (jnp.dot(later, hi, preferred_element_type=F32)
                 + jnp.dot(later, lo, preferred_element_type=F32))
        a = jnp.where(mask, jnp.exp(z - sp - after - carry), 0.0).astype(BF16)
        accs = tuple(accs[h] + jnp.dot(vt_ref[0, h * HEAD_DIM:(h + 1) * HEAD_DIM, pl.ds(k0, kb)],
                                       a[:, h * tq:(h + 1) * tq], preferred_element_type=F32)
                     for h in range(nh))
        carry = carry + after[0:1] + spm[0:1]
        go = jnp.logical_and(k > 0, jnp.min(carry) <= SB_EXIT)
        return k - 1, carry, accs, go

    zacc = jnp.zeros((HEAD_DIM, tq), F32)
    st = (i, jnp.zeros((1, n), F32), (zacc,) * nh, i >= 0)
    _, _, accs, _ = lax.while_loop(lambda s: s[3], body, st)
    for h in range(nh):
        o_ref[0, h * HEAD_DIM:(h + 1) * HEAD_DIM, :] = accs[h].astype(o_ref.dtype)


def _sb(proj3, tposed, batch, seq):
    tq = SB_TQ
    w = SB_SLABS * LANES
    assert TS_QSB % SB_SLABS == 0 and TS_VSB % SB_SLABS == 0 and CB_KSB % SB_SLABS == 0
    return pl.pallas_call(
        _sb_kernel,
        out_shape=jax.ShapeDtypeStruct((batch, SB_HEADS * HEAD_DIM, seq), BF16),
        grid=(batch, SB_HEADS // (2 * SB_SLABS), seq // tq),
        in_specs=[
            pl.BlockSpec((1, w, tq), lambda b, h, i: (b, TS_QSB // SB_SLABS + h, i)),
            pl.BlockSpec((1, seq, w), lambda b, h, i: (b, 0, CB_KSB // SB_SLABS + h)),
            pl.BlockSpec((1, w, seq), lambda b, h, i: (b, TS_VSB // SB_SLABS + h, 0)),
        ],
        out_specs=pl.BlockSpec((1, w, tq), lambda b, h, i: (b, h, i)),
        compiler_params=_cparams(("parallel", "parallel", "arbitrary")),
        name="stickbreaking",
    )(tposed, proj3, tposed)


def _merge_kernel(ont_ref, ost_ref, ga_ref, gb_ref, x_ref, wa_ref, wb_ref, wo_ref, o_ref):
    tn_dims = (((0,), (0,)), ((), ()))
    a = lax.dot_general(ont_ref[0], wa_ref[...], tn_dims, preferred_element_type=F32)
    b = lax.dot_general(ost_ref[0], wb_ref[...], tn_dims, preferred_element_type=F32)
    merged = jax.nn.sigmoid(ga_ref[...]) * a + jax.nn.sigmoid(gb_ref[...]) * b
    o_ref[...] = x_ref[...] + jnp.dot(merged.astype(BF16), wo_ref[...], preferred_element_type=F32)


def _merge(ont, ost, proj, x2d, wa, wb, wo, batch, seq, tm):
    d = x2d.shape[1]
    s_tiles = seq // tm
    hw = ont.shape[1]
    return pl.pallas_call(
        _merge_kernel,
        out_shape=jax.ShapeDtypeStruct(x2d.shape, F32),
        grid=(batch, s_tiles),
        in_specs=[
            pl.BlockSpec((1, hw, tm), lambda b, i: (b, 0, i)),
            pl.BlockSpec((1, hw, tm), lambda b, i: (b, 0, i)),
            pl.BlockSpec((tm, d), lambda b, i: (b * s_tiles + i, 0)),
            pl.BlockSpec((tm, d), lambda b, i: (b * s_tiles + i, 1)),
            pl.BlockSpec((tm, d), lambda b, i: (b * s_tiles + i, 0)),
            pl.BlockSpec((hw, d), lambda b, i: (0, 0)),
            pl.BlockSpec((hw, d), lambda b, i: (0, 0)),
            pl.BlockSpec((d, d), lambda b, i: (0, 0)),
        ],
        out_specs=pl.BlockSpec((tm, d), lambda b, i: (b * s_tiles + i, 0)),
        compiler_params=_cparams(("parallel", "parallel")),
        name="merge",
    )(ont, ost, proj, proj, x2d, wa, wb, wo)


PEER_TR = 512


def _topk_ranked(s, idx, k, vals_ref=None, first_ref=None, want_rank=True):
    rows, n = s.shape
    sub = 8
    s = s.reshape(rows // sub, sub, n)
    idx = idx.reshape(rows // sub, sub, n)

    def all_sublanes(x, op):
        for sh in (4, 2, 1):
            x = op(x, pltpu.roll(x, sh, 0))
        return x

    rank = jnp.full(s.shape, float(k), F32) if want_rank else None
    for r in range(k):
        mx = all_sublanes(jnp.max(s, axis=0), jnp.maximum)
        first = all_sublanes(jnp.min(jnp.where(s == mx[None], idx, 1e9), axis=0), jnp.minimum)
        pick = idx == first[None]
        if want_rank:
            rank = jnp.where(pick, float(r), rank)
        s = jnp.where(pick, -jnp.inf, s)
        if vals_ref is not None:
            vals_ref[r:r + 1, :] = mx[0:1]
        if first_ref is not None:
            first_ref[r:r + 1, :] = first[0:1]
    return rank.reshape(rows, n) if want_rank else None


_CAND_GROUPS = ((0, 0, 16), (16, 1, 8), (24, 2, 8), (32, 3, 4), (36, 4, 4), (40, 5, 2), (42, 6, 2), (44, 7, 2))
_CAND_ROWS = 56


def _route_kernel(q_ref, sk_ref, n1_ref, a1_ref, b2_ref, e2_ref, v1_ref, v2_ref, f1_ref, cand_ref):
    k = PEER_TOPK
    half = PEER_QDIM // 2
    nt_dims = (((1,), (1,)), ((), ()))
    qh = q_ref[...].astype(BF16)
    s1 = lax.dot_general(sk_ref[0].astype(BF16), qh[:, :half], nt_dims, preferred_element_type=F32)
    s2 = lax.dot_general(sk_ref[1].astype(BF16), qh[:, half:], nt_dims, preferred_element_type=F32)
    tr = s1.shape[1]
    kidx = lax.broadcasted_iota(jnp.int32, s1.shape, 0).astype(F32)
    _topk_ranked(s1, kidx, k, v1_ref, f1_ref, want_rank=False)
    rank2 = _topk_ranked(s2, kidx, k, v2_ref)
    v1 = v1_ref[...]
    v2 = v2_ref[...]

    flat = lax.broadcasted_iota(jnp.int32, (_CAND_ROWS, tr), 0)
    row = flat
    for r0, a, nb in _CAND_GROUPS:
        cand_ref[r0:r0 + nb, :] = v1[a:a + 1] + v2[0:nb]
        flat = jnp.where((row >= r0) & (row < r0 + nb), row - r0 + a * k, flat)
    cand_ref[46:48, :] = jnp.full((2, tr), -jnp.inf, F32)
    cand_ref[48:56, :] = v1[8:16] + v2[0:1]
    flat = jnp.where(row >= 48, (row - 40) * k, flat).astype(F32)
    cand = cand_ref[...]
    crank = _topk_ranked(cand, flat, k)
    selc = jnp.where(crank < float(k), 1.0, 0.0)
    mx = v1[0:1] + v2[0:1]
    z = jnp.sum(jnp.where(crank < float(k), jnp.exp(cand - mx), 0.0), axis=0, keepdims=True)
    n_a = [jnp.sum(selc[r0:r0 + nb], axis=0, keepdims=True) for r0, _, nb in _CAND_GROUPS]
    n_a += [selc[48 + a - 8:48 + a - 7] for a in range(8, k)]

    f1 = f1_ref[...]
    n1 = jnp.zeros_like(s1)
    for a in range(k):
        n1 = jnp.where(kidx == f1[a:a + 1], n_a[a], n1)
    n1_ref[0] = n1
    a1_ref[0] = jnp.exp(s1 - v1[0:1]) / z
    b2_ref[0] = rank2.astype(b2_ref.dtype)
    e2_ref[0] = jnp.exp(s2 - v2[0:1]).astype(e2_ref.dtype)


def _route(qp, subkeys):
    t = qp.shape[0]
    tr = PEER_TR
    nk = PEER_NKEYS
    shp = jax.ShapeDtypeStruct((PEER_HEADS, nk, t), F32)
    shp16 = jax.ShapeDtypeStruct((PEER_HEADS, nk, t), BF16)
    spec = pl.BlockSpec((1, nk, tr), lambda i, h: (h, 0, i))
    return pl.pallas_call(
        _route_kernel,
        out_shape=(shp, shp, shp16, shp16),
        grid=(t // tr, PEER_HEADS),
        in_specs=[
            pl.BlockSpec((tr, PEER_QDIM), lambda i, h: (i, h)),
            pl.BlockSpec((2, nk, PEER_QDIM // 2), lambda i, h: (0, 0, 0)),
        ],
        out_specs=(spec, spec, spec, spec),
        scratch_shapes=[pltpu.VMEM((PEER_TOPK, tr), F32), pltpu.VMEM((PEER_TOPK, tr), F32),
                        pltpu.VMEM((PEER_TOPK, tr), F32), pltpu.VMEM((_CAND_ROWS, tr), F32)],
        compiler_params=_cparams(("parallel", "parallel")),
        name="peer_route",
    )(qp, subkeys)


PEER_TM = 512
PEER_NB = 16


def _peer_kernel(h_ref, g_ref, u_ref, vt_ref, n1_ref, a1_ref, b2_ref, e2_ref, o_ref, hnt_ref, acc_ref):
    kk = pl.program_id(1)
    nk = PEER_NKEYS

    @pl.when(kk == 0)
    def _():
        h = h_ref[...]
        ms = jnp.mean(h * h, axis=-1, keepdims=True)
        hnt_ref[...] = (h * lax.rsqrt(ms + EPS) * g_ref[...]).T.astype(BF16)
        acc_ref[...] = jnp.zeros_like(acc_ref)

    act = jax.nn.gelu(jnp.dot(u_ref[...], hnt_ref[...], preferred_element_type=F32)).astype(BF16)
    tm = act.shape[1]
    rep = nk // 16

    def bcast_row(ref, hd, i1):
        row16 = jnp.broadcast_to(ref[hd, pl.ds(i1, 1), :], (16, tm)).astype(BF16)
        return jnp.concatenate([row16] * rep, axis=0)

    parts = []
    for j in range(PEER_NB):
        i1 = kk * PEER_NB + j
        w = None
        for hd in range(PEER_HEADS):
            n1 = bcast_row(n1_ref, hd, i1)
            a1 = bcast_row(a1_ref, hd, i1)
            w_hd = jnp.where(b2_ref[hd] < n1, e2_ref[hd] * a1, jnp.zeros((), BF16))
            w = w_hd if w is None else w + w_hd
        parts.append(w * act[j * nk:(j + 1) * nk])
    mt = jnp.concatenate(parts, axis=0)
    acc_ref[...] += jnp.dot(vt_ref[...], mt, preferred_element_type=F32)

    @pl.when(kk == pl.num_programs(1) - 1)
    def _():
        o_ref[...] = h_ref[...] + acc_ref[...].T


def _peer(h2d, gamma, u_bf16, vt_bf16, tables):
    t, d = h2d.shape
    tm = min(PEER_TM, t)
    nk = PEER_NKEYS
    ke = PEER_NB * nk
    tspec = pl.BlockSpec((PEER_HEADS, nk, tm), lambda i, k: (0, 0, i))
    return pl.pallas_call(
        _peer_kernel,
        out_shape=jax.ShapeDtypeStruct((t, d), F32),
        grid=(t // tm, nk // PEER_NB),
        in_specs=[
            pl.BlockSpec((tm, d), lambda i, k: (i, 0)),
            pl.BlockSpec((1, d), lambda i, k: (0, 0)),
            pl.BlockSpec((ke, d), lambda i, k: (k, 0)),
            pl.BlockSpec((d, ke), lambda i, k: (0, k)),
            tspec, tspec, tspec, tspec,
        ],
        out_specs=pl.BlockSpec((tm, d), lambda i, k: (i, 0)),
        scratch_shapes=[pltpu.VMEM((d, tm), BF16), pltpu.VMEM((d, tm), F32)],
        compiler_params=_cparams(("parallel", "arbitrary")),
        name="peer_experts",
    )(h2d, gamma.reshape(1, d), u_bf16, vt_bf16, *tables)


def _reorder_w_in(w_in):
    d = w_in.shape[0]
    nsa_w = NSA_HEADS * HEAD_DIM
    kv_w = NSA_GROUPS * HEAD_DIM
    sb_w = SB_HEADS * HEAD_DIM
    c0 = nsa_w
    c1 = c0 + 6 * kv_w
    c2 = c1 + 3 * NSA_HEADS
    c3 = c2 + 3 * sb_w
    gn = w_in[:, c1:c2].reshape(d, 3, NSA_GROUPS, NSA_HPG).transpose(0, 2, 1, 3)
    gn = jnp.pad(gn.reshape(d, NSA_GROUPS, 3 * NSA_HPG), ((0, 0), (0, 0), (0, 16 - 3 * NSA_HPG)))
    gn = jnp.pad(gn.reshape(d, NSA_GROUPS * 16), ((0, 0), (0, LANES - NSA_GROUPS * 16)))
    return jnp.concatenate([w_in[:, c3:], w_in[:, :c0], w_in[:, c0:c1], w_in[:, c2:c3], gn], axis=1).astype(BF16)


def _overlap_t(seq):
    ncp = seq // CMP_STRIDE
    nc = (seq - CMP_LEN) // CMP_STRIDE + 1
    ns = seq // SEL_BLOCK
    c_start = jnp.arange(ncp) * CMP_STRIDE
    j_start = jnp.arange(ns) * SEL_BLOCK
    ov = jnp.clip(jnp.minimum(c_start[None, :] + CMP_LEN, j_start[:, None] + SEL_BLOCK)
                  - jnp.maximum(c_start[None, :], j_start[:, None]), 0).astype(F32) / CMP_LEN
    return jnp.where(jnp.arange(ncp)[None, :] < nc, ov, 0.0).astype(BF16)


def _token_mixers(x, norm1_g, w_in, q_norm_g, k_norm_g, cmp_pos_k, cmp_pos_v, cmp_k_w1, cmp_k_w2,
                  cmp_v_w1, cmp_v_w2, w_branch_nsa, w_branch_sb, w_out):
    batch, seq, d = x.shape
    t = batch * seq
    x2d = x.reshape(t, d)
    scale = 1.0 / math.sqrt(HEAD_DIM)

    proj = _norm_matmul(x2d, norm1_g, _reorder_w_in(w_in), tm=1024, tn=13 * LANES, name="in_proj")

    cos, sin = _rope_tables(jnp.arange(seq))
    qg = jnp.tile(q_norm_g.astype(F32), 2) * scale
    gammas = jnp.stack([qg] * 4 + [jnp.tile(k_norm_g[1].astype(F32), 2), jnp.tile(k_norm_g[2].astype(F32), 2)])
    qk = _prep_norm_rope(proj, cos, sin, gammas.reshape(6, 1, LANES), seq,
                         ts=min(1024, seq)).reshape(batch, seq, 6 * LANES)
    tposed = _prep_transpose(proj, batch, seq, ts=min(2048, seq))

    ncp = seq // CMP_STRIDE
    cosc, sinc = _rope_tables(jnp.arange(ncp) * CMP_STRIDE + CMP_LEN - 1)
    kc = _compress(proj[:, CB_KC * LANES:(CB_KC + 1) * LANES], cmp_pos_k, cmp_k_w1, cmp_k_w2, batch, seq,
                   key_args=(cosc, sinc, jnp.tile(k_norm_g[0].astype(F32), 2).reshape(1, LANES)))
    vct = _compress(proj[:, CB_VC * LANES:(CB_VC + 1) * LANES], cmp_pos_v, cmp_v_w1, cmp_v_w2, batch, seq)

    ont = _nsa(qk, kc, vct, tposed, _overlap_t(seq), batch, seq)
    ost = _sb(proj.reshape(batch, seq, N_CB * LANES), tposed, batch, seq)
    return _merge(ont, ost, proj, x2d, w_branch_nsa.astype(BF16), w_branch_sb.astype(BF16),
                  w_out.astype(BF16), batch, seq, tm=512)


def kernel(x, norm1_g, w_in, q_norm_g, k_norm_g, cmp_pos_k, cmp_pos_v, cmp_k_w1, cmp_k_w2, cmp_v_w1, cmp_v_w2,
           w_branch_nsa, w_branch_sb, w_out, norm2_g, peer_w_query, peer_subkeys, peer_u, peer_v):
    batch, seq, d = x.shape
    h = x.astype(F32)
    for l in range(norm1_g.shape[0]):
        h2d = _token_mixers(h, norm1_g[l], w_in[l], q_norm_g[l], k_norm_g[l], cmp_pos_k[l], cmp_pos_v[l],
                            cmp_k_w1[l], cmp_k_w2[l], cmp_v_w1[l], cmp_v_w2[l],
                            w_branch_nsa[l], w_branch_sb[l], w_out[l])
        qp = _norm_matmul(h2d, norm2_g[l], peer_w_query[l].astype(BF16), tm=1024, tn=1024, name="peer_query")
        tables = _route(qp, peer_subkeys[l])
        h2d = _peer(h2d, norm2_g[l], peer_u[l].astype(BF16), peer_v[l].T.astype(BF16), tables)
        h = h2d.reshape(batch, seq, d)
    return h.astype(x.dtype)
```

```python
import functools
import math

import jax
import jax.numpy as jnp
from jax import lax
from jax.experimental import pallas as pl
from jax.experimental.pallas import tpu as pltpu

F32 = jnp.float32
BF16 = jnp.bfloat16

HEAD_DIM = 64
NSA_HEADS = 8
NSA_GROUPS = 2
NSA_HPG = NSA_HEADS // NSA_GROUPS
CMP_LEN = 32
CMP_STRIDE = 16
CMP_HIDDEN = 256
SEL_BLOCK = 64
SEL_TOPN = 16
WINDOW = 512
SB_HEADS = 8
ROPE_THETA = 10000.0
PEER_HEADS = 8
PEER_NKEYS = 128
PEER_QDIM = 256
PEER_TOPK = 16
EPS = 1e-6
NEG = -1e30

LANES = 128
VMEM_LIMIT = 56 * 1024 * 1024

CB_GATE = 0
CB_Q = 16
CB_KC, CB_VC, CB_KS, CB_VS, CB_KW, CB_VW = 20, 21, 22, 23, 24, 25
CB_QSB, CB_KSB, CB_VSB = 26, 30, 34
CB_GN = 38
N_CB = 39

SB_EXIT = 104.0


def _cparams(sem, vmem=None):
    return pltpu.CompilerParams(dimension_semantics=sem, vmem_limit_bytes=vmem or VMEM_LIMIT)


def _norm_matmul_kernel(x_ref, g_ref, w_ref, o_ref):
    x = x_ref[...]
    ms = jnp.mean(x * x, axis=-1, keepdims=True)
    xn = (x * lax.rsqrt(ms + EPS) * g_ref[...]).astype(BF16)
    o_ref[...] = jnp.dot(xn, w_ref[...], preferred_element_type=F32)


def _norm_matmul(x2d, gamma, w_bf16, tm, tn, name):
    t, d = x2d.shape
    n = w_bf16.shape[1]
    return pl.pallas_call(
        _norm_matmul_kernel,
        out_shape=jax.ShapeDtypeStruct((t, n), F32),
        grid=(n // tn, t // tm),
        in_specs=[
            pl.BlockSpec((tm, d), lambda j, i: (i, 0)),
            pl.BlockSpec((1, d), lambda j, i: (0, 0)),
            pl.BlockSpec((d, tn), lambda j, i: (0, j)),
        ],
        out_specs=pl.BlockSpec((tm, tn), lambda j, i: (i, j)),
        compiler_params=_cparams(("parallel", "parallel")),
        name=name,
    )(x2d, gamma.reshape(1, d), w_bf16)


def _group_mean_sq(x):
    ss = x * x
    r = lax.broadcasted_iota(jnp.int32, (LANES, LANES), 0) // HEAD_DIM
    c = lax.broadcasted_iota(jnp.int32, (LANES, LANES), 1) // HEAD_DIM
    ones_bd = jnp.where(r == c, 1.0, 0.0).astype(BF16)
    hi = ss.astype(BF16)
    lo = (ss - hi.astype(F32)).astype(BF16)
    tot = (jnp.dot(hi, ones_bd, preferred_element_type=F32)
           + jnp.dot(lo, ones_bd, preferred_element_type=F32))
    return tot * (1.0 / HEAD_DIM)


def _norm_rope(x, cos, sin_signed, gamma):
    xn = x * lax.rsqrt(_group_mean_sq(x) + EPS) * gamma
    lane = lax.broadcasted_iota(jnp.int32, xn.shape, 1)
    first_half = (lane % HEAD_DIM) < (HEAD_DIM // 2)
    swapped = jnp.where(first_half, pltpu.roll(xn, LANES - HEAD_DIM // 2, 1), pltpu.roll(xn, HEAD_DIM // 2, 1))
    return xn * cos + swapped * sin_signed


def _rope_tables(pos):
    half = HEAD_DIM // 2
    d = jnp.arange(LANES) % HEAD_DIM
    freqs = ROPE_THETA ** (-(d % half).astype(F32) / half)
    ang = pos.astype(F32)[:, None] * freqs[None, :]
    sign = jnp.where(d < half, -1.0, 1.0).astype(F32)
    return jnp.cos(ang), jnp.sin(ang) * sign[None, :]


def _norm_rope_kernel(x_ref, cos_ref, sin_ref, g_ref, o_ref):
    o_ref[...] = _norm_rope(x_ref[...], cos_ref[...], sin_ref[...], g_ref[0]).astype(o_ref.dtype)


def _prep_norm_rope(proj, cos, sin, gammas, seq, ts):
    t = proj.shape[0]
    n_slab = gammas.shape[0]
    s_tiles = seq // ts
    return pl.pallas_call(
        _norm_rope_kernel,
        out_shape=jax.ShapeDtypeStruct((t, n_slab * LANES), BF16),
        grid=(t // ts, n_slab),
        in_specs=[
            pl.BlockSpec((ts, LANES), lambda i, j: (i, CB_Q + j + 2 * (j // 4) + (j // 5))),
            pl.BlockSpec((ts, LANES), lambda i, j: (i % s_tiles, 0)),
            pl.BlockSpec((ts, LANES), lambda i, j: (i % s_tiles, 0)),
            pl.BlockSpec((1, 1, LANES), lambda i, j: (j, 0, 0)),
        ],
        out_specs=pl.BlockSpec((ts, LANES), lambda i, j: (i, j)),
        compiler_params=_cparams(("parallel", "parallel")),
        name="prep_norm_rope",
    )(proj, cos, sin, gammas)


T_SLABS = (CB_VS, CB_VW, CB_QSB, CB_QSB + 1, CB_QSB + 2, CB_QSB + 3,
           CB_VSB, CB_VSB + 1, CB_VSB + 2, CB_VSB + 3, CB_GN)
TS_VS, TS_VW, TS_QSB, TS_VSB, TS_GN = 0, 1, 2, 6, 10


def _transpose_kernel(cm_ref, x_ref, o_ref):
    del cm_ref
    o_ref[0] = x_ref[...].T.astype(o_ref.dtype)


def _prep_transpose(proj, batch, seq, ts):
    s_tiles = seq // ts
    colmap = jnp.asarray(T_SLABS, jnp.int32)
    return pl.pallas_call(
        _transpose_kernel,
        out_shape=jax.ShapeDtypeStruct((batch, len(T_SLABS) * LANES, seq), BF16),
        grid_spec=pltpu.PrefetchScalarGridSpec(
            num_scalar_prefetch=1,
            grid=(batch, s_tiles, len(T_SLABS)),
            in_specs=[pl.BlockSpec((ts, LANES), lambda b, i, j, cm: (b * s_tiles + i, cm[j]))],
            out_specs=pl.BlockSpec((1, LANES, ts), lambda b, i, j, cm: (b, j, i)),
        ),
        compiler_params=_cparams(("parallel", "parallel", "parallel")),
        name="prep_transpose",
    )(colmap, proj)


def _compress_kernel(a_ref, pa_ref, pb_ref, w1a_ref, w1b_ref, w2_ref, *rest, is_key):
    a = a_ref[0]
    ncp = a.shape[0]
    p = jnp.dot((a + pa_ref[...]).astype(BF16), w1a_ref[...], preferred_element_type=F32)
    q = jnp.dot((a + pb_ref[...]).astype(BF16), w1b_ref[...], preferred_element_type=F32)
    hid = jax.nn.gelu(p + pltpu.roll(q, ncp - 1, 0))
    if is_key:
        cos_ref, sin_ref, g_ref, o_ref = rest
        out = jnp.dot(hid.astype(BF16), w2_ref[...], preferred_element_type=F32)
        o_ref[0] = _norm_rope(out, cos_ref[...], sin_ref[...], g_ref[...]).astype(o_ref.dtype)
    else:
        (o_ref,) = rest
        nt_dims = (((1,), (1,)), ((), ()))
        out_t = lax.dot_general(w2_ref[...], hid.astype(BF16), nt_dims, preferred_element_type=F32)
        o_ref[0] = out_t.astype(o_ref.dtype)


def _expand_cmp_weights(w1, w2):
    eye = jnp.eye(NSA_GROUPS, dtype=F32)
    w1r = w1.reshape(2, CMP_STRIDE, HEAD_DIM, CMP_HIDDEN)
    ex = w1r[:, :, None, :, None, :] * eye[None, None, :, None, :, None]
    ex = ex.reshape(2, CMP_STRIDE * LANES, NSA_GROUPS * CMP_HIDDEN).astype(BF16)
    w2x = (w2[None, :, None, :] * eye[:, None, :, None]).reshape(NSA_GROUPS * CMP_HIDDEN, LANES).astype(BF16)
    return ex[0], ex[1], w2x


def _compress(tok_slab, pos_emb, w1, w2, batch, seq, key_args=None):
    ncp = seq // CMP_STRIDE
    a = tok_slab.reshape(batch, ncp, CMP_STRIDE * LANES)
    pos2 = jnp.tile(pos_emb.astype(F32)[:, None, :], (1, NSA_GROUPS, 1)).reshape(2, 1, CMP_STRIDE * LANES)
    w1a, w1b, w2x = _expand_cmp_weights(w1, w2)
    kdim = CMP_STRIDE * LANES
    hdim = NSA_GROUPS * CMP_HIDDEN
    if key_args is None:
        w2x = w2x.T
    in_specs = [
        pl.BlockSpec((1, ncp, kdim), lambda b: (b, 0, 0)),
        pl.BlockSpec((1, kdim), lambda b: (0, 0)),
        pl.BlockSpec((1, kdim), lambda b: (0, 0)),
        pl.BlockSpec((kdim, hdim), lambda b: (0, 0)),
        pl.BlockSpec((kdim, hdim), lambda b: (0, 0)),
        pl.BlockSpec(w2x.shape, lambda b: (0, 0)),
    ]
    args = [a, pos2[0], pos2[1], w1a, w1b, w2x]
    if key_args is not None:
        cos, sin, gamma = key_args
        in_specs += [pl.BlockSpec((ncp, LANES), lambda b: (0, 0)),
                     pl.BlockSpec((ncp, LANES), lambda b: (0, 0)),
                     pl.BlockSpec((1, LANES), lambda b: (0, 0))]
        args += [cos, sin, gamma]
        out_shape = jax.ShapeDtypeStruct((batch, ncp, LANES), BF16)
        out_spec = pl.BlockSpec((1, ncp, LANES), lambda b: (b, 0, 0))
    else:
        out_shape = jax.ShapeDtypeStruct((batch, LANES, ncp), BF16)
        out_spec = pl.BlockSpec((1, LANES, ncp), lambda b: (b, 0, 0))
    return pl.pallas_call(
        functools.partial(_compress_kernel, is_key=key_args is not None),
        out_shape=out_shape,
        grid=(batch,),
        in_specs=in_specs,
        out_specs=out_spec,
        compiler_params=_cparams(("parallel",)),
        name="compress_k" if key_args is not None else "compress_v",
    )(*args)


NSA_TQ = 256
NSA_KB = 1024


def _flash_step(k_tile, vt_tile, qtm, bias, carry):
    m, l, acc = carry
    s = jnp.dot(k_tile, qtm, preferred_element_type=F32) + bias
    m_new = jnp.maximum(m, jnp.max(s, axis=0, keepdims=True))
    alpha = jnp.exp(m - m_new)
    p = jnp.exp(s - m_new).astype(BF16)
    vt_ones = jnp.concatenate([vt_tile, jnp.ones((16, vt_tile.shape[1]), BF16)], axis=0)
    pv = jnp.dot(vt_ones, p, preferred_element_type=F32)
    l = alpha * l + pv[HEAD_DIM:HEAD_DIM + 1]
    acc = alpha * acc + pv[:HEAD_DIM]
    return m_new, l, acc


def _nsa_kernel(q_ref, kc_ref, vct_ref, ks_ref, kw_ref, vst_ref, vwt_ref, gate_ref, ovt_ref,
                o_ref, selb_ref, *, n_sel):
    g = pl.program_id(1)
    i = pl.program_id(2)
    tq, kb, hpg = NSA_TQ, NSA_KB, NSA_HPG
    n = hpg * tq
    t0 = i * tq
    grow = pl.multiple_of(g * HEAD_DIM, HEAD_DIM)

    qt = q_ref[0].astype(F32).T
    rowgrp = lax.broadcasted_iota(jnp.int32, (LANES, tq), 0) // HEAD_DIM
    parts = []
    for h in range(hpg):
        blk = qt[h * HEAD_DIM:(h + 1) * HEAD_DIM]
        parts.append(jnp.where(rowgrp == g, jnp.concatenate([blk, blk], axis=0), 0.0))
    qtm = jnp.concatenate(parts, axis=1).astype(BF16)

    tok1 = t0 + lax.broadcasted_iota(jnp.int32, (1, tq), 1)
    tokn = jnp.concatenate([tok1] * hpg, axis=1)

    ncp = kc_ref.shape[1]
    sc = jnp.dot(kc_ref[0], qtm, preferred_element_type=F32)
    cend = lax.broadcasted_iota(jnp.int32, (ncp, n), 0) * CMP_STRIDE + (CMP_LEN - 1)
    sc = jnp.where(cend <= tokn, sc, NEG)
    mc = jnp.max(sc, axis=0, keepdims=True)
    ec = jnp.exp(sc - mc)
    has_c = jnp.where(tokn >= CMP_LEN - 1, 1.0, 0.0)
    pc = ec * (has_c / jnp.sum(ec, axis=0, keepdims=True))
    o_cmp = jnp.dot(vct_ref[0, pl.ds(grow, HEAD_DIM), :], pc.astype(BF16), preferred_element_type=F32)

    psum = pc[:, 0:tq]
    for h in range(1, hpg):
        psum = psum + pc[:, h * tq:(h + 1) * tq]
    p_hi = psum.astype(BF16)
    p_lo = (psum - p_hi.astype(F32)).astype(BF16)
    imp = (jnp.dot(ovt_ref[...], p_hi, preferred_element_type=F32)
           + jnp.dot(ovt_ref[...], p_lo, preferred_element_type=F32))
    ns = imp.shape[0]
    jidx = lax.broadcasted_iota(jnp.int32, (ns, tq), 0)
    jf = jidx.astype(F32)
    tokb = jnp.broadcast_to(tok1, (ns, tq))
    cur = tokb // SEL_BLOCK
    forced = (jidx == 0) | (jidx == cur) | (jidx == cur - 1)
    valid = jidx * SEL_BLOCK <= tokb
    score = jnp.where(forced, 1e4, jnp.where(valid, imp, -1e4))
    selb = jnp.full((ns, tq), NEG, F32)
    for _ in range(n_sel):
        mx = jnp.max(score, axis=0, keepdims=True)
        first = jnp.min(jnp.where(score == mx, jf, float(ns)), axis=0, keepdims=True)
        pick = jf == first
        selb = jnp.where(pick, 0.0, selb)
        score = jnp.where(pick, -jnp.inf, score)
    selb_ref[...] = selb

    wlen = WINDOW + tq
    w0 = pl.multiple_of(jnp.maximum(t0 - WINDOW, 0), tq)
    wpos = w0 + lax.broadcasted_iota(jnp.int32, (wlen, tq), 0)
    wbias1 = jnp.where((wpos <= tok1) & (wpos > tok1 - WINDOW), 0.0, NEG)
    sw = (jnp.dot(kw_ref[0, pl.ds(w0, wlen), :], qtm, preferred_element_type=F32)
          + jnp.concatenate([wbias1] * hpg, axis=1))
    pw = jnp.exp(sw - jnp.max(sw, axis=0, keepdims=True)).astype(BF16)
    vw_ones = jnp.concatenate([vwt_ref[0, pl.ds(grow, HEAD_DIM), pl.ds(w0, wlen)],
                               jnp.ones((16, wlen), BF16)], axis=0)
    pvw = jnp.dot(vw_ones, pw, preferred_element_type=F32)
    o_win = pvw[:HEAD_DIM] / pvw[HEAD_DIM:HEAD_DIM + 1]

    krow = lax.broadcasted_iota(jnp.int32, (kb, tq), 0)
    init = (jnp.full((1, n), NEG, F32), jnp.zeros((1, n), F32), jnp.zeros((HEAD_DIM, n), F32))
    nblk = kb // SEL_BLOCK
    def sel_body(k, carry):
        k0 = pl.multiple_of(k * kb, kb)
        rows = [jnp.broadcast_to(selb_ref[pl.ds(k * nblk + r, 1), :], (SEL_BLOCK, tq)) for r in range(nblk)]
        bias1 = jnp.where(k0 + krow <= tok1, jnp.concatenate(rows, axis=0), NEG)
        bias = jnp.concatenate([bias1] * hpg, axis=1)
        return _flash_step(ks_ref[0, pl.ds(k0, kb), :], vst_ref[0, pl.ds(grow, HEAD_DIM), pl.ds(k0, kb)],
                           qtm, bias, carry)
    _, l_s, acc_s = lax.fori_loop(0, t0 // kb + 1, sel_body, init)

    gt = jax.nn.sigmoid(gate_ref[0, pl.ds(pl.multiple_of(g * 16, 16), 16), :].astype(F32))
    o_sel = acc_s / l_s
    for h in range(hpg):
        sl = slice(h * tq, (h + 1) * tq)
        o_h = (gt[h:h + 1] * o_cmp[:, sl] + gt[hpg + h:hpg + h + 1] * o_sel[:, sl]
               + gt[2 * hpg + h:2 * hpg + h + 1] * o_win[:, sl])
        o_ref[0, h * HEAD_DIM:(h + 1) * HEAD_DIM, :] = o_h.astype(o_ref.dtype)


def _nsa(qk, kc, vct, tposed, ovt, batch, seq):
    tq = NSA_TQ
    ncp = seq // CMP_STRIDE
    ns = seq // SEL_BLOCK
    n_sel = min(SEL_TOPN, ns)
    qw = NSA_HPG * HEAD_DIM
    return pl.pallas_call(
        functools.partial(_nsa_kernel, n_sel=n_sel),
        out_shape=jax.ShapeDtypeStruct((batch, NSA_HEADS * HEAD_DIM, seq), BF16),
        grid=(batch, NSA_GROUPS, seq // tq),
        in_specs=[
            pl.BlockSpec((1, tq, qw), lambda b, g, i: (b, i, g)),
            pl.BlockSpec((1, ncp, LANES), lambda b, g, i: (b, 0, 0)),
            pl.BlockSpec((1, LANES, ncp), lambda b, g, i: (b, 0, 0)),
            pl.BlockSpec((1, seq, LANES), lambda b, g, i: (b, 0, 4)),
            pl.BlockSpec((1, seq, LANES), lambda b, g, i: (b, 0, 5)),
            pl.BlockSpec((1, LANES, seq), lambda b, g, i: (b, TS_VS, 0)),
            pl.BlockSpec((1, LANES, seq), lambda b, g, i: (b, TS_VW, 0)),
            pl.BlockSpec((1, LANES, tq), lambda b, g, i: (b, TS_GN, i)),
            pl.BlockSpec((ns, ncp), lambda b, g, i: (0, 0)),
        ],
        out_specs=pl.BlockSpec((1, qw, tq), lambda b, g, i: (b, g, i)),
        scratch_shapes=[pltpu.VMEM((ns, tq), F32)],
        compiler_params=_cparams(("parallel", "parallel", "arbitrary")),
        name="nsa",
    )(qk, kc, vct, qk, qk, tposed, tposed, tposed, ovt)


SB_TQ = 256


SB_SLABS = 2


def _sb_kernel(qt_ref, k_ref, vt_ref, o_ref):
    i = pl.program_id(2)
    tq = SB_TQ
    kb = tq
    nh = 2 * SB_SLABS
    n = nh * tq
    scale = 1.0 / math.sqrt(HEAD_DIM)

    rowgrp = lax.broadcasted_iota(jnp.int32, (LANES, tq), 0) // HEAD_DIM
    qtm = []
    for p in range(SB_SLABS):
        q2 = qt_ref[0, p * LANES:(p + 1) * LANES, :].astype(F32) * scale
        qtm.append(jnp.concatenate([jnp.where(rowgrp == 0, q2, 0.0), jnp.where(rowgrp == 1, q2, 0.0)],
                                   axis=1).astype(BF16))

    r = lax.broadcasted_iota(jnp.int32, (kb, kb), 0)
    c = lax.broadcasted_iota(jnp.int32, (kb, kb), 1)
    later = jnp.where(c > r, 1.0, 0.0).astype(BF16)
    krow = lax.broadcasted_iota(jnp.int32, (kb, n), 0)
    tcol = lax.broadcasted_iota(jnp.int32, (kb, n), 1) % tq

    def body(st):
        k, carry, accs, _ = st
        k0 = pl.multiple_of(k * kb, kb)
        kt = k_ref[0, pl.ds(k0, kb), :].astype(BF16)
        z = jnp.concatenate([jnp.dot(kt[:, p * LANES:(p + 1) * LANES], qtm[p], preferred_element_type=F32)
                             for p in range(SB_SLABS)], axis=1)
        sp = jnp.maximum(z, 0.0) + jnp.log(1.0 + jnp.exp(-jnp.abs(z)))
        mask = (k0 + krow) < (i * tq + tcol)
        spm = jnp.where(mask, sp, 0.0)
        hi = spm.astype(BF16)
        lo = (spm - hi.astype(F32)).astype(BF16)
        after = (jnp.dot(later, hi, preferred_element_type=F32)
                 + jnp.dot(later, lo, preferred_element_type=F32))
        a = jnp.where(mask, jnp.exp(z - sp - after - carry), 0.0).astype(BF16)
        accs = tuple(accs[h] + jnp.dot(vt_ref[0, h * HEAD_DIM:(h + 1) * HEAD_DIM, pl.ds(k0, kb)],
                                       a[:, h * tq:(h + 1) * tq], preferred_element_type=F32)
                     for h in range(nh))
        carry = carry + after[0:1] + spm[0:1]
        go = jnp.logical_and(k > 0, jnp.min(carry) <= SB_EXIT)
        return k - 1, carry, accs, go

    zacc = jnp.zeros((HEAD_DIM, tq), F32)
    st = (i, jnp.zeros((1, n), F32), (zacc,) * nh, i >= 0)
    _, _, accs, _ = lax.while_loop(lambda s: s[3], body, st)
    for h in range(nh):
        o_ref[0, h * HEAD_DIM:(h + 1) * HEAD_DIM, :] = accs[h].astype(o_ref.dtype)


def _sb(proj3, tposed, batch, seq):
    tq = SB_TQ
    w = SB_SLABS * LANES
    assert TS_QSB % SB_SLABS == 0 and TS_VSB % SB_SLABS == 0 and CB_KSB % SB_SLABS == 0
    return pl.pallas_call(
        _sb_kernel,
        out_shape=jax.ShapeDtypeStruct((batch, SB_HEADS * HEAD_DIM, seq), BF16),
        grid=(batch, SB_HEADS // (2 * SB_SLABS), seq // tq),
        in_specs=[
            pl.BlockSpec((1, w, tq), lambda b, h, i: (b, TS_QSB // SB_SLABS + h, i)),
            pl.BlockSpec((1, seq, w), lambda b, h, i: (b, 0, CB_KSB // SB_SLABS + h)),
            pl.BlockSpec((1, w, seq), lambda b, h, i: (b, TS_VSB // SB_SLABS + h, 0)),
        ],
        out_specs=pl.BlockSpec((1, w, tq), lambda b, h, i: (b, h, i)),
        compiler_params=_cparams(("parallel", "parallel", "arbitrary")),
        name="stickbreaking",
    )(tposed, proj3, tposed)


def _merge_kernel(ont_ref, ost_ref, ga_ref, gb_ref, x_ref, wa_ref, wb_ref, wo_ref, o_ref):
    tn_dims = (((0,), (0,)), ((), ()))
    a = lax.dot_general(ont_ref[0], wa_ref[...], tn_dims, preferred_element_type=F32)
    b = lax.dot_general(ost_ref[0], wb_ref[...], tn_dims, preferred_element_type=F32)
    merged = jax.nn.sigmoid(ga_ref[...]) * a + jax.nn.sigmoid(gb_ref[...]) * b
    o_ref[...] = x_ref[...] + jnp.dot(merged.astype(BF16), wo_ref[...], preferred_element_type=F32)


def _merge(ont, ost, proj, x2d, wa, wb, wo, batch, seq, tm):
    d = x2d.shape[1]
    s_tiles = seq // tm
    hw = ont.shape[1]
    return pl.pallas_call(
        _merge_kernel,
        out_shape=jax.ShapeDtypeStruct(x2d.shape, F32),
        grid=(batch, s_tiles),
        in_specs=[
            pl.BlockSpec((1, hw, tm), lambda b, i: (b, 0, i)),
            pl.BlockSpec((1, hw, tm), lambda b, i: (b, 0, i)),
            pl.BlockSpec((tm, d), lambda b, i: (b * s_tiles + i, 0)),
            pl.BlockSpec((tm, d), lambda b, i: (b * s_tiles + i, 1)),
            pl.BlockSpec((tm, d), lambda b, i: (b * s_tiles + i, 0)),
            pl.BlockSpec((hw, d), lambda b, i: (0, 0)),
            pl.BlockSpec((hw, d), lambda b, i: (0, 0)),
            pl.BlockSpec((d, d), lambda b, i: (0, 0)),
        ],
        out_specs=pl.BlockSpec((tm, d), lambda b, i: (b * s_tiles + i, 0)),
        compiler_params=_cparams(("parallel", "parallel")),
        name="merge",
    )(ont, ost, proj, proj, x2d, wa, wb, wo)


PEER_TR = 1024


def _topk_ranked(s, idx, k, vals_ref=None, first_ref=None, want_rank=True):
    rows, n = s.shape
    sub = 8
    s = s.reshape(rows // sub, sub, n)
    idx = idx.reshape(rows // sub, sub, n)

    def all_sublanes(x, op):
        for sh in (4, 2, 1):
            x = op(x, pltpu.roll(x, sh, 0))
        return x

    rank = jnp.full(s.shape, float(k), F32) if want_rank else None
    for r in range(k):
        mx = all_sublanes(jnp.max(s, axis=0), jnp.maximum)
        first = all_sublanes(jnp.min(jnp.where(s == mx[None], idx, 1e9), axis=0), jnp.minimum)
        pick = idx == first[None]
        if want_rank:
            rank = jnp.where(pick, float(r), rank)
        s = jnp.where(pick, -jnp.inf, s)
        if vals_ref is not None:
            vals_ref[r:r + 1, :] = mx[0:1]
        if first_ref is not None:
            first_ref[r:r + 1, :] = first[0:1]
    return rank.reshape(rows, n) if want_rank else None


_CAND_GROUPS = ((0, 0, 16), (16, 1, 8), (24, 2, 8), (32, 3, 4), (36, 4, 4), (40, 5, 2), (42, 6, 2), (44, 7, 2))
_CAND_ROWS = 56


def _route_kernel(q_ref, sk_ref, n1_ref, a1_ref, b2_ref, e2_ref, v1_ref, v2_ref, f1_ref, cand_ref):
    k = PEER_TOPK
    half = PEER_QDIM // 2
    nt_dims = (((1,), (1,)), ((), ()))
    qh = q_ref[...].astype(BF16)
    s1 = lax.dot_general(sk_ref[0].astype(BF16), qh[:, :half], nt_dims, preferred_element_type=F32)
    s2 = lax.dot_general(sk_ref[1].astype(BF16), qh[:, half:], nt_dims, preferred_element_type=F32)
    tr = s1.shape[1]
    kidx = lax.broadcasted_iota(jnp.int32, s1.shape, 0).astype(F32)
    _topk_ranked(s1, kidx, k, v1_ref, f1_ref, want_rank=False)
    rank2 = _topk_ranked(s2, kidx, k, v2_ref)
    v1 = v1_ref[...]
    v2 = v2_ref[...]

    flat = lax.broadcasted_iota(jnp.int32, (_CAND_ROWS, tr), 0)
    row = flat
    for r0, a, nb in _CAND_GROUPS:
        cand_ref[r0:r0 + nb, :] = v1[a:a + 1] + v2[0:nb]
        flat = jnp.where((row >= r0) & (row < r0 + nb), row - r0 + a * k, flat)
    cand_ref[46:48, :] = jnp.full((2, tr), -jnp.inf, F32)
    cand_ref[48:56, :] = v1[8:16] + v2[0:1]
    flat = jnp.where(row >= 48, (row - 40) * k, flat).astype(F32)
    cand = cand_ref[...]
    crank = _topk_ranked(cand, flat, k)
    selc = jnp.where(crank < float(k), 1.0, 0.0)
    mx = v1[0:1] + v2[0:1]
    z = jnp.sum(jnp.where(crank < float(k), jnp.exp(cand - mx), 0.0), axis=0, keepdims=True)
    n_a = [jnp.sum(selc[r0:r0 + nb], axis=0, keepdims=True) for r0, _, nb in _CAND_GROUPS]
    n_a += [selc[48 + a - 8:48 + a - 7] for a in range(8, k)]

    f1 = f1_ref[...]
    n1 = jnp.zeros_like(s1)
    for a in range(k):
        n1 = jnp.where(kidx == f1[a:a + 1], n_a[a], n1)
    n1_ref[0] = n1
    a1_ref[0] = jnp.exp(s1 - v1[0:1]) / z
    b2_ref[0] = rank2.astype(b2_ref.dtype)
    e2_ref[0] = jnp.exp(s2 - v2[0:1]).astype(e2_ref.dtype)


def _route(qp, subkeys):
    t = qp.shape[0]
    tr = PEER_TR
    nk = PEER_NKEYS
    shp = jax.ShapeDtypeStruct((PEER_HEADS, nk, t), F32)
    shp16 = jax.ShapeDtypeStruct((PEER_HEADS, nk, t), BF16)
    spec = pl.BlockSpec((1, nk, tr), lambda i, h: (h, 0, i))
    return pl.pallas_call(
        _route_kernel,
        out_shape=(shp, shp, shp16, shp16),
        grid=(t // tr, PEER_HEADS),
        in_specs=[
            pl.BlockSpec((tr, PEER_QDIM), lambda i, h: (i, h)),
            pl.BlockSpec((2, nk, PEER_QDIM // 2), lambda i, h: (0, 0, 0)),
        ],
        out_specs=(spec, spec, spec, spec),
        scratch_shapes=[pltpu.VMEM((PEER_TOPK, tr), F32), pltpu.VMEM((PEER_TOPK, tr), F32),
                        pltpu.VMEM((PEER_TOPK, tr), F32), pltpu.VMEM((_CAND_ROWS, tr), F32)],
        compiler_params=_cparams(("parallel", "parallel")),
        name="peer_route",
    )(qp, subkeys)


PEER_TM = 512
PEER_NB = 16


def _peer_kernel(h_ref, g_ref, u_ref, vt_ref, n1_ref, a1_ref, b2_ref, e2_ref, o_ref, hnt_ref, acc_ref):
    kk = pl.program_id(1)
    nk = PEER_NKEYS

    @pl.when(kk == 0)
    def _():
        h = h_ref[...]
        ms = jnp.mean(h * h, axis=-1, keepdims=True)
        hnt_ref[...] = (h * lax.rsqrt(ms + EPS) * g_ref[...]).T.astype(BF16)
        acc_ref[...] = jnp.zeros_like(acc_ref)

    act = jax.nn.gelu(jnp.dot(u_ref[...], hnt_ref[...], preferred_element_type=F32)).astype(BF16)
    tm = act.shape[1]
    rep = nk // 16

    def bcast_row(ref, hd, i1):
        row16 = jnp.broadcast_to(ref[hd, pl.ds(i1, 1), :], (16, tm)).astype(BF16)
        return jnp.concatenate([row16] * rep, axis=0)

    parts = []
    for j in range(PEER_NB):
        i1 = kk * PEER_NB + j
        w = None
        for hd in range(PEER_HEADS):
            n1 = bcast_row(n1_ref, hd, i1)
            a1 = bcast_row(a1_ref, hd, i1)
            w_hd = jnp.where(b2_ref[hd] < n1, e2_ref[hd] * a1, jnp.zeros((), BF16))
            w = w_hd if w is None else w + w_hd
        parts.append(w * act[j * nk:(j + 1) * nk])
    mt = jnp.concatenate(parts, axis=0)
    acc_ref[...] += jnp.dot(vt_ref[...], mt, preferred_element_type=F32)

    @pl.when(kk == pl.num_programs(1) - 1)
    def _():
        o_ref[...] = h_ref[...] + acc_ref[...].T


def _peer(h2d, gamma, u_bf16, vt_bf16, tables):
    t, d = h2d.shape
    tm = min(PEER_TM, t)
    nk = PEER_NKEYS
    ke = PEER_NB * nk
    tspec = pl.BlockSpec((PEER_HEADS, nk, tm), lambda i, k: (0, 0, i))
    return pl.pallas_call(
        _peer_kernel,
        out_shape=jax.ShapeDtypeStruct((t, d), F32),
        grid=(t // tm, nk // PEER_NB),
        in_specs=[
            pl.BlockSpec((tm, d), lambda i, k: (i, 0)),
            pl.BlockSpec((1, d), lambda i, k: (0, 0)),
            pl.BlockSpec((ke, d), lambda i, k: (k, 0)),
            pl.BlockSpec((d, ke), lambda i, k: (0, k)),
            tspec, tspec, tspec, tspec,
        ],
        out_specs=pl.BlockSpec((tm, d), lambda i, k: (i, 0)),
        scratch_shapes=[pltpu.VMEM((d, tm), BF16), pltpu.VMEM((d, tm), F32)],
        compiler_params=_cparams(("parallel", "arbitrary")),
        name="peer_experts",
    )(h2d, gamma.reshape(1, d), u_bf16, vt_bf16, *tables)


def _reorder_w_in(w_in):
    d = w_in.shape[0]
    nsa_w = NSA_HEADS * HEAD_DIM
    kv_w = NSA_GROUPS * HEAD_DIM
    sb_w = SB_HEADS * HEAD_DIM
    c0 = nsa_w
    c1 = c0 + 6 * kv_w
    c2 = c1 + 3 * NSA_HEADS
    c3 = c2 + 3 * sb_w
    gn = w_in[:, c1:c2].reshape(d, 3, NSA_GROUPS, NSA_HPG).transpose(0, 2, 1, 3)
    gn = jnp.pad(gn.reshape(d, NSA_GROUPS, 3 * NSA_HPG), ((0, 0), (0, 0), (0, 16 - 3 * NSA_HPG)))
    gn = jnp.pad(gn.reshape(d, NSA_GROUPS * 16), ((0, 0), (0, LANES - NSA_GROUPS * 16)))
    return jnp.concatenate([w_in[:, c3:], w_in[:, :c0], w_in[:, c0:c1], w_in[:, c2:c3], gn], axis=1).astype(BF16)


def _overlap_t(seq):
    ncp = seq // CMP_STRIDE
    nc = (seq - CMP_LEN) // CMP_STRIDE + 1
    ns = seq // SEL_BLOCK
    c_start = jnp.arange(ncp) * CMP_STRIDE
    j_start = jnp.arange(ns) * SEL_BLOCK
    ov = jnp.clip(jnp.minimum(c_start[None, :] + CMP_LEN, j_start[:, None] + SEL_BLOCK)
                  - jnp.maximum(c_start[None, :], j_start[:, None]), 0).astype(F32) / CMP_LEN
    return jnp.where(jnp.arange(ncp)[None, :] < nc, ov, 0.0).astype(BF16)


def _token_mixers(x, norm1_g, w_in, q_norm_g, k_norm_g, cmp_pos_k, cmp_pos_v, cmp_k_w1, cmp_k_w2,
                  cmp_v_w1, cmp_v_w2, w_branch_nsa, w_branch_sb, w_out):
    batch, seq, d = x.shape
    t = batch * seq
    x2d = x.reshape(t, d)
    scale = 1.0 / math.sqrt(HEAD_DIM)

    proj = _norm_matmul(x2d, norm1_g, _reorder_w_in(w_in), tm=1024, tn=13 * LANES, name="in_proj")

    cos, sin = _rope_tables(jnp.arange(seq))
    qg = jnp.tile(q_norm_g.astype(F32), 2) * scale
    gammas = jnp.stack([qg] * 4 + [jnp.tile(k_norm_g[1].astype(F32), 2), jnp.tile(k_norm_g[2].astype(F32), 2)])
    qk = _prep_norm_rope(proj, cos, sin, gammas.reshape(6, 1, LANES), seq,
                         ts=min(2048, seq)).reshape(batch, seq, 6 * LANES)
    tposed = _prep_transpose(proj, batch, seq, ts=min(4096, seq))

    ncp = seq // CMP_STRIDE
    cosc, sinc = _rope_tables(jnp.arange(ncp) * CMP_STRIDE + CMP_LEN - 1)
    kc = _compress(proj[:, CB_KC * LANES:(CB_KC + 1) * LANES], cmp_pos_k, cmp_k_w1, cmp_k_w2, batch, seq,
                   key_args=(cosc, sinc, jnp.tile(k_norm_g[0].astype(F32), 2).reshape(1, LANES)))
    vct = _compress(proj[:, CB_VC * LANES:(CB_VC + 1) * LANES], cmp_pos_v, cmp_v_w1, cmp_v_w2, batch, seq)

    ont = _nsa(qk, kc, vct, tposed, _overlap_t(seq), batch, seq)
    ost = _sb(proj.reshape(batch, seq, N_CB * LANES), tposed, batch, seq)
    return _merge(ont, ost, proj, x2d, w_branch_nsa.astype(BF16), w_branch_sb.astype(BF16),
                  w_out.astype(BF16), batch, seq, tm=512)


def kernel(x, norm1_g, w_in, q_norm_g, k_norm_g, cmp_pos_k, cmp_pos_v, cmp_k_w1, cmp_k_w2, cmp_v_w1, cmp_v_w2,
           w_branch_nsa, w_branch_sb, w_out, norm2_g, peer_w_query, peer_subkeys, peer_u, peer_v):
    batch, seq, d = x.shape
    h = x.astype(F32)
    for l in range(norm1_g.shape[0]):
        h2d = _token_mixers(h, norm1_g[l], w_in[l], q_norm_g[l], k_norm_g[l], cmp_pos_k[l], cmp_pos_v[l],
                            cmp_k_w1[l], cmp_k_w2[l], cmp_v_w1[l], cmp_v_w2[l],
                            w_branch_nsa[l], w_branch_sb[l], w_out[l])
        qp = _norm_matmul(h2d, norm2_g[l], peer_w_query[l].astype(BF16), tm=1024, tn=1024, name="peer_query")
        tables = _route(qp, peer_subkeys[l])
        h2d = _peer(h2d, norm2_g[l], peer_u[l].astype(BF16), peer_v[l].T.astype(BF16), tables)
        h = h2d.reshape(batch, seq, d)
    return h.astype(x.dtype)
```

```python
import functools
import math

import jax
import jax.numpy as jnp
from jax import lax
from jax.experimental import pallas as pl
from jax.experimental.pallas import tpu as pltpu

F32 = jnp.float32
BF16 = jnp.bfloat16

HEAD_DIM = 64
NSA_HEADS = 8
NSA_GROUPS = 2
NSA_HPG = NSA_HEADS // NSA_GROUPS
CMP_LEN = 32
CMP_STRIDE = 16
CMP_HIDDEN = 256
SEL_BLOCK = 64
SEL_TOPN = 16
WINDOW = 512
SB_HEADS = 8
ROPE_THETA = 10000.0
PEER_HEADS = 8
PEER_NKEYS = 128
PEER_QDIM = 256
PEER_TOPK = 16
EPS = 1e-6
NEG = -1e30

LANES = 128
VMEM_LIMIT = 56 * 1024 * 1024

CB_GATE = 0
CB_Q = 16
CB_KC, CB_VC, CB_KS, CB_VS, CB_KW, CB_VW = 20, 21, 22, 23, 24, 25
CB_QSB, CB_KSB, CB_VSB = 26, 30, 34
CB_GN = 38
N_CB = 39

SB_EXIT = 104.0


def _cparams(sem, vmem=None):
    return pltpu.CompilerParams(dimension_semantics=sem, vmem_limit_bytes=vmem or VMEM_LIMIT)


def _norm_matmul_kernel(x_ref, g_ref, w_ref, o_ref):
    x = x_ref[...]
    ms = jnp.mean(x * x, axis=-1, keepdims=True)
    xn = (x * lax.rsqrt(ms + EPS) * g_ref[...]).astype(BF16)
    o_ref[...] = jnp.dot(xn, w_ref[...], preferred_element_type=F32)


def _norm_matmul(x2d, gamma, w_bf16, tm, tn, name):
    t, d = x2d.shape
    n = w_bf16.shape[1]
    return pl.pallas_call(
        _norm_matmul_kernel,
        out_shape=jax.ShapeDtypeStruct((t, n), F32),
        grid=(n // tn, t // tm),
        in_specs=[
            pl.BlockSpec((tm, d), lambda j, i: (i, 0)),
            pl.BlockSpec((1, d), lambda j, i: (0, 0)),
            pl.BlockSpec((d, tn), lambda j, i: (0, j)),
        ],
        out_specs=pl.BlockSpec((tm, tn), lambda j, i: (i, j)),
        compiler_params=_cparams(("parallel", "parallel")),
        name=name,
    )(x2d, gamma.reshape(1, d), w_bf16)


def _group_mean_sq(x):
    ss = x * x
    r = lax.broadcasted_iota(jnp.int32, (LANES, LANES), 0) // HEAD_DIM
    c = lax.broadcasted_iota(jnp.int32, (LANES, LANES), 1) // HEAD_DIM
    ones_bd = jnp.where(r == c, 1.0, 0.0).astype(BF16)
    hi = ss.astype(BF16)
    lo = (ss - hi.astype(F32)).astype(BF16)
    tot = (jnp.dot(hi, ones_bd, preferred_element_type=F32)
           + jnp.dot(lo, ones_bd, preferred_element_type=F32))
    return tot * (1.0 / HEAD_DIM)


def _norm_rope(x, cos, sin_signed, gamma):
    xn = x * lax.rsqrt(_group_mean_sq(x) + EPS) * gamma
    lane = lax.broadcasted_iota(jnp.int32, xn.shape, 1)
    first_half = (lane % HEAD_DIM) < (HEAD_DIM // 2)
    swapped = jnp.where(first_half, pltpu.roll(xn, LANES - HEAD_DIM // 2, 1), pltpu.roll(xn, HEAD_DIM // 2, 1))
    return xn * cos + swapped * sin_signed


def _rope_tables(pos):
    half = HEAD_DIM // 2
    d = jnp.arange(LANES) % HEAD_DIM
    freqs = ROPE_THETA ** (-(d % half).astype(F32) / half)
    ang = pos.astype(F32)[:, None] * freqs[None, :]
    sign = jnp.where(d < half, -1.0, 1.0).astype(F32)
    return jnp.cos(ang), jnp.sin(ang) * sign[None, :]


def _norm_rope_kernel(x_ref, cos_ref, sin_ref, g_ref, o_ref):
    o_ref[...] = _norm_rope(x_ref[...], cos_ref[...], sin_ref[...], g_ref[0]).astype(o_ref.dtype)


def _prep_norm_rope(proj, cos, sin, gammas, seq, ts):
    t = proj.shape[0]
    n_slab = gammas.shape[0]
    s_tiles = seq // ts
    return pl.pallas_call(
        _norm_rope_kernel,
        out_shape=jax.ShapeDtypeStruct((t, n_slab * LANES), BF16),
        grid=(t // ts, n_slab),
        in_specs=[
            pl.BlockSpec((ts, LANES), lambda i, j: (i, CB_Q + j + 2 * (j // 4) + (j // 5))),
            pl.BlockSpec((ts, LANES), lambda i, j: (i % s_tiles, 0)),
            pl.BlockSpec((ts, LANES), lambda i, j: (i % s_tiles, 0)),
            pl.BlockSpec((1, 1, LANES), lambda i, j: (j, 0, 0)),
        ],
        out_specs=pl.BlockSpec((ts, LANES), lambda i, j: (i, j)),
        compiler_params=_cparams(("parallel", "parallel")),
        name="prep_norm_rope",
    )(proj, cos, sin, gammas)


T_SLABS = (CB_VS, CB_VW, CB_QSB, CB_QSB + 1, CB_QSB + 2, CB_QSB + 3,
           CB_VSB, CB_VSB + 1, CB_VSB + 2, CB_VSB + 3, CB_GN)
TS_VS, TS_VW, TS_QSB, TS_VSB, TS_GN = 0, 1, 2, 6, 10


def _transpose_kernel(cm_ref, x_ref, o_ref):
    del cm_ref
    o_ref[0] = x_ref[...].T.astype(o_ref.dtype)


def _prep_transpose(proj, batch, seq, ts):
    s_tiles = seq // ts
    colmap = jnp.asarray(T_SLABS, jnp.int32)
    return pl.pallas_call(
        _transpose_kernel,
        out_shape=jax.ShapeDtypeStruct((batch, len(T_SLABS) * LANES, seq), BF16),
        grid_spec=pltpu.PrefetchScalarGridSpec(
            num_scalar_prefetch=1,
            grid=(batch, s_tiles, len(T_SLABS)),
            in_specs=[pl.BlockSpec((ts, LANES), lambda b, i, j, cm: (b * s_tiles + i, cm[j]))],
            out_specs=pl.BlockSpec((1, LANES, ts), lambda b, i, j, cm: (b, j, i)),
        ),
        compiler_params=_cparams(("parallel", "parallel", "parallel")),
        name="prep_transpose",
    )(colmap, proj)


def _compress_kernel(a_ref, pa_ref, pb_ref, w1a_ref, w1b_ref, w2_ref, *rest, is_key):
    a = a_ref[0]
    ncp = a.shape[0]
    p = jnp.dot((a + pa_ref[...]).astype(BF16), w1a_ref[...], preferred_element_type=F32)
    q = jnp.dot((a + pb_ref[...]).astype(BF16), w1b_ref[...], preferred_element_type=F32)
    hid = jax.nn.gelu(p + pltpu.roll(q, ncp - 1, 0))
    if is_key:
        cos_ref, sin_ref, g_ref, o_ref = rest
        out = jnp.dot(hid.astype(BF16), w2_ref[...], preferred_element_type=F32)
        o_ref[0] = _norm_rope(out, cos_ref[...], sin_ref[...], g_ref[...]).astype(o_ref.dtype)
    else:
        (o_ref,) = rest
        nt_dims = (((1,), (1,)), ((), ()))
        out_t = lax.dot_general(w2_ref[...], hid.astype(BF16), nt_dims, preferred_element_type=F32)
        o_ref[0] = out_t.astype(o_ref.dtype)


def _expand_cmp_weights(w1, w2):
    eye = jnp.eye(NSA_GROUPS, dtype=F32)
    w1r = w1.reshape(2, CMP_STRIDE, HEAD_DIM, CMP_HIDDEN)
    ex = w1r[:, :, None, :, None, :] * eye[None, None, :, None, :, None]
    ex = ex.reshape(2, CMP_STRIDE * LANES, NSA_GROUPS * CMP_HIDDEN).astype(BF16)
    w2x = (w2[None, :, None, :] * eye[:, None, :, None]).reshape(NSA_GROUPS * CMP_HIDDEN, LANES).astype(BF16)
    return ex[0], ex[1], w2x


def _compress(tok_slab, pos_emb, w1, w2, batch, seq, key_args=None):
    ncp = seq // CMP_STRIDE
    a = tok_slab.reshape(batch, ncp, CMP_STRIDE * LANES)
    pos2 = jnp.tile(pos_emb.astype(F32)[:, None, :], (1, NSA_GROUPS, 1)).reshape(2, 1, CMP_STRIDE * LANES)
    w1a, w1b, w2x = _expand_cmp_weights(w1, w2)
    kdim = CMP_STRIDE * LANES
    hdim = NSA_GROUPS * CMP_HIDDEN
    if key_args is None:
        w2x = w2x.T
    in_specs = [
        pl.BlockSpec((1, ncp, kdim), lambda b: (b, 0, 0)),
        pl.BlockSpec((1, kdim), lambda b: (0, 0)),
        pl.BlockSpec((1, kdim), lambda b: (0, 0)),
        pl.BlockSpec((kdim, hdim), lambda b: (0, 0)),
        pl.BlockSpec((kdim, hdim), lambda b: (0, 0)),
        pl.BlockSpec(w2x.shape, lambda b: (0, 0)),
    ]
    args = [a, pos2[0], pos2[1], w1a, w1b, w2x]
    if key_args is not None:
        cos, sin, gamma = key_args
        in_specs += [pl.BlockSpec((ncp, LANES), lambda b: (0, 0)),
                     pl.BlockSpec((ncp, LANES), lambda b: (0, 0)),
                     pl.BlockSpec((1, LANES), lambda b: (0, 0))]
        args += [cos, sin, gamma]
        out_shape = jax.ShapeDtypeStruct((batch, ncp, LANES), BF16)
        out_spec = pl.BlockSpec((1, ncp, LANES), lambda b: (b, 0, 0))
    else:
        out_shape = jax.ShapeDtypeStruct((batch, LANES, ncp), BF16)
        out_spec = pl.BlockSpec((1, LANES, ncp), lambda b: (b, 0, 0))
    return pl.pallas_call(
        functools.partial(_compress_kernel, is_key=key_args is not None),
        out_shape=out_shape,
        grid=(batch,),
        in_specs=in_specs,
        out_specs=out_spec,
        compiler_params=_cparams(("parallel",)),
        name="compress_k" if key_args is not None else "compress_v",
    )(*args)


NSA_TQ = 256
NSA_KB = 1024


def _flash_step(k_tile, vt_tile, qtm, bias, carry):
    m, l, acc = carry
    s = jnp.dot(k_tile, qtm, preferred_element_type=F32) + bias
    m_new = jnp.maximum(m, jnp.max(s, axis=0, keepdims=True))
    alpha = jnp.exp(m - m_new)
    p = jnp.exp(s - m_new).astype(BF16)
    vt_ones = jnp.concatenate([vt_tile, jnp.ones((16, vt_tile.shape[1]), BF16)], axis=0)
    pv = jnp.dot(vt_ones, p, preferred_element_type=F32)
    l = alpha * l + pv[HEAD_DIM:HEAD_DIM + 1]
    acc = alpha * acc + pv[:HEAD_DIM]
    return m_new, l, acc


def _nsa_kernel(q_ref, kc_ref, vct_ref, ks_ref, kw_ref, vst_ref, vwt_ref, gate_ref, ovt_ref,
                o_ref, selb_ref, *, n_sel):
    g = pl.program_id(1)
    i = pl.program_id(2)
    tq, kb, hpg = NSA_TQ, NSA_KB, NSA_HPG
    n = hpg * tq
    t0 = i * tq
    grow = pl.multiple_of(g * HEAD_DIM, HEAD_DIM)

    qt = q_ref[0].astype(F32).T
    rowgrp = lax.broadcasted_iota(jnp.int32, (LANES, tq), 0) // HEAD_DIM
    parts = []
    for h in range(hpg):
        blk = qt[h * HEAD_DIM:(h + 1) * HEAD_DIM]
        parts.append(jnp.where(rowgrp == g, jnp.concatenate([blk, blk], axis=0), 0.0))
    qtm = jnp.concatenate(parts, axis=1).astype(BF16)

    tok1 = t0 + lax.broadcasted_iota(jnp.int32, (1, tq), 1)
    tokn = jnp.concatenate([tok1] * hpg, axis=1)

    ncp = kc_ref.shape[1]
    sc = jnp.dot(kc_ref[0], qtm, preferred_element_type=F32)
    cend = lax.broadcasted_iota(jnp.int32, (ncp, n), 0) * CMP_STRIDE + (CMP_LEN - 1)
    sc = jnp.where(cend <= tokn, sc, NEG)
    mc = jnp.max(sc, axis=0, keepdims=True)
    ec = jnp.exp(sc - mc)
    has_c = jnp.where(tokn >= CMP_LEN - 1, 1.0, 0.0)
    pc = ec * (has_c / jnp.sum(ec, axis=0, keepdims=True))
    o_cmp = jnp.dot(vct_ref[0, pl.ds(grow, HEAD_DIM), :], pc.astype(BF16), preferred_element_type=F32)

    psum = pc[:, 0:tq]
    for h in range(1, hpg):
        psum = psum + pc[:, h * tq:(h + 1) * tq]
    p_hi = psum.astype(BF16)
    p_lo = (psum - p_hi.astype(F32)).astype(BF16)
    imp = (jnp.dot(ovt_ref[...], p_hi, preferred_element_type=F32)
           + jnp.dot(ovt_ref[...], p_lo, preferred_element_type=F32))
    ns = imp.shape[0]
    jidx = lax.broadcasted_iota(jnp.int32, (ns, tq), 0)
    jf = jidx.astype(F32)
    tokb = jnp.broadcast_to(tok1, (ns, tq))
    cur = tokb // SEL_BLOCK
    forced = (jidx == 0) | (jidx == cur) | (jidx == cur - 1)
    valid = jidx * SEL_BLOCK <= tokb
    score = jnp.where(forced, 1e4, jnp.where(valid, imp, -1e4))
    selb = jnp.full((ns, tq), NEG, F32)
    for _ in range(n_sel):
        mx = jnp.max(score, axis=0, keepdims=True)
        first = jnp.min(jnp.where(score == mx, jf, float(ns)), axis=0, keepdims=True)
        pick = jf == first
        selb = jnp.where(pick, 0.0, selb)
        score = jnp.where(pick, -jnp.inf, score)
    selb_ref[...] = selb

    wlen = WINDOW + tq
    w0 = pl.multiple_of(jnp.maximum(t0 - WINDOW, 0), tq)
    wpos = w0 + lax.broadcasted_iota(jnp.int32, (wlen, tq), 0)
    wbias1 = jnp.where((wpos <= tok1) & (wpos > tok1 - WINDOW), 0.0, NEG)
    sw = (jnp.dot(kw_ref[0, pl.ds(w0, wlen), :], qtm, preferred_element_type=F32)
          + jnp.concatenate([wbias1] * hpg, axis=1))
    pw = jnp.exp(sw - jnp.max(sw, axis=0, keepdims=True)).astype(BF16)
    vw_ones = jnp.concatenate([vwt_ref[0, pl.ds(grow, HEAD_DIM), pl.ds(w0, wlen)],
                               jnp.ones((16, wlen), BF16)], axis=0)
    pvw = jnp.dot(vw_ones, pw, preferred_element_type=F32)
    o_win = pvw[:HEAD_DIM] / pvw[HEAD_DIM:HEAD_DIM + 1]

    krow = lax.broadcasted_iota(jnp.int32, (kb, tq), 0)
    init = (jnp.full((1, n), NEG, F32), jnp.zeros((1, n), F32), jnp.zeros((HEAD_DIM, n), F32))
    nblk = kb // SEL_BLOCK
    def sel_body(k, carry):
        k0 = pl.multiple_of(k * kb, kb)
        rows = [jnp.broadcast_to(selb_ref[pl.ds(k * nblk + r, 1), :], (SEL_BLOCK, tq)) for r in range(nblk)]
        bias1 = jnp.where(k0 + krow <= tok1, jnp.concatenate(rows, axis=0), NEG)
        bias = jnp.concatenate([bias1] * hpg, axis=1)
        return _flash_step(ks_ref[0, pl.ds(k0, kb), :], vst_ref[0, pl.ds(grow, HEAD_DIM), pl.ds(k0, kb)],
                           qtm, bias, carry)
    _, l_s, acc_s = lax.fori_loop(0, t0 // kb + 1, sel_body, init)

    gt = jax.nn.sigmoid(gate_ref[0, pl.ds(pl.multiple_of(g * 16, 16), 16), :].astype(F32))
    o_sel = acc_s / l_s
    for h in range(hpg):
        sl = slice(h * tq, (h + 1) * tq)
        o_h = (gt[h:h + 1] * o_cmp[:, sl] + gt[hpg + h:hpg + h + 1] * o_sel[:, sl]
               + gt[2 * hpg + h:2 * hpg + h + 1] * o_win[:, sl])
        o_ref[0, h * HEAD_DIM:(h + 1) * HEAD_DIM, :] = o_h.astype(o_ref.dtype)


def _nsa(qk, kc, vct, tposed, ovt, batch, seq):
    tq = NSA_TQ
    ncp = seq // CMP_STRIDE
    ns = seq // SEL_BLOCK
    n_sel = min(SEL_TOPN, ns)
    qw = NSA_HPG * HEAD_DIM
    return pl.pallas_call(
        functools.partial(_nsa_kernel, n_sel=n_sel),
        out_shape=jax.ShapeDtypeStruct((batch, NSA_HEADS * HEAD_DIM, seq), BF16),
        grid=(batch, NSA_GROUPS, seq // tq),
        in_specs=[
            pl.BlockSpec((1, tq, qw), lambda b, g, i: (b, i, g)),
            pl.BlockSpec((1, ncp, LANES), lambda b, g, i: (b, 0, 0)),
            pl.BlockSpec((1, LANES, ncp), lambda b, g, i: (b, 0, 0)),
            pl.BlockSpec((1, seq, LANES), lambda b, g, i: (b, 0, 4)),
            pl.BlockSpec((1, seq, LANES), lambda b, g, i: (b, 0, 5)),
            pl.BlockSpec((1, LANES, seq), lambda b, g, i: (b, TS_VS, 0)),
            pl.BlockSpec((1, LANES, seq), lambda b, g, i: (b, TS_VW, 0)),
            pl.BlockSpec((1, LANES, tq), lambda b, g, i: (b, TS_GN, i)),
            pl.BlockSpec((ns, ncp), lambda b, g, i: (0, 0)),
        ],
        out_specs=pl.BlockSpec((1, qw, tq), lambda b, g, i: (b, g, i)),
        scratch_shapes=[pltpu.VMEM((ns, tq), F32)],
        compiler_params=_cparams(("parallel", "parallel", "arbitrary")),
        name="nsa",
    )(qk, kc, vct, qk, qk, tposed, tposed, tposed, ovt)


SB_TQ = 256


SB_SLABS = 2


def _sb_kernel(qt_ref, k_ref, vt_ref, o_ref):
    i = pl.program_id(2)
    tq = SB_TQ
    kb = tq
    nh = 2 * SB_SLABS
    n = nh * tq
    scale = 1.0 / math.sqrt(HEAD_DIM)

    rowgrp = lax.broadcasted_iota(jnp.int32, (LANES, tq), 0) // HEAD_DIM
    qtm = []
    for p in range(SB_SLABS):
        q2 = qt_ref[0, p * LANES:(p + 1) * LANES, :].astype(F32) * scale
        qtm.append(jnp.concatenate([jnp.where(rowgrp == 0, q2, 0.0), jnp.where(rowgrp == 1, q2, 0.0)],
                                   axis=1).astype(BF16))

    r = lax.broadcasted_iota(jnp.int32, (kb, kb), 0)
    c = lax.broadcasted_iota(jnp.int32, (kb, kb), 1)
    later = jnp.where(c > r, 1.0, 0.0).astype(BF16)
    krow = lax.broadcasted_iota(jnp.int32, (kb, n), 0)
    tcol = lax.broadcasted_iota(jnp.int32, (kb, n), 1) % tq

    def body(st):
        k, carry, accs, _ = st
        k0 = pl.multiple_of(k * kb, kb)
        kt = k_ref[0, pl.ds(k0, kb), :].astype(BF16)
        z = jnp.concatenate([jnp.dot(kt[:, p * LANES:(p + 1) * LANES], qtm[p], preferred_element_type=F32)
                             for p in range(SB_SLABS)], axis=1)
        sp = jnp.maximum(z, 0.0) + jnp.log(1.0 + jnp.exp(-jnp.abs(z)))
        mask = (k0 + krow) < (i * tq + tcol)
        spm = jnp.where(mask, sp, 0.0)
        hi = spm.astype(BF16)
        lo = (spm - hi.astype(F32)).astype(BF16)
        after = (jnp.dot(later, hi, preferred_element_type=F32)
                 + jnp.dot(later, lo, preferred_element_type=F32))
        a = jnp.where(mask, jnp.exp(z - sp - after - carry), 0.0).astype(BF16)
        accs = tuple(accs[h] + jnp.dot(vt_ref[0, h * HEAD_DIM:(h + 1) * HEAD_DIM, pl.ds(k0, kb)],
                                       a[:, h * tq:(h + 1) * tq], preferred_element_type=F32)
                     for h in range(nh))
        carry = carry + after[0:1] + spm[0:1]
        go = jnp.logical_and(k > 0, jnp.min(carry) <= SB_EXIT)
        return k - 1, carry, accs, go

    zacc = jnp.zeros((HEAD_DIM, tq), F32)
    st = (i, jnp.zeros((1, n), F32), (zacc,) * nh, i >= 0)
    _, _, accs, _ = lax.while_loop(lambda s: s[3], body, st)
    for h in range(nh):
        o_ref[0, h * HEAD_DIM:(h + 1) * HEAD_DIM, :] = accs[h].astype(o_ref.dtype)


def _sb(proj3, tposed, batch, seq):
    tq = SB_TQ
    w = SB_SLABS * LANES
    assert TS_QSB % SB_SLABS == 0 and TS_VSB % SB_SLABS == 0 and CB_KSB % SB_SLABS == 0
    return pl.pallas_call(
        _sb_kernel,
        out_shape=jax.ShapeDtypeStruct((batch, SB_HEADS * HEAD_DIM, seq), BF16),
        grid=(batch, SB_HEADS // (2 * SB_SLABS), seq // tq),
        in_specs=[
            pl.BlockSpec((1, w, tq), lambda b, h, i: (b, TS_QSB // SB_SLABS + h, i)),
            pl.BlockSpec((1, seq, w), lambda b, h, i: (b, 0, CB_KSB // SB_SLABS + h)),
            pl.BlockSpec((1, w, seq), lambda b, h, i: (b, TS_VSB // SB_SLABS + h, 0)),
        ],
        out_specs=pl.BlockSpec((1, w, tq), lambda b, h, i: (b, h, i)),
        compiler_params=_cparams(("parallel", "parallel", "arbitrary")),
        name="stickbreaking",
    )(tposed, proj3, tposed)


def _merge_kernel(ont_ref, ost_ref, ga_ref, gb_ref, x_ref, wa_ref, wb_ref, wo_ref, o_ref):
    tn_dims = (((0,), (0,)), ((), ()))
    a = lax.dot_general(ont_ref[0], wa_ref[...], tn_dims, preferred_element_type=F32)
    b = lax.dot_general(ost_ref[0], wb_ref[...], tn_dims, preferred_element_type=F32)
    merged = jax.nn.sigmoid(ga_ref[...]) * a + jax.nn.sigmoid(gb_ref[...]) * b
    o_ref[...] = x_ref[...] + jnp.dot(merged.astype(BF16), wo_ref[...], preferred_element_type=F32)


def _merge(ont, ost, proj, x2d, wa, wb, wo, batch, seq, tm):
    d = x2d.shape[1]
    s_tiles = seq // tm
    hw = ont.shape[1]
    return pl.pallas_call(
        _merge_kernel,
        out_shape=jax.ShapeDtypeStruct(x2d.shape, F32),
        grid=(batch, s_tiles),
        in_specs=[
            pl.BlockSpec((1, hw, tm), lambda b, i: (b, 0, i)),
            pl.BlockSpec((1, hw, tm), lambda b, i: (b, 0, i)),
            pl.BlockSpec((tm, d), lambda b, i: (b * s_tiles + i, 0)),
            pl.BlockSpec((tm, d), lambda b, i: (b * s_tiles + i, 1)),
            pl.BlockSpec((tm, d), lambda b, i: (b * s_tiles + i, 0)),
            pl.BlockSpec((hw, d), lambda b, i: (0, 0)),
            pl.BlockSpec((hw, d), lambda b, i: (0, 0)),
            pl.BlockSpec((d, d), lambda b, i: (0, 0)),
        ],
        out_specs=pl.BlockSpec((tm, d), lambda b, i: (b * s_tiles + i, 0)),
        compiler_params=_cparams(("parallel", "parallel")),
        name="merge",
    )(ont, ost, proj, proj, x2d, wa, wb, wo)


PEER_TR = 1024


def _topk_ranked(s, idx, k, vals_ref=None, first_ref=None, want_rank=True):
    rows, n = s.shape
    sub = 8
    s = s.reshape(rows // sub, sub, n)
    idx = idx.reshape(rows // sub, sub, n)

    def all_sublanes(x, op):
        for sh in (4, 2, 1):
            x = op(x, pltpu.roll(x, sh, 0))
        return x

    rank = jnp.full(s.shape, float(k), F32) if want_rank else None
    for r in range(k):
        mx = all_sublanes(jnp.max(s, axis=0), jnp.maximum)
        first = all_sublanes(jnp.min(jnp.where(s == mx[None], idx, 1e9), axis=0), jnp.minimum)
        pick = idx == first[None]
        if want_rank:
            rank = jnp.where(pick, float(r), rank)
        s = jnp.where(pick, -jnp.inf, s)
        if vals_ref is not None:
            vals_ref[r:r + 1, :] = mx[0:1]
        if first_ref is not None:
            first_ref[r:r + 1, :] = first[0:1]
    return rank.reshape(rows, n) if want_rank else None


_CAND_GROUPS = ((0, 0, 16), (16, 1, 8), (24, 2, 8), (32, 3, 4), (36, 4, 4), (40, 5, 2), (42, 6, 2), (44, 7, 2))
_CAND_ROWS = 56


def _route_kernel(q_ref, sk_ref, n1_ref, a1_ref, b2_ref, e2_ref, v1_ref, v2_ref, f1_ref, cand_ref):
    k = PEER_TOPK
    half = PEER_QDIM // 2
    nt_dims = (((1,), (1,)), ((), ()))
    qh = q_ref[...].astype(BF16)
    s1 = lax.dot_general(sk_ref[0].astype(BF16), qh[:, :half], nt_dims, preferred_element_type=F32)
    s2 = lax.dot_general(sk_ref[1].astype(BF16), qh[:, half:], nt_dims, preferred_element_type=F32)
    tr = s1.shape[1]
    kidx = lax.broadcasted_iota(jnp.int32, s1.shape, 0).astype(F32)
    _topk_ranked(s1, kidx, k, v1_ref, f1_ref, want_rank=False)
    rank2 = _topk_ranked(s2, kidx, k, v2_ref)
    v1 = v1_ref[...]
    v2 = v2_ref[...]

    flat = lax.broadcasted_iota(jnp.int32, (_CAND_ROWS, tr), 0)
    row = flat
    for r0, a, nb in _CAND_GROUPS:
        cand_ref[r0:r0 + nb, :] = v1[a:a + 1] + v2[0:nb]
        flat = jnp.where((row >= r0) & (row < r0 + nb), row - r0 + a * k, flat)
    cand_ref[46:48, :] = jnp.full((2, tr), -jnp.inf, F32)
    cand_ref[48:56, :] = v1[8:16] + v2[0:1]
    flat = jnp.where(row >= 48, (row - 40) * k, flat).astype(F32)
    cand = cand_ref[...]
    crank = _topk_ranked(cand, flat, k)
    selc = jnp.where(crank < float(k), 1.0, 0.0)
    mx = v1[0:1] + v2[0:1]
    z = jnp.sum(jnp.where(crank < float(k), jnp.exp(cand - mx), 0.0), axis=0, keepdims=True)
    n_a = [jnp.sum(selc[r0:r0 + nb], axis=0, keepdims=True) for r0, _, nb in _CAND_GROUPS]
    n_a += [selc[48 + a - 8:48 + a - 7] for a in range(8, k)]

    f1 = f1_ref[...]
    n1 = jnp.zeros_like(s1)
    for a in range(k):
        n1 = jnp.where(kidx == f1[a:a + 1], n_a[a], n1)
    n1_ref[0] = n1
    a1_ref[0] = jnp.exp(s1 - v1[0:1]) / z
    b2_ref[0] = rank2.astype(b2_ref.dtype)
    e2_ref[0] = jnp.exp(s2 - v2[0:1]).astype(e2_ref.dtype)


def _route(qp, subkeys):
    t = qp.shape[0]
    tr = PEER_TR
    nk = PEER_NKEYS
    shp = jax.ShapeDtypeStruct((PEER_HEADS, nk, t), F32)
    shp16 = jax.ShapeDtypeStruct((PEER_HEADS, nk, t), BF16)
    spec = pl.BlockSpec((1, nk, tr), lambda i, h: (h, 0, i))
    return pl.pallas_call(
        _route_kernel,
        out_shape=(shp, shp, shp16, shp16),
        grid=(t // tr, PEER_HEADS),
        in_specs=[
            pl.BlockSpec((tr, PEER_QDIM), lambda i, h: (i, h)),
            pl.BlockSpec((2, nk, PEER_QDIM // 2), lambda i, h: (0, 0, 0)),
        ],
        out_specs=(spec, spec, spec, spec),
        scratch_shapes=[pltpu.VMEM((PEER_TOPK, tr), F32), pltpu.VMEM((PEER_TOPK, tr), F32),
                        pltpu.VMEM((PEER_TOPK, tr), F32), pltpu.VMEM((_CAND_ROWS, tr), F32)],
        compiler_params=_cparams(("parallel", "parallel")),
        name="peer_route",
    )(qp, subkeys)


PEER_TM = 512
PEER_NB = 16


def _peer_kernel(h_ref, g_ref, u_ref, vt_ref, n1_ref, a1_ref, b2_ref, e2_ref, o_ref, hnt_ref, acc_ref):
    kk = pl.program_id(1)
    nk = PEER_NKEYS

    @pl.when(kk == 0)
    def _():
        h = h_ref[...]
        ms = jnp.mean(h * h, axis=-1, keepdims=True)
        hnt_ref[...] = (h * lax.rsqrt(ms + EPS) * g_ref[...]).T.astype(BF16)
        acc_ref[...] = jnp.zeros_like(acc_ref)

    act = jax.nn.gelu(jnp.dot(u_ref[...], hnt_ref[...], preferred_element_type=F32)).astype(BF16)
    tm = act.shape[1]
    rep = nk // 16

    def bcast_row(ref, hd, i1):
        row16 = jnp.broadcast_to(ref[hd, pl.ds(i1, 1), :], (16, tm)).astype(BF16)
        return jnp.concatenate([row16] * rep, axis=0)

    parts = []
    for j in range(PEER_NB):
        i1 = kk * PEER_NB + j
        w = None
        for hd in range(PEER_HEADS):
            n1 = bcast_row(n1_ref, hd, i1)
            a1 = bcast_row(a1_ref, hd, i1)
            w_hd = jnp.where(b2_ref[hd] < n1, e2_ref[hd] * a1, jnp.zeros((), BF16))
            w = w_hd if w is None else w + w_hd
        parts.append(w * act[j * nk:(j + 1) * nk])
    mt = jnp.concatenate(parts, axis=0)
    acc_ref[...] += jnp.dot(vt_ref[...], mt, preferred_element_type=F32)

    @pl.when(kk == pl.num_programs(1) - 1)
    def _():
        o_ref[...] = h_ref[...] + acc_ref[...].T


def _peer(h2d, gamma, u_bf16, vt_bf16, tables):
    t, d = h2d.shape
    tm = min(PEER_TM, t)
    nk = PEER_NKEYS
    ke = PEER_NB * nk
    tspec = pl.BlockSpec((PEER_HEADS, nk, tm), lambda i, k: (0, 0, i))
    return pl.pallas_call(
        _peer_kernel,
        out_shape=jax.ShapeDtypeStruct((t, d), F32),
        grid=(t // tm, nk // PEER_NB),
        in_specs=[
            pl.BlockSpec((tm, d), lambda i, k: (i, 0)),
            pl.BlockSpec((1, d), lambda i, k: (0, 0)),
            pl.BlockSpec((ke, d), lambda i, k: (k, 0)),
            pl.BlockSpec((d, ke), lambda i, k: (0, k)),
            tspec, tspec, tspec, tspec,
        ],
        out_specs=pl.BlockSpec((tm, d), lambda i, k: (i, 0)),
        scratch_shapes=[pltpu.VMEM((d, tm), BF16), pltpu.VMEM((d, tm), F32)],
        compiler_params=_cparams(("parallel", "arbitrary")),
        name="peer_experts",
    )(h2d, gamma.reshape(1, d), u_bf16, vt_bf16, *tables)


def _reorder_w_in(w_in):
    d = w_in.shape[0]
    nsa_w = NSA_HEADS * HEAD_DIM
    kv_w = NSA_GROUPS * HEAD_DIM
    sb_w = SB_HEADS * HEAD_DIM
    c0 = nsa_w
    c1 = c0 + 6 * kv_w
    c2 = c1 + 3 * NSA_HEADS
    c3 = c2 + 3 * sb_w
    gn = w_in[:, c1:c2].reshape(d, 3, NSA_GROUPS, NSA_HPG).transpose(0, 2, 1, 3)
    gn = jnp.pad(gn.reshape(d, NSA_GROUPS, 3 * NSA_HPG), ((0, 0), (0, 0), (0, 16 - 3 * NSA_HPG)))
    gn = jnp.pad(gn.reshape(d, NSA_GROUPS * 16), ((0, 0), (0, LANES - NSA_GROUPS * 16)))
    return jnp.concatenate([w_in[:, c3:], w_in[:, :c0], w_in[:, c0:c1], w_in[:, c2:c3], gn], axis=1).astype(BF16)


def _overlap_t(seq):
    ncp = seq // CMP_STRIDE
    nc = (seq - CMP_LEN) // CMP_STRIDE + 1
    ns = seq // SEL_BLOCK
    c_start = jnp.arange(ncp) * CMP_STRIDE
    j_start = jnp.arange(ns) * SEL_BLOCK
    ov = jnp.clip(jnp.minimum(c_start[None, :] + CMP_LEN, j_start[:, None] + SEL_BLOCK)
                  - jnp.maximum(c_start[None, :], j_start[:, None]), 0).astype(F32) / CMP_LEN
    return jnp.where(jnp.arange(ncp)[None, :] < nc, ov, 0.0).astype(BF16)


def _token_mixers(x, norm1_g, w_in, q_norm_g, k_norm_g, cmp_pos_k, cmp_pos_v, cmp_k_w1, cmp_k_w2,
                  cmp_v_w1, cmp_v_w2, w_branch_nsa, w_branch_sb, w_out):
    batch, seq, d = x.shape
    t = batch * seq
    x2d = x.reshape(t, d)
    scale = 1.0 / math.sqrt(HEAD_DIM)

    proj = _norm_matmul(x2d, norm1_g, _reorder_w_in(w_in), tm=1024, tn=13 * LANES, name="in_proj")

    cos, sin = _rope_tables(jnp.arange(seq))
    qg = jnp.tile(q_norm_g.astype(F32), 2) * scale
    gammas = jnp.stack([qg] * 4 + [jnp.tile(k_norm_g[1].astype(F32), 2), jnp.tile(k_norm_g[2].astype(F32), 2)])
    qk = _prep_norm_rope(proj, cos, sin, gammas.reshape(6, 1, LANES), seq,
                         ts=min(2048, seq)).reshape(batch, seq, 6 * LANES)
    tposed = _prep_transpose(proj, batch, seq, ts=min(4096, seq))

    ncp = seq // CMP_STRIDE
    cosc, sinc = _rope_tables(jnp.arange(ncp) * CMP_STRIDE + CMP_LEN - 1)
    kc = _compress(proj[:, CB_KC * LANES:(CB_KC + 1) * LANES], cmp_pos_k, cmp_k_w1, cmp_k_w2, batch, seq,
                   key_args=(cosc, sinc, jnp.tile(k_norm_g[0].astype(F32), 2).reshape(1, LANES)))
    vct = _compress(proj[:, CB_VC * LANES:(CB_VC + 1) * LANES], cmp_pos_v, cmp_v_w1, cmp_v_w2, batch, seq)

    ont = _nsa(qk, kc, vct, tposed, _overlap_t(seq), batch, seq)
    ost = _sb(proj.reshape(batch, seq, N_CB * LANES), tposed, batch, seq)
    return _merge(ont, ost, proj, x2d, w_branch_nsa.astype(BF16), w_branch_sb.astype(BF16),
                  w_out.astype(BF16), batch, seq, tm=1024)


def kernel(x, norm1_g, w_in, q_norm_g, k_norm_g, cmp_pos_k, cmp_pos_v, cmp_k_w1, cmp_k_w2, cmp_v_w1, cmp_v_w2,
           w_branch_nsa, w_branch_sb, w_out, norm2_g, peer_w_query, peer_subkeys, peer_u, peer_v):
    batch, seq, d = x.shape
    h = x.astype(F32)
    for l in range(norm1_g.shape[0]):
        h2d = _token_mixers(h, norm1_g[l], w_in[l], q_norm_g[l], k_norm_g[l], cmp_pos_k[l], cmp_pos_v[l],
                            cmp_k_w1[l], cmp_k_w2[l], cmp_v_w1[l], cmp_v_w2[l],
                            w_branch_nsa[l], w_branch_sb[l], w_out[l])
        qp = _norm_matmul(h2d, norm2_g[l], peer_w_query[l].astype(BF16), tm=1024, tn=2048, name="peer_query")
        tables = _route(qp, peer_subkeys[l])
        h2d = _peer(h2d, norm2_g[l], peer_u[l].astype(BF16), peer_v[l].T.astype(BF16), tables)
        h = h2d.reshape(batch, seq, d)
    return h.astype(x.dtype)
```

```python
import functools
import math

import jax
import jax.numpy as jnp
from jax import lax
from jax.experimental import pallas as pl
from jax.experimental.pallas import tpu as pltpu

F32 = jnp.float32
BF16 = jnp.bfloat16

HEAD_DIM = 64
NSA_HEADS = 8
NSA_GROUPS = 2
NSA_HPG = NSA_HEADS // NSA_GROUPS
CMP_LEN = 32
CMP_STRIDE = 16
CMP_HIDDEN = 256
SEL_BLOCK = 64
SEL_TOPN = 16
WINDOW = 512
SB_HEADS = 8
ROPE_THETA = 10000.0
PEER_HEADS = 8
PEER_NKEYS = 128
PEER_QDIM = 256
PEER_TOPK = 16
EPS = 1e-6
NEG = -1e30

LANES = 128
VMEM_LIMIT = 56 * 1024 * 1024

CB_GATE = 0
CB_Q = 16
CB_KC, CB_VC, CB_KS, CB_VS, CB_KW, CB_VW = 20, 21, 22, 23, 24, 25
CB_QSB, CB_KSB, CB_VSB = 26, 30, 34
CB_GN = 38
N_CB = 39

SB_EXIT = 104.0


def _cparams(sem, vmem=None):
    return pltpu.CompilerParams(dimension_semantics=sem, vmem_limit_bytes=vmem or VMEM_LIMIT)


def _norm_matmul_kernel(x_ref, g_ref, w_ref, o_ref):
    x = x_ref[...]
    ms = jnp.mean(x * x, axis=-1, keepdims=True)
    xn = (x * lax.rsqrt(ms + EPS) * g_ref[...]).astype(BF16)
    o_ref[...] = jnp.dot(xn, w_ref[...], preferred_element_type=F32)


def _norm_matmul(x2d, gamma, w_bf16, tm, tn, name):
    t, d = x2d.shape
    n = w_bf16.shape[1]
    return pl.pallas_call(
        _norm_matmul_kernel,
        out_shape=jax.ShapeDtypeStruct((t, n), F32),
        grid=(n // tn, t // tm),
        in_specs=[
            pl.BlockSpec((tm, d), lambda j, i: (i, 0)),
            pl.BlockSpec((1, d), lambda j, i: (0, 0)),
            pl.BlockSpec((d, tn), lambda j, i: (0, j)),
        ],
        out_specs=pl.BlockSpec((tm, tn), lambda j, i: (i, j)),
        compiler_params=_cparams(("parallel", "parallel")),
        name=name,
    )(x2d, gamma.reshape(1, d), w_bf16)


def _group_mean_sq(x):
    ss = x * x
    r = lax.broadcasted_iota(jnp.int32, (LANES, LANES), 0) // HEAD_DIM
    c = lax.broadcasted_iota(jnp.int32, (LANES, LANES), 1) // HEAD_DIM
    ones_bd = jnp.where(r == c, 1.0, 0.0).astype(BF16)
    hi = ss.astype(BF16)
    lo = (ss - hi.astype(F32)).astype(BF16)
    tot = (jnp.dot(hi, ones_bd, preferred_element_type=F32)
           + jnp.dot(lo, ones_bd, preferred_element_type=F32))
    return tot * (1.0 / HEAD_DIM)


def _norm_rope(x, cos, sin_signed, gamma):
    xn = x * lax.rsqrt(_group_mean_sq(x) + EPS) * gamma
    lane = lax.broadcasted_iota(jnp.int32, xn.shape, 1)
    first_half = (lane % HEAD_DIM) < (HEAD_DIM // 2)
    swapped = jnp.where(first_half, pltpu.roll(xn, LANES - HEAD_DIM // 2, 1), pltpu.roll(xn, HEAD_DIM // 2, 1))
    return xn * cos + swapped * sin_signed


def _rope_tables(pos):
    half = HEAD_DIM // 2
    d = jnp.arange(LANES) % HEAD_DIM
    freqs = ROPE_THETA ** (-(d % half).astype(F32) / half)
    ang = pos.astype(F32)[:, None] * freqs[None, :]
    sign = jnp.where(d < half, -1.0, 1.0).astype(F32)
    return jnp.cos(ang), jnp.sin(ang) * sign[None, :]


def _norm_rope_kernel(x_ref, cos_ref, sin_ref, g_ref, o_ref):
    o_ref[...] = _norm_rope(x_ref[...], cos_ref[...], sin_ref[...], g_ref[0]).astype(o_ref.dtype)


def _prep_norm_rope(proj, cos, sin, gammas, seq, ts):
    t = proj.shape[0]
    n_slab = gammas.shape[0]
    s_tiles = seq // ts
    return pl.pallas_call(
        _norm_rope_kernel,
        out_shape=jax.ShapeDtypeStruct((t, n_slab * LANES), BF16),
        grid=(t // ts, n_slab),
        in_specs=[
            pl.BlockSpec((ts, LANES), lambda i, j: (i, CB_Q + j + 2 * (j // 4) + (j // 5))),
            pl.BlockSpec((ts, LANES), lambda i, j: (i % s_tiles, 0)),
            pl.BlockSpec((ts, LANES), lambda i, j: (i % s_tiles, 0)),
            pl.BlockSpec((1, 1, LANES), lambda i, j: (j, 0, 0)),
        ],
        out_specs=pl.BlockSpec((ts, LANES), lambda i, j: (i, j)),
        compiler_params=_cparams(("parallel", "parallel")),
        name="prep_norm_rope",
    )(proj, cos, sin, gammas)


T_SLABS = (CB_VS, CB_VW, CB_QSB, CB_QSB + 1, CB_QSB + 2, CB_QSB + 3,
           CB_VSB, CB_VSB + 1, CB_VSB + 2, CB_VSB + 3, CB_GN)
TS_VS, TS_VW, TS_QSB, TS_VSB, TS_GN = 0, 1, 2, 6, 10


def _transpose_kernel(cm_ref, x_ref, o_ref):
    del cm_ref
    o_ref[0] = x_ref[...].T.astype(o_ref.dtype)


def _prep_transpose(proj, batch, seq, ts):
    s_tiles = seq // ts
    colmap = jnp.asarray(T_SLABS, jnp.int32)
    return pl.pallas_call(
        _transpose_kernel,
        out_shape=jax.ShapeDtypeStruct((batch, len(T_SLABS) * LANES, seq), BF16),
        grid_spec=pltpu.PrefetchScalarGridSpec(
            num_scalar_prefetch=1,
            grid=(batch, s_tiles, len(T_SLABS)),
            in_specs=[pl.BlockSpec((ts, LANES), lambda b, i, j, cm: (b * s_tiles + i, cm[j]))],
            out_specs=pl.BlockSpec((1, LANES, ts), lambda b, i, j, cm: (b, j, i)),
        ),
        compiler_params=_cparams(("parallel", "parallel", "parallel")),
        name="prep_transpose",
    )(colmap, proj)


def _compress_kernel(a_ref, pa_ref, pb_ref, w1a_ref, w1b_ref, w2_ref, *rest, is_key):
    a = a_ref[0]
    ncp = a.shape[0]
    p = jnp.dot((a + pa_ref[...]).astype(BF16), w1a_ref[...], preferred_element_type=F32)
    q = jnp.dot((a + pb_ref[...]).astype(BF16), w1b_ref[...], preferred_element_type=F32)
    hid = jax.nn.gelu(p + pltpu.roll(q, ncp - 1, 0))
    if is_key:
        cos_ref, sin_ref, g_ref, o_ref = rest
        out = jnp.dot(hid.astype(BF16), w2_ref[...], preferred_element_type=F32)
        o_ref[0] = _norm_rope(out, cos_ref[...], sin_ref[...], g_ref[...]).astype(o_ref.dtype)
    else:
        (o_ref,) = rest
        nt_dims = (((1,), (1,)), ((), ()))
        out_t = lax.dot_general(w2_ref[...], hid.astype(BF16), nt_dims, preferred_element_type=F32)
        o_ref[0] = out_t.astype(o_ref.dtype)


def _expand_cmp_weights(w1, w2):
    eye = jnp.eye(NSA_GROUPS, dtype=F32)
    w1r = w1.reshape(2, CMP_STRIDE, HEAD_DIM, CMP_HIDDEN)
    ex = w1r[:, :, None, :, None, :] * eye[None, None, :, None, :, None]
    ex = ex.reshape(2, CMP_STRIDE * LANES, NSA_GROUPS * CMP_HIDDEN).astype(BF16)
    w2x = (w2[None, :, None, :] * eye[:, None, :, None]).reshape(NSA_GROUPS * CMP_HIDDEN, LANES).astype(BF16)
    return ex[0], ex[1], w2x


def _compress(tok_slab, pos_emb, w1, w2, batch, seq, key_args=None):
    ncp = seq // CMP_STRIDE
    a = tok_slab.reshape(batch, ncp, CMP_STRIDE * LANES)
    pos2 = jnp.tile(pos_emb.astype(F32)[:, None, :], (1, NSA_GROUPS, 1)).reshape(2, 1, CMP_STRIDE * LANES)
    w1a, w1b, w2x = _expand_cmp_weights(w1, w2)
    kdim = CMP_STRIDE * LANES
    hdim = NSA_GROUPS * CMP_HIDDEN
    if key_args is None:
        w2x = w2x.T
    in_specs = [
        pl.BlockSpec((1, ncp, kdim), lambda b: (b, 0, 0)),
        pl.BlockSpec((1, kdim), lambda b: (0, 0)),
        pl.BlockSpec((1, kdim), lambda b: (0, 0)),
        pl.BlockSpec((kdim, hdim), lambda b: (0, 0)),
        pl.BlockSpec((kdim, hdim), lambda b: (0, 0)),
        pl.BlockSpec(w2x.shape, lambda b: (0, 0)),
    ]
    args = [a, pos2[0], pos2[1], w1a, w1b, w2x]
    if key_args is not None:
        cos, sin, gamma = key_args
        in_specs += [pl.BlockSpec((ncp, LANES), lambda b: (0, 0)),
                     pl.BlockSpec((ncp, LANES), lambda b: (0, 0)),
                     pl.BlockSpec((1, LANES), lambda b: (0, 0))]
        args += [cos, sin, gamma]
        out_shape = jax.ShapeDtypeStruct((batch, ncp, LANES), BF16)
        out_spec = pl.BlockSpec((1, ncp, LANES), lambda b: (b, 0, 0))
    else:
        out_shape = jax.ShapeDtypeStruct((batch, LANES, ncp), BF16)
        out_spec = pl.BlockSpec((1, LANES, ncp), lambda b: (b, 0, 0))
    return pl.pallas_call(
        functools.partial(_compress_kernel, is_key=key_args is not None),
        out_shape=out_shape,
        grid=(batch,),
        in_specs=in_specs,
        out_specs=out_spec,
        compiler_params=_cparams(("parallel",)),
        name="compress_k" if key_args is not None else "compress_v",
    )(*args)


NSA_TQ = 256
NSA_KB = 1024


def _flash_step(k_tile, vt_tile, qtm, bias, carry):
    m, l, acc = carry
    s = jnp.dot(k_tile, qtm, preferred_element_type=F32) + bias
    m_new = jnp.maximum(m, jnp.max(s, axis=0, keepdims=True))
    alpha = jnp.exp(m - m_new)
    p = jnp.exp(s - m_new).astype(BF16)
    vt_ones = jnp.concatenate([vt_tile, jnp.ones((16, vt_tile.shape[1]), BF16)], axis=0)
    pv = jnp.dot(vt_ones, p, preferred_element_type=F32)
    l = alpha * l + pv[HEAD_DIM:HEAD_DIM + 1]
    acc = alpha * acc + pv[:HEAD_DIM]
    return m_new, l, acc


def _nsa_kernel(q_ref, kc_ref, vct_ref, ks_ref, kw_ref, vst_ref, vwt_ref, gate_ref, ovt_ref,
                o_ref, selb_ref, *, n_sel):
    g = pl.program_id(1)
    i = pl.program_id(2)
    tq, kb, hpg = NSA_TQ, NSA_KB, NSA_HPG
    n = hpg * tq
    t0 = i * tq
    grow = pl.multiple_of(g * HEAD_DIM, HEAD_DIM)

    qt = q_ref[0].astype(F32).T
    rowgrp = lax.broadcasted_iota(jnp.int32, (LANES, tq), 0) // HEAD_DIM
    parts = []
    for h in range(hpg):
        blk = qt[h * HEAD_DIM:(h + 1) * HEAD_DIM]
        parts.append(jnp.where(rowgrp == g, jnp.concatenate([blk, blk], axis=0), 0.0))
    qtm = jnp.concatenate(parts, axis=1).astype(BF16)

    tok1 = t0 + lax.broadcasted_iota(jnp.int32, (1, tq), 1)
    tokn = jnp.concatenate([tok1] * hpg, axis=1)

    ncp = kc_ref.shape[1]
    sc = jnp.dot(kc_ref[0], qtm, preferred_element_type=F32)
    cend = lax.broadcasted_iota(jnp.int32, (ncp, n), 0) * CMP_STRIDE + (CMP_LEN - 1)
    sc = jnp.where(cend <= tokn, sc, NEG)
    mc = jnp.max(sc, axis=0, keepdims=True)
    ec = jnp.exp(sc - mc)
    has_c = jnp.where(tokn >= CMP_LEN - 1, 1.0, 0.0)
    pc = ec * (has_c / jnp.sum(ec, axis=0, keepdims=True))
    o_cmp = jnp.dot(vct_ref[0, pl.ds(grow, HEAD_DIM), :], pc.astype(BF16), preferred_element_type=F32)

    psum = pc[:, 0:tq]
    for h in range(1, hpg):
        psum = psum + pc[:, h * tq:(h + 1) * tq]
    p_hi = psum.astype(BF16)
    p_lo = (psum - p_hi.astype(F32)).astype(BF16)
    imp = (jnp.dot(ovt_ref[...], p_hi, preferred_element_type=F32)
           + jnp.dot(ovt_ref[...], p_lo, preferred_element_type=F32))
    ns = imp.shape[0]
    jidx = lax.broadcasted_iota(jnp.int32, (ns, tq), 0)
    jf = jidx.astype(F32)
    tokb = jnp.broadcast_to(tok1, (ns, tq))
    cur = tokb // SEL_BLOCK
    forced = (jidx == 0) | (jidx == cur) | (jidx == cur - 1)
    valid = jidx * SEL_BLOCK <= tokb
    score = jnp.where(forced, 1e4, jnp.where(valid, imp, -1e4))
    selb = jnp.full((ns, tq), NEG, F32)
    for _ in range(n_sel):
        mx = jnp.max(score, axis=0, keepdims=True)
        first = jnp.min(jnp.where(score == mx, jf, float(ns)), axis=0, keepdims=True)
        pick = jf == first
        selb = jnp.where(pick, 0.0, selb)
        score = jnp.where(pick, -jnp.inf, score)
    selb_ref[...] = selb

    wlen = WINDOW + tq
    w0 = pl.multiple_of(jnp.maximum(t0 - WINDOW, 0), tq)
    wpos = w0 + lax.broadcasted_iota(jnp.int32, (wlen, tq), 0)
    wbias1 = jnp.where((wpos <= tok1) & (wpos > tok1 - WINDOW), 0.0, NEG)
    sw = (jnp.dot(kw_ref[0, pl.ds(w0, wlen), :], qtm, preferred_element_type=F32)
          + jnp.concatenate([wbias1] * hpg, axis=1))
    pw = jnp.exp(sw - jnp.max(sw, axis=0, keepdims=True)).astype(BF16)
    vw_ones = jnp.concatenate([vwt_ref[0, pl.ds(grow, HEAD_DIM), pl.ds(w0, wlen)],
                               jnp.ones((16, wlen), BF16)], axis=0)
    pvw = jnp.dot(vw_ones, pw, preferred_element_type=F32)
    o_win = pvw[:HEAD_DIM] / pvw[HEAD_DIM:HEAD_DIM + 1]

    krow = lax.broadcasted_iota(jnp.int32, (kb, tq), 0)
    init = (jnp.full((1, n), NEG, F32), jnp.zeros((1, n), F32), jnp.zeros((HEAD_DIM, n), F32))
    nblk = kb // SEL_BLOCK
    def sel_body(k, carry):
        k0 = pl.multiple_of(k * kb, kb)
        rows = [jnp.broadcast_to(selb_ref[pl.ds(k * nblk + r, 1), :], (SEL_BLOCK, tq)) for r in range(nblk)]
        bias1 = jnp.where(k0 + krow <= tok1, jnp.concatenate(rows, axis=0), NEG)
        bias = jnp.concatenate([bias1] * hpg, axis=1)
        return _flash_step(ks_ref[0, pl.ds(k0, kb), :], vst_ref[0, pl.ds(grow, HEAD_DIM), pl.ds(k0, kb)],
                           qtm, bias, carry)
    _, l_s, acc_s = lax.fori_loop(0, t0 // kb + 1, sel_body, init)

    gt = jax.nn.sigmoid(gate_ref[0, pl.ds(pl.multiple_of(g * 16, 16), 16), :].astype(F32))
    o_sel = acc_s / l_s
    for h in range(hpg):
        sl = slice(h * tq, (h + 1) * tq)
        o_h = (gt[h:h + 1] * o_cmp[:, sl] + gt[hpg + h:hpg + h + 1] * o_sel[:, sl]
               + gt[2 * hpg + h:2 * hpg + h + 1] * o_win[:, sl])
        o_ref[0, h * HEAD_DIM:(h + 1) * HEAD_DIM, :] = o_h.astype(o_ref.dtype)


def _nsa(qk, kc, vct, tposed, ovt, batch, seq):
    tq = NSA_TQ
    ncp = seq // CMP_STRIDE
    ns = seq // SEL_BLOCK
    n_sel = min(SEL_TOPN, ns)
    qw = NSA_HPG * HEAD_DIM
    return pl.pallas_call(
        functools.partial(_nsa_kernel, n_sel=n_sel),
        out_shape=jax.ShapeDtypeStruct((batch, NSA_HEADS * HEAD_DIM, seq), BF16),
        grid=(batch, NSA_GROUPS, seq // tq),
        in_specs=[
            pl.BlockSpec((1, tq, qw), lambda b, g, i: (b, i, g)),
            pl.BlockSpec((1, ncp, LANES), lambda b, g, i: (b, 0, 0)),
            pl.BlockSpec((1, LANES, ncp), lambda b, g, i: (b, 0, 0)),
            pl.BlockSpec((1, seq, LANES), lambda b, g, i: (b, 0, 4)),
            pl.BlockSpec((1, seq, LANES), lambda b, g, i: (b, 0, 5)),
            pl.BlockSpec((1, LANES, seq), lambda b, g, i: (b, TS_VS, 0)),
            pl.BlockSpec((1, LANES, seq), lambda b, g, i: (b, TS_VW, 0)),
            pl.BlockSpec((1, LANES, tq), lambda b, g, i: (b, TS_GN, i)),
            pl.BlockSpec((ns, ncp), lambda b, g, i: (0, 0)),
        ],
        out_specs=pl.BlockSpec((1, qw, tq), lambda b, g, i: (b, g, i)),
        scratch_shapes=[pltpu.VMEM((ns, tq), F32)],
        compiler_params=_cparams(("parallel", "parallel", "arbitrary")),
        name="nsa",
    )(qk, kc, vct, qk, qk, tposed, tposed, tposed, ovt)


SB_TQ = 256


SB_SLABS = 2


def _sb_kernel(qt_ref, k_ref, vt_ref, o_ref):
    i = pl.program_id(2)
    tq = SB_TQ
    kb = tq
    nh = 2 * SB_SLABS
    n = nh * tq
    scale = 1.0 / math.sqrt(HEAD_DIM)

    rowgrp = lax.broadcasted_iota(jnp.int32, (LANES, tq), 0) // HEAD_DIM
    qtm = []
    for p in range(SB_SLABS):
        q2 = qt_ref[0, p * LANES:(p + 1) * LANES, :].astype(F32) * scale
        qtm.append(jnp.concatenate([jnp.where(rowgrp == 0, q2, 0.0), jnp.where(rowgrp == 1, q2, 0.0)],
                                   axis=1).astype(BF16))

    r = lax.broadcasted_iota(jnp.int32, (kb, kb), 0)
    c = lax.broadcasted_iota(jnp.int32, (kb, kb), 1)
    later = jnp.where(c > r, 1.0, 0.0).astype(BF16)
    krow = lax.broadcasted_iota(jnp.int32, (kb, n), 0)
    tcol = lax.broadcasted_iota(jnp.int32, (kb, n), 1) % tq

    def body(st):
        k, carry, accs, _ = st
        k0 = pl.multiple_of(k * kb, kb)
        kt = k_ref[0, pl.ds(k0, kb), :].astype(BF16)
        z = jnp.concatenate([jnp.dot(kt[:, p * LANES:(p + 1) * LANES], qtm[p], preferred_element_type=F32)
                             for p in range(SB_SLABS)], axis=1)
        sp = jnp.maximum(z, 0.0) + jnp.log(1.0 + jnp.exp(-jnp.abs(z)))
        mask = (k0 + krow) < (i * tq + tcol)
        spm = jnp.where(mask, sp, 0.0)
        hi = spm.astype(BF16)
        lo = (spm - hi.astype(F32)).astype(BF16)
        after = (jnp.dot(later, hi, preferred_element_type=F32)
                 + jnp.dot(later, lo, preferred_element_type=F32))
        a = jnp.where(mask, jnp.exp(z - sp - after - carry), 0.0).astype(BF16)
        accs = tuple(accs[h] + jnp.dot(vt_ref[0, h * HEAD_DIM:(h + 1) * HEAD_DIM, pl.ds(k0, kb)],
                                       a[:, h * tq:(h + 1) * tq], preferred_element_type=F32)
                     for h in range(nh))
        carry = carry + after[0:1] + spm[0:1]
        go = jnp.logical_and(k > 0, jnp.min(carry) <= SB_EXIT)
        return k - 1, carry, accs, go

    zacc = jnp.zeros((HEAD_DIM, tq), F32)
    st = (i, jnp.zeros((1, n), F32), (zacc,) * nh, i >= 0)
    _, _, accs, _ = lax.while_loop(lambda s: s[3], body, st)
    for h in range(nh):
        o_ref[0, h * HEAD_DIM:(h + 1) * HEAD_DIM, :] = accs[h].astype(o_ref.dtype)


def _sb(proj3, tposed, batch, seq):
    tq = SB_TQ
    w = SB_SLABS * LANES
    assert TS_QSB % SB_SLABS == 0 and TS_VSB % SB_SLABS == 0 and CB_KSB % SB_SLABS == 0
    return pl.pallas_call(
        _sb_kernel,
        out_shape=jax.ShapeDtypeStruct((batch, SB_HEADS * HEAD_DIM, seq), BF16),
        grid=(batch, SB_HEADS // (2 * SB_SLABS), seq // tq),
        in_specs=[
            pl.BlockSpec((1, w, tq), lambda b, h, i: (b, TS_QSB // SB_SLABS + h, i)),
            pl.BlockSpec((1, seq, w), lambda b, h, i: (b, 0, CB_KSB // SB_SLABS + h)),
            pl.BlockSpec((1, w, seq), lambda b, h, i: (b, TS_VSB // SB_SLABS + h, 0)),
        ],
        out_specs=pl.BlockSpec((1, w, tq), lambda b, h, i: (b, h, i)),
        compiler_params=_cparams(("parallel", "parallel", "arbitrary")),
        name="stickbreaking",
    )(tposed, proj3, tposed)


def _merge_kernel(ont_ref, ost_ref, ga_ref, gb_ref, x_ref, wa_ref, wb_ref, wo_ref, o_ref):
    tn_dims = (((0,), (0,)), ((), ()))
    a = lax.dot_general(ont_ref[0], wa_ref[...], tn_dims, preferred_element_type=F32)
    b = lax.dot_general(ost_ref[0], wb_ref[...], tn_dims, preferred_element_type=F32)
    merged = jax.nn.sigmoid(ga_ref[...]) * a + jax.nn.sigmoid(gb_ref[...]) * b
    o_ref[...] = x_ref[...] + jnp.dot(merged.astype(BF16), wo_ref[...], preferred_element_type=F32)


def _merge(ont, ost, proj, x2d, wa, wb, wo, batch, seq, tm):
    d = x2d.shape[1]
    s_tiles = seq // tm
    hw = ont.shape[1]
    return pl.pallas_call(
        _merge_kernel,
        out_shape=jax.ShapeDtypeStruct(x2d.shape, F32),
        grid=(batch, s_tiles),
        in_specs=[
            pl.BlockSpec((1, hw, tm), lambda b, i: (b, 0, i)),
            pl.BlockSpec((1, hw, tm), lambda b, i: (b, 0, i)),
            pl.BlockSpec((tm, d), lambda b, i: (b * s_tiles + i, 0)),
            pl.BlockSpec((tm, d), lambda b, i: (b * s_tiles + i, 1)),
            pl.BlockSpec((tm, d), lambda b, i: (b * s_tiles + i, 0)),
            pl.BlockSpec((hw, d), lambda b, i: (0, 0)),
            pl.BlockSpec((hw, d), lambda b, i: (0, 0)),
            pl.BlockSpec((d, d), lambda b, i: (0, 0)),
        ],
        out_specs=pl.BlockSpec((tm, d), lambda b, i: (b * s_tiles + i, 0)),
        compiler_params=_cparams(("parallel", "parallel")),
        name="merge",
    )(ont, ost, proj, proj, x2d, wa, wb, wo)


PEER_TR = 1024


def _topk_ranked(s, idx, k, vals_ref=None, first_ref=None, want_rank=True):
    rows, n = s.shape
    sub = 8
    s = s.reshape(rows // sub, sub, n)
    idx = idx.reshape(rows // sub, sub, n)

    def all_sublanes(x, op):
        for sh in (4, 2, 1):
            x = op(x, pltpu.roll(x, sh, 0))
        return x

    rank = jnp.full(s.shape, float(k), F32) if want_rank else None
    for r in range(k):
        mx = all_sublanes(jnp.max(s, axis=0), jnp.maximum)
        first = all_sublanes(jnp.min(jnp.where(s == mx[None], idx, 1e9), axis=0), jnp.minimum)
        pick = idx == first[None]
        if want_rank:
            rank = jnp.where(pick, float(r), rank)
        s = jnp.where(pick, -jnp.inf, s)
        if vals_ref is not None:
            vals_ref[r:r + 1, :] = mx[0:1]
        if first_ref is not None:
            first_ref[r:r + 1, :] = first[0:1]
    return rank.reshape(rows, n) if want_rank else None


_CAND_GROUPS = ((0, 0, 16), (16, 1, 8), (24, 2, 8), (32, 3, 4), (36, 4, 4), (40, 5, 2), (42, 6, 2), (44, 7, 2))
_CAND_ROWS = 56


def _route_kernel(q_ref, sk_ref, n1_ref, a1_ref, b2_ref, e2_ref, v1_ref, v2_ref, f1_ref, cand_ref):
    k = PEER_TOPK
    half = PEER_QDIM // 2
    nt_dims = (((1,), (1,)), ((), ()))
    qh = q_ref[...].astype(BF16)
    s1 = lax.dot_general(sk_ref[0].astype(BF16), qh[:, :half], nt_dims, preferred_element_type=F32)
    s2 = lax.dot_general(sk_ref[1].astype(BF16), qh[:, half:], nt_dims, preferred_element_type=F32)
    tr = s1.shape[1]
    kidx = lax.broadcasted_iota(jnp.int32, s1.shape, 0).astype(F32)
    _topk_ranked(s1, kidx, k, v1_ref, f1_ref, want_rank=False)
    rank2 = _topk_ranked(s2, kidx, k, v2_ref)
    v1 = v1_ref[...]
    v2 = v2_ref[...]

    flat = lax.broadcasted_iota(jnp.int32, (_CAND_ROWS, tr), 0)
    row = flat
    for r0, a, nb in _CAND_GROUPS:
        cand_ref[r0:r0 + nb, :] = v1[a:a + 1] + v2[0:nb]
        flat = jnp.where((row >= r0) & (row < r0 + nb), row - r0 + a * k, flat)
    cand_ref[46:48, :] = jnp.full((2, tr), -jnp.inf, F32)
    cand_ref[48:56, :] = v1[8:16] + v2[0:1]
    flat = jnp.where(row >= 48, (row - 40) * k, flat).astype(F32)
    cand = cand_ref[...]
    crank = _topk_ranked(cand, flat, k)
    selc = jnp.where(crank < float(k), 1.0, 0.0)
    mx = v1[0:1] + v2[0:1]
    z = jnp.sum(jnp.where(crank < float(k), jnp.exp(cand - mx), 0.0), axis=0, keepdims=True)
    n_a = [jnp.sum(selc[r0:r0 + nb], axis=0, keepdims=True) for r0, _, nb in _CAND_GROUPS]
    n_a += [selc[48 + a - 8:48 + a - 7] for a in range(8, k)]

    f1 = f1_ref[...]
    n1 = jnp.zeros_like(s1)
    for a in range(k):
        n1 = jnp.where(kidx == f1[a:a + 1], n_a[a], n1)
    n1_ref[0] = n1
    a1_ref[0] = jnp.exp(s1 - v1[0:1]) / z
    b2_ref[0] = rank2.astype(b2_ref.dtype)
    e2_ref[0] = jnp.exp(s2 - v2[0:1]).astype(e2_ref.dtype)


def _route(qp, subkeys):
    t = qp.shape[0]
    tr = PEER_TR
    nk = PEER_NKEYS
    shp = jax.ShapeDtypeStruct((PEER_HEADS, nk, t), F32)
    shp16 = jax.ShapeDtypeStruct((PEER_HEADS, nk, t), BF16)
    spec = pl.BlockSpec((1, nk, tr), lambda i, h: (h, 0, i))
    return pl.pallas_call(
        _route_kernel,
        out_shape=(shp, shp, shp16, shp16),
        grid=(t // tr, PEER_HEADS),
        in_specs=[
            pl.BlockSpec((tr, PEER_QDIM), lambda i, h: (i, h)),
            pl.BlockSpec((2, nk, PEER_QDIM // 2), lambda i, h: (0, 0, 0)),
        ],
        out_specs=(spec, spec, spec, spec),
        scratch_shapes=[pltpu.VMEM((PEER_TOPK, tr), F32), pltpu.VMEM((PEER_TOPK, tr), F32),
                        pltpu.VMEM((PEER_TOPK, tr), F32), pltpu.VMEM((_CAND_ROWS, tr), F32)],
        compiler_params=_cparams(("parallel", "parallel")),
        name="peer_route",
    )(qp, subkeys)


PEER_TM = 512
PEER_NB = 16


def _peer_kernel(h_ref, g_ref, u_ref, vt_ref, n1_ref, a1_ref, b2_ref, e2_ref, o_ref, hnt_ref, acc_ref):
    kk = pl.program_id(1)
    nk = PEER_NKEYS

    @pl.when(kk == 0)
    def _():
        h = h_ref[...]
        ms = jnp.mean(h * h, axis=-1, keepdims=True)
        hnt_ref[...] = (h * lax.rsqrt(ms + EPS) * g_ref[...]).T.astype(BF16)
        acc_ref[...] = jnp.zeros_like(acc_ref)

    act = jax.nn.gelu(jnp.dot(u_ref[...], hnt_ref[...], preferred_element_type=F32)).astype(BF16)
    tm = act.shape[1]
    rep = nk // 16

    def bcast_row(ref, hd, i1):
        row16 = jnp.broadcast_to(ref[hd, pl.ds(i1, 1), :], (16, tm)).astype(BF16)
        return jnp.concatenate([row16] * rep, axis=0)

    parts = []
    for j in range(PEER_NB):
        i1 = kk * PEER_NB + j
        w = None
        for hd in range(PEER_HEADS):
            n1 = bcast_row(n1_ref, hd, i1)
            a1 = bcast_row(a1_ref, hd, i1)
            w_hd = jnp.where(b2_ref[hd] < n1, e2_ref[hd] * a1, jnp.zeros((), BF16))
            w = w_hd if w is None else w + w_hd
        parts.append(w * act[j * nk:(j + 1) * nk])
    mt = jnp.concatenate(parts, axis=0)
    acc_ref[...] += jnp.dot(vt_ref[...], mt, preferred_element_type=F32)

    @pl.when(kk == pl.num_programs(1) - 1)
    def _():
        o_ref[...] = h_ref[...] + acc_ref[...].T


def _peer(h2d, gamma, u_bf16, vt_bf16, tables):
    t, d = h2d.shape
    tm = min(PEER_TM, t)
    nk = PEER_NKEYS
    ke = PEER_NB * nk
    tspec = pl.BlockSpec((PEER_HEADS, nk, tm), lambda i, k: (0, 0, i))
    return pl.pallas_call(
        _peer_kernel,
        out_shape=jax.ShapeDtypeStruct((t, d), F32),
        grid=(t // tm, nk // PEER_NB),
        in_specs=[
            pl.BlockSpec((tm, d), lambda i, k: (i, 0)),
            pl.BlockSpec((1, d), lambda i, k: (0, 0)),
            pl.BlockSpec((ke, d), lambda i, k: (k, 0)),
            pl.BlockSpec((d, ke), lambda i, k: (0, k)),
            tspec, tspec, tspec, tspec,
        ],
        out_specs=pl.BlockSpec((tm, d), lambda i, k: (i, 0)),
        scratch_shapes=[pltpu.VMEM((d, tm), BF16), pltpu.VMEM((d, tm), F32)],
        compiler_params=_cparams(("parallel", "arbitrary")),
        name="peer_experts",
    )(h2d, gamma.reshape(1, d), u_bf16, vt_bf16, *tables)


def _reorder_w_in(w_in):
    d = w_in.shape[0]
    nsa_w = NSA_HEADS * HEAD_DIM
    kv_w = NSA_GROUPS * HEAD_DIM
    sb_w = SB_HEADS * HEAD_DIM
    c0 = nsa_w
    c1 = c0 + 6 * kv_w
    c2 = c1 + 3 * NSA_HEADS
    c3 = c2 + 3 * sb_w
    gn = w_in[:, c1:c2].reshape(d, 3, NSA_GROUPS, NSA_HPG).transpose(0, 2, 1, 3)
    gn = jnp.pad(gn.reshape(d, NSA_GROUPS, 3 * NSA_HPG), ((0, 0), (0, 0), (0, 16 - 3 * NSA_HPG)))
    gn = jnp.pad(gn.reshape(d, NSA_GROUPS * 16), ((0, 0), (0, LANES - NSA_GROUPS * 16)))
    return jnp.concatenate([w_in[:, c3:], w_in[:, :c0], w_in[:, c0:c1], w_in[:, c2:c3], gn], axis=1).astype(BF16)


def _overlap_t(seq):
    ncp = seq // CMP_STRIDE
    nc = (seq - CMP_LEN) // CMP_STRIDE + 1
    ns = seq // SEL_BLOCK
    c_start = jnp.arange(ncp) * CMP_STRIDE
    j_start = jnp.arange(ns) * SEL_BLOCK
    ov = jnp.clip(jnp.minimum(c_start[None, :] + CMP_LEN, j_start[:, None] + SEL_BLOCK)
                  - jnp.maximum(c_start[None, :], j_start[:, None]), 0).astype(F32) / CMP_LEN
    return jnp.where(jnp.arange(ncp)[None, :] < nc, ov, 0.0).astype(BF16)


def _token_mixers(x, norm1_g, w_in, q_norm_g, k_norm_g, cmp_pos_k, cmp_pos_v, cmp_k_w1, cmp_k_w2,
                  cmp_v_w1, cmp_v_w2, w_branch_nsa, w_branch_sb, w_out):
    batch, seq, d = x.shape
    t = batch * seq
    x2d = x.reshape(t, d)
    scale = 1.0 / math.sqrt(HEAD_DIM)

    proj = _norm_matmul(x2d, norm1_g, _reorder_w_in(w_in), tm=512, tn=N_CB * LANES, name="in_proj")

    cos, sin = _rope_tables(jnp.arange(seq))
    qg = jnp.tile(q_norm_g.astype(F32), 2) * scale
    gammas = jnp.stack([qg] * 4 + [jnp.tile(k_norm_g[1].astype(F32), 2), jnp.tile(k_norm_g[2].astype(F32), 2)])
    qk = _prep_norm_rope(proj, cos, sin, gammas.reshape(6, 1, LANES), seq,
                         ts=min(2048, seq)).reshape(batch, seq, 6 * LANES)
    tposed = _prep_transpose(proj, batch, seq, ts=min(4096, seq))

    ncp = seq // CMP_STRIDE
    cosc, sinc = _rope_tables(jnp.arange(ncp) * CMP_STRIDE + CMP_LEN - 1)
    kc = _compress(proj[:, CB_KC * LANES:(CB_KC + 1) * LANES], cmp_pos_k, cmp_k_w1, cmp_k_w2, batch, seq,
                   key_args=(cosc, sinc, jnp.tile(k_norm_g[0].astype(F32), 2).reshape(1, LANES)))
    vct = _compress(proj[:, CB_VC * LANES:(CB_VC + 1) * LANES], cmp_pos_v, cmp_v_w1, cmp_v_w2, batch, seq)

    ont = _nsa(qk, kc, vct, tposed, _overlap_t(seq), batch, seq)
    ost = _sb(proj.reshape(batch, seq, N_CB * LANES), tposed, batch, seq)
    return _merge(ont, ost, proj, x2d, w_branch_nsa.astype(BF16), w_branch_sb.astype(BF16),
                  w_out.astype(BF16), batch, seq, tm=1024)


def kernel(x, norm1_g, w_in, q_norm_g, k_norm_g, cmp_pos_k, cmp_pos_v, cmp_k_w1, cmp_k_w2, cmp_v_w1, cmp_v_w2,
           w_branch_nsa, w_branch_sb, w_out, norm2_g, peer_w_query, peer_subkeys, peer_u, peer_v):
    batch, seq, d = x.shape
    h = x.astype(F32)
    for l in range(norm1_g.shape[0]):
        h2d = _token_mixers(h, norm1_g[l], w_in[l], q_norm_g[l], k_norm_g[l], cmp_pos_k[l], cmp_pos_v[l],
                            cmp_k_w1[l], cmp_k_w2[l], cmp_v_w1[l], cmp_v_w2[l],
                            w_branch_nsa[l], w_branch_sb[l], w_out[l])
        qp = _norm_matmul(h2d, norm2_g[l], peer_w_query[l].astype(BF16), tm=1024, tn=2048, name="peer_query")
        tables = _route(qp, peer_subkeys[l])
        h2d = _peer(h2d, norm2_g[l], peer_u[l].astype(BF16), peer_v[l].T.astype(BF16), tables)
        h = h2d.reshape(batch, seq, d)
    return h.astype(x.dtype)
```

```python
import functools
import math

import jax
import jax.numpy as jnp
from jax import lax
from jax.experimental import pallas as pl
from jax.experimental.pallas import tpu as pltpu

F32 = jnp.float32
BF16 = jnp.bfloat16

HEAD_DIM = 64
NSA_HEADS = 8
NSA_GROUPS = 2
NSA_HPG = NSA_HEADS // NSA_GROUPS
CMP_LEN = 32
CMP_STRIDE = 16
CMP_HIDDEN = 256
SEL_BLOCK = 64
SEL_TOPN = 16
WINDOW = 512
SB_HEADS = 8
ROPE_THETA = 10000.0
PEER_HEADS = 8
PEER_NKEYS = 128
PEER_QDIM = 256
PEER_TOPK = 16
EPS = 1e-6
NEG = -1e30

LANES = 128
VMEM_LIMIT = 56 * 1024 * 1024

CB_GATE = 0
CB_Q = 16
CB_KC, CB_VC, CB_KS, CB_VS, CB_KW, CB_VW = 20, 21, 22, 23, 24, 25
CB_QSB, CB_KSB, CB_VSB = 26, 30, 34
CB_GN = 38
N_CB = 39

SB_EXIT = 104.0


def _cparams(sem, vmem=None):
    return pltpu.CompilerParams(dimension_semantics=sem, vmem_limit_bytes=vmem or VMEM_LIMIT)


def _norm_matmul_kernel(x_ref, g_ref, w_ref, o_ref):
    x = x_ref[...]
    ms = jnp.mean(x * x, axis=-1, keepdims=True)
    xn = (x * lax.rsqrt(ms + EPS) * g_ref[...]).astype(BF16)
    o_ref[...] = jnp.dot(xn, w_ref[...], preferred_element_type=F32)


def _norm_matmul(x2d, gamma, w_bf16, tm, tn, name):
    t, d = x2d.shape
    n = w_bf16.shape[1]
    return pl.pallas_call(
        _norm_matmul_kernel,
        out_shape=jax.ShapeDtypeStruct((t, n), F32),
        grid=(n // tn, t // tm),
        in_specs=[
            pl.BlockSpec((tm, d), lambda j, i: (i, 0)),
            pl.BlockSpec((1, d), lambda j, i: (0, 0)),
            pl.BlockSpec((d, tn), lambda j, i: (0, j)),
        ],
        out_specs=pl.BlockSpec((tm, tn), lambda j, i: (i, j)),
        compiler_params=_cparams(("parallel", "parallel")),
        name=name,
    )(x2d, gamma.reshape(1, d), w_bf16)


def _group_mean_sq(x):
    ss = x * x
    r = lax.broadcasted_iota(jnp.int32, (LANES, LANES), 0) // HEAD_DIM
    c = lax.broadcasted_iota(jnp.int32, (LANES, LANES), 1) // HEAD_DIM
    ones_bd = jnp.where(r == c, 1.0, 0.0).astype(BF16)
    hi = ss.astype(BF16)
    lo = (ss - hi.astype(F32)).astype(BF16)
    tot = (jnp.dot(hi, ones_bd, preferred_element_type=F32)
           + jnp.dot(lo, ones_bd, preferred_element_type=F32))
    return tot * (1.0 / HEAD_DIM)


def _norm_rope(x, cos, sin_signed, gamma):
    xn = x * lax.rsqrt(_group_mean_sq(x) + EPS) * gamma
    lane = lax.broadcasted_iota(jnp.int32, xn.shape, 1)
    first_half = (lane % HEAD_DIM) < (HEAD_DIM // 2)
    swapped = jnp.where(first_half, pltpu.roll(xn, LANES - HEAD_DIM // 2, 1), pltpu.roll(xn, HEAD_DIM // 2, 1))
    return xn * cos + swapped * sin_signed


def _rope_tables(pos):
    half = HEAD_DIM // 2
    d = jnp.arange(LANES) % HEAD_DIM
    freqs = ROPE_THETA ** (-(d % half).astype(F32) / half)
    ang = pos.astype(F32)[:, None] * freqs[None, :]
    sign = jnp.where(d < half, -1.0, 1.0).astype(F32)
    return jnp.cos(ang), jnp.sin(ang) * sign[None, :]


def _norm_rope_kernel(x_ref, cos_ref, sin_ref, g_ref, o_ref):
    o_ref[...] = _norm_rope(x_ref[...], cos_ref[...], sin_ref[...], g_ref[0]).astype(o_ref.dtype)


def _prep_norm_rope(proj, cos, sin, gammas, seq, ts):
    t = proj.shape[0]
    n_slab = gammas.shape[0]
    s_tiles = seq // ts
    return pl.pallas_call(
        _norm_rope_kernel,
        out_shape=jax.ShapeDtypeStruct((t, n_slab * LANES), BF16),
        grid=(t // ts, n_slab),
        in_specs=[
            pl.BlockSpec((ts, LANES), lambda i, j: (i, CB_Q + j + 2 * (j // 4) + (j // 5))),
            pl.BlockSpec((ts, LANES), lambda i, j: (i % s_tiles, 0)),
            pl.BlockSpec((ts, LANES), lambda i, j: (i % s_tiles, 0)),
            pl.BlockSpec((1, 1, LANES), lambda i, j: (j, 0, 0)),
        ],
        out_specs=pl.BlockSpec((ts, LANES), lambda i, j: (i, j)),
        compiler_params=_cparams(("parallel", "parallel")),
        name="prep_norm_rope",
    )(proj, cos, sin, gammas)


T_SLABS = (CB_VS, CB_VW, CB_QSB, CB_QSB + 1, CB_QSB + 2, CB_QSB + 3,
           CB_VSB, CB_VSB + 1, CB_VSB + 2, CB_VSB + 3, CB_GN)
TS_VS, TS_VW, TS_QSB, TS_VSB, TS_GN = 0, 1, 2, 6, 10


def _transpose_kernel(cm_ref, x_ref, o_ref):
    del cm_ref
    o_ref[0] = x_ref[...].T.astype(o_ref.dtype)


def _prep_transpose(proj, batch, seq, ts):
    s_tiles = seq // ts
    colmap = jnp.asarray(T_SLABS, jnp.int32)
    return pl.pallas_call(
        _transpose_kernel,
        out_shape=jax.ShapeDtypeStruct((batch, len(T_SLABS) * LANES, seq), BF16),
        grid_spec=pltpu.PrefetchScalarGridSpec(
            num_scalar_prefetch=1,
            grid=(batch, s_tiles, len(T_SLABS)),
            in_specs=[pl.BlockSpec((ts, LANES), lambda b, i, j, cm: (b * s_tiles + i, cm[j]))],
            out_specs=pl.BlockSpec((1, LANES, ts), lambda b, i, j, cm: (b, j, i)),
        ),
        compiler_params=_cparams(("parallel", "parallel", "parallel")),
        name="prep_transpose",
    )(colmap, proj)


def _compress_kernel(a_ref, pa_ref, pb_ref, w1a_ref, w1b_ref, w2_ref, *rest, is_key):
    a = a_ref[0]
    ncp = a.shape[0]
    p = jnp.dot((a + pa_ref[...]).astype(BF16), w1a_ref[...], preferred_element_type=F32)
    q = jnp.dot((a + pb_ref[...]).astype(BF16), w1b_ref[...], preferred_element_type=F32)
    hid = jax.nn.gelu(p + pltpu.roll(q, ncp - 1, 0))
    if is_key:
        cos_ref, sin_ref, g_ref, o_ref = rest
        out = jnp.dot(hid.astype(BF16), w2_ref[...], preferred_element_type=F32)
        o_ref[0] = _norm_rope(out, cos_ref[...], sin_ref[...], g_ref[...]).astype(o_ref.dtype)
    else:
        (o_ref,) = rest
        nt_dims = (((1,), (1,)), ((), ()))
        out_t = lax.dot_general(w2_ref[...], hid.astype(BF16), nt_dims, preferred_element_type=F32)
        o_ref[0] = out_t.astype(o_ref.dtype)


def _expand_cmp_weights(w1, w2):
    eye = jnp.eye(NSA_GROUPS, dtype=F32)
    w1r = w1.reshape(2, CMP_STRIDE, HEAD_DIM, CMP_HIDDEN)
    ex = w1r[:, :, None, :, None, :] * eye[None, None, :, None, :, None]
    ex = ex.reshape(2, CMP_STRIDE * LANES, NSA_GROUPS * CMP_HIDDEN).astype(BF16)
    w2x = (w2[None, :, None, :] * eye[:, None, :, None]).reshape(NSA_GROUPS * CMP_HIDDEN, LANES).astype(BF16)
    return ex[0], ex[1], w2x


def _compress(tok_slab, pos_emb, w1, w2, batch, seq, key_args=None):
    ncp = seq // CMP_STRIDE
    a = tok_slab.reshape(batch, ncp, CMP_STRIDE * LANES)
    pos2 = jnp.tile(pos_emb.astype(F32)[:, None, :], (1, NSA_GROUPS, 1)).reshape(2, 1, CMP_STRIDE * LANES)
    w1a, w1b, w2x = _expand_cmp_weights(w1, w2)
    kdim = CMP_STRIDE * LANES
    hdim = NSA_GROUPS * CMP_HIDDEN
    if key_args is None:
        w2x = w2x.T
    in_specs = [
        pl.BlockSpec((1, ncp, kdim), lambda b: (b, 0, 0)),
        pl.BlockSpec((1, kdim), lambda b: (0, 0)),
        pl.BlockSpec((1, kdim), lambda b: (0, 0)),
        pl.BlockSpec((kdim, hdim), lambda b: (0, 0)),
        pl.BlockSpec((kdim, hdim), lambda b: (0, 0)),
        pl.BlockSpec(w2x.shape, lambda b: (0, 0)),
    ]
    args = [a, pos2[0], pos2[1], w1a, w1b, w2x]
    if key_args is not None:
        cos, sin, gamma = key_args
        in_specs += [pl.BlockSpec((ncp, LANES), lambda b: (0, 0)),
                     pl.BlockSpec((ncp, LANES), lambda b: (0, 0)),
                     pl.BlockSpec((1, LANES), lambda b: (0, 0))]
        args += [cos, sin, gamma]
        out_shape = jax.ShapeDtypeStruct((batch, ncp, LANES), BF16)
        out_spec = pl.BlockSpec((1, ncp, LANES), lambda b: (b, 0, 0))
    else:
        out_shape = jax.ShapeDtypeStruct((batch, LANES, ncp), BF16)
        out_spec = pl.BlockSpec((1, LANES, ncp), lambda b: (b, 0, 0))
    return pl.pallas_call(
        functools.partial(_compress_kernel, is_key=key_args is not None),
        out_shape=out_shape,
        grid=(batch,),
        in_specs=in_specs,
        out_specs=out_spec,
        compiler_params=_cparams(("parallel",)),
        name="compress_k" if key_args is not None else "compress_v",
    )(*args)


NSA_TQ = 256
NSA_KB = 1024


def _flash_step(k_tile, vt_tile, qtm, bias, carry):
    m, l, acc = carry
    s = jnp.dot(k_tile, qtm, preferred_element_type=F32) + bias
    m_new = jnp.maximum(m, jnp.max(s, axis=0, keepdims=True))
    alpha = jnp.exp(m - m_new)
    p = jnp.exp(s - m_new).astype(BF16)
    vt_ones = jnp.concatenate([vt_tile, jnp.ones((16, vt_tile.shape[1]), BF16)], axis=0)
    pv = jnp.dot(vt_ones, p, preferred_element_type=F32)
    l = alpha * l + pv[HEAD_DIM:HEAD_DIM + 1]
    acc = alpha * acc + pv[:HEAD_DIM]
    return m_new, l, acc


def _nsa_kernel(q_ref, kc_ref, vct_ref, ks_ref, kw_ref, vst_ref, vwt_ref, gate_ref, ovt_ref,
                o_ref, selb_ref, *, n_sel):
    g = pl.program_id(1)
    i = pl.program_id(2)
    tq, kb, hpg = NSA_TQ, NSA_KB, NSA_HPG
    n = hpg * tq
    t0 = i * tq
    grow = pl.multiple_of(g * HEAD_DIM, HEAD_DIM)

    qt = q_ref[0].astype(F32).T
    rowgrp = lax.broadcasted_iota(jnp.int32, (LANES, tq), 0) // HEAD_DIM
    parts = []
    for h in range(hpg):
        blk = qt[h * HEAD_DIM:(h + 1) * HEAD_DIM]
        parts.append(jnp.where(rowgrp == g, jnp.concatenate([blk, blk], axis=0), 0.0))
    qtm = jnp.concatenate(parts, axis=1).astype(BF16)

    tok1 = t0 + lax.broadcasted_iota(jnp.int32, (1, tq), 1)
    tokn = jnp.concatenate([tok1] * hpg, axis=1)

    ncp = kc_ref.shape[1]
    sc = jnp.dot(kc_ref[0], qtm, preferred_element_type=F32)
    cend = lax.broadcasted_iota(jnp.int32, (ncp, n), 0) * CMP_STRIDE + (CMP_LEN - 1)
    sc = jnp.where(cend <= tokn, sc, NEG)
    mc = jnp.max(sc, axis=0, keepdims=True)
    ec = jnp.exp(sc - mc)
    has_c = jnp.where(tokn >= CMP_LEN - 1, 1.0, 0.0)
    pc = ec * (has_c / jnp.sum(ec, axis=0, keepdims=True))
    o_cmp = jnp.dot(vct_ref[0, pl.ds(grow, HEAD_DIM), :], pc.astype(BF16), preferred_element_type=F32)

    psum = pc[:, 0:tq]
    for h in range(1, hpg):
        psum = psum + pc[:, h * tq:(h + 1) * tq]
    p_hi = psum.astype(BF16)
    p_lo = (psum - p_hi.astype(F32)).astype(BF16)
    imp = (jnp.dot(ovt_ref[...], p_hi, preferred_element_type=F32)
           + jnp.dot(ovt_ref[...], p_lo, preferred_element_type=F32))
    ns = imp.shape[0]
    jidx = lax.broadcasted_iota(jnp.int32, (ns, tq), 0)
    jf = jidx.astype(F32)
    tokb = jnp.broadcast_to(tok1, (ns, tq))
    cur = tokb // SEL_BLOCK
    forced = (jidx == 0) | (jidx == cur) | (jidx == cur - 1)
    valid = jidx * SEL_BLOCK <= tokb
    score = jnp.where(forced, 1e4, jnp.where(valid, imp, -1e4))
    selb = jnp.full((ns, tq), NEG, F32)
    for _ in range(n_sel):
        mx = jnp.max(score, axis=0, keepdims=True)
        first = jnp.min(jnp.where(score == mx, jf, float(ns)), axis=0, keepdims=True)
        pick = jf == first
        selb = jnp.where(pick, 0.0, selb)
        score = jnp.where(pick, -jnp.inf, score)
    selb_ref[...] = selb

    wlen = WINDOW + tq
    w0 = pl.multiple_of(jnp.maximum(t0 - WINDOW, 0), tq)
    wpos = w0 + lax.broadcasted_iota(jnp.int32, (wlen, tq), 0)
    wbias1 = jnp.where((wpos <= tok1) & (wpos > tok1 - WINDOW), 0.0, NEG)
    sw = (jnp.dot(kw_ref[0, pl.ds(w0, wlen), :], qtm, preferred_element_type=F32)
          + jnp.concatenate([wbias1] * hpg, axis=1))
    pw = jnp.exp(sw - jnp.max(sw, axis=0, keepdims=True)).astype(BF16)
    vw_ones = jnp.concatenate([vwt_ref[0, pl.ds(grow, HEAD_DIM), pl.ds(w0, wlen)],
                               jnp.ones((16, wlen), BF16)], axis=0)
    pvw = jnp.dot(vw_ones, pw, preferred_element_type=F32)
    o_win = pvw[:HEAD_DIM] / pvw[HEAD_DIM:HEAD_DIM + 1]

    krow = lax.broadcasted_iota(jnp.int32, (kb, tq), 0)
    init = (jnp.full((1, n), NEG, F32), jnp.zeros((1, n), F32), jnp.zeros((HEAD_DIM, n), F32))
    nblk = kb // SEL_BLOCK
    def sel_body(k, carry):
        k0 = pl.multiple_of(k * kb, kb)
        rows = [jnp.broadcast_to(selb_ref[pl.ds(k * nblk + r, 1), :], (SEL_BLOCK, tq)) for r in range(nblk)]
        bias1 = jnp.where(k0 + krow <= tok1, jnp.concatenate(rows, axis=0), NEG)
        bias = jnp.concatenate([bias1] * hpg, axis=1)
        return _flash_step(ks_ref[0, pl.ds(k0, kb), :], vst_ref[0, pl.ds(grow, HEAD_DIM), pl.ds(k0, kb)],
                           qtm, bias, carry)
    _, l_s, acc_s = lax.fori_loop(0, t0 // kb + 1, sel_body, init)

    gt = jax.nn.sigmoid(gate_ref[0, pl.ds(pl.multiple_of(g * 16, 16), 16), :].astype(F32))
    o_sel = acc_s / l_s
    for h in range(hpg):
        sl = slice(h * tq, (h + 1) * tq)
        o_h = (gt[h:h + 1] * o_cmp[:, sl] + gt[hpg + h:hpg + h + 1] * o_sel[:, sl]
               + gt[2 * hpg + h:2 * hpg + h + 1] * o_win[:, sl])
        o_ref[0, h * HEAD_DIM:(h + 1) * HEAD_DIM, :] = o_h.astype(o_ref.dtype)


def _nsa(qk, kc, vct, tposed, ovt, batch, seq):
    tq = NSA_TQ
    ncp = seq // CMP_STRIDE
    ns = seq // SEL_BLOCK
    n_sel = min(SEL_TOPN, ns)
    qw = NSA_HPG * HEAD_DIM
    return pl.pallas_call(
        functools.partial(_nsa_kernel, n_sel=n_sel),
        out_shape=jax.ShapeDtypeStruct((batch, NSA_HEADS * HEAD_DIM, seq), BF16),
        grid=(batch, NSA_GROUPS, seq // tq),
        in_specs=[
            pl.BlockSpec((1, tq, qw), lambda b, g, i: (b, i, g)),
            pl.BlockSpec((1, ncp, LANES), lambda b, g, i: (b, 0, 0)),
            pl.BlockSpec((1, LANES, ncp), lambda b, g, i: (b, 0, 0)),
            pl.BlockSpec((1, seq, LANES), lambda b, g, i: (b, 0, 4)),
            pl.BlockSpec((1, seq, LANES), lambda b, g, i: (b, 0, 5)),
            pl.BlockSpec((1, LANES, seq), lambda b, g, i: (b, TS_VS, 0)),
            pl.BlockSpec((1, LANES, seq), lambda b, g, i: (b, TS_VW, 0)),
            pl.BlockSpec((1, LANES, tq), lambda b, g, i: (b, TS_GN, i)),
            pl.BlockSpec((ns, ncp), lambda b, g, i: (0, 0)),
        ],
        out_specs=pl.BlockSpec((1, qw, tq), lambda b, g, i: (b, g, i)),
        scratch_shapes=[pltpu.VMEM((ns, tq), F32)],
        compiler_params=_cparams(("parallel", "parallel", "arbitrary")),
        name="nsa",
    )(qk, kc, vct, qk, qk, tposed, tposed, tposed, ovt)


SB_TQ = 256


SB_SLABS = 2


def _sb_kernel(qt_ref, k_ref, vt_ref, o_ref):
    i = pl.program_id(2)
    tq = SB_TQ
    kb = tq
    nh = 2 * SB_SLABS
    n = nh * tq
    scale = 1.0 / math.sqrt(HEAD_DIM)

    rowgrp = lax.broadcasted_iota(jnp.int32, (LANES, tq), 0) // HEAD_DIM
    qtm = []
    for p in range(SB_SLABS):
        q2 = qt_ref[0, p * LANES:(p + 1) * LANES, :].astype(F32) * scale
        qtm.append(jnp.concatenate([jnp.where(rowgrp == 0, q2, 0.0), jnp.where(rowgrp == 1, q2, 0.0)],
                                   axis=1).astype(BF16))

    r = lax.broadcasted_iota(jnp.int32, (kb, kb), 0)
    c = lax.broadcasted_iota(jnp.int32, (kb, kb), 1)
    later = jnp.where(c > r, 1.0, 0.0).astype(BF16)
    krow = lax.broadcasted_iota(jnp.int32, (kb, n), 0)
    tcol = lax.broadcasted_iota(jnp.int32, (kb, n), 1) % tq

    def body(st):
        k, carry, accs, _ = st
        k0 = pl.multiple_of(k * kb, kb)
        kt = k_ref[0, pl.ds(k0, kb), :].astype(BF16)
        z = jnp.concatenate([jnp.dot(kt[:, p * LANES:(p + 1) * LANES], qtm[p], preferred_element_type=F32)
                             for p in range(SB_SLABS)], axis=1)
        sp = jnp.maximum(z, 0.0) + jnp.log(1.0 + jnp.exp(-jnp.abs(z)))
        mask = (k0 + krow) < (i * tq + tcol)
        spm = jnp.where(mask, sp, 0.0)
        hi = spm.astype(BF16)
        lo = (spm - hi.astype(F32)).astype(BF16)
        after = (jnp.dot(later, hi, preferred_element_type=F32)
                 + jnp.dot(later, lo, preferred_element_type=F32))
        a = jnp.where(mask, jnp.exp(z - sp - after - carry), 0.0).astype(BF16)
        accs = tuple(accs[h] + jnp.dot(vt_ref[0, h * HEAD_DIM:(h + 1) * HEAD_DIM, pl.ds(k0, kb)],
                                       a[:, h * tq:(h + 1) * tq], preferred_element_type=F32)
                     for h in range(nh))
        carry = carry + after[0:1] + spm[0:1]
        go = jnp.logical_and(k > 0, jnp.min(carry) <= SB_EXIT)
        return k - 1, carry, accs, go

    zacc = jnp.zeros((HEAD_DIM, tq), F32)
    st = (i, jnp.zeros((1, n), F32), (zacc,) * nh, i >= 0)
    _, _, accs, _ = lax.while_loop(lambda s: s[3], body, st)
    for h in range(nh):
        o_ref[0, h * HEAD_DIM:(h + 1) * HEAD_DIM, :] = accs[h].astype(o_ref.dtype)


def _sb(proj3, tposed, batch, seq):
    tq = SB_TQ
    w = SB_SLABS * LANES
    assert TS_QSB % SB_SLABS == 0 and TS_VSB % SB_SLABS == 0 and CB_KSB % SB_SLABS == 0
    return pl.pallas_call(
        _sb_kernel,
        out_shape=jax.ShapeDtypeStruct((batch, SB_HEADS * HEAD_DIM, seq), BF16),
        grid=(batch, SB_HEADS // (2 * SB_SLABS), seq // tq),
        in_specs=[
            pl.BlockSpec((1, w, tq), lambda b, h, i: (b, TS_QSB // SB_SLABS + h, i)),
            pl.BlockSpec((1, seq, w), lambda b, h, i: (b, 0, CB_KSB // SB_SLABS + h)),
            pl.BlockSpec((1, w, seq), lambda b, h, i: (b, TS_VSB // SB_SLABS + h, 0)),
        ],
        out_specs=pl.BlockSpec((1, w, tq), lambda b, h, i: (b, h, i)),
        compiler_params=_cparams(("parallel", "parallel", "arbitrary")),
        name="stickbreaking",
    )(tposed, proj3, tposed)


def _merge_kernel(ont_ref, ost_ref, ga_ref, gb_ref, x_ref, wa_ref, wb_ref, wo_ref, o_ref):
    tn_dims = (((0,), (0,)), ((), ()))
    a = lax.dot_general(ont_ref[0], wa_ref[...], tn_dims, preferred_element_type=F32)
    b = lax.dot_general(ost_ref[0], wb_ref[...], tn_dims, preferred_element_type=F32)
    merged = jax.nn.sigmoid(ga_ref[...]) * a + jax.nn.sigmoid(gb_ref[...]) * b
    o_ref[...] = x_ref[...] + jnp.dot(merged.astype(BF16), wo_ref[...], preferred_element_type=F32)


def _merge(ont, ost, proj, x2d, wa, wb, wo, batch, seq, tm):
    d = x2d.shape[1]
    s_tiles = seq // tm
    hw = ont.shape[1]
    return pl.pallas_call(
        _merge_kernel,
        out_shape=jax.ShapeDtypeStruct(x2d.shape, F32),
        grid=(batch, s_tiles),
        in_specs=[
            pl.BlockSpec((1, hw, tm), lambda b, i: (b, 0, i)),
            pl.BlockSpec((1, hw, tm), lambda b, i: (b, 0, i)),
            pl.BlockSpec((tm, d), lambda b, i: (b * s_tiles + i, 0)),
            pl.BlockSpec((tm, d), lambda b, i: (b * s_tiles + i, 1)),
            pl.BlockSpec((tm, d), lambda b, i: (b * s_tiles + i, 0)),
            pl.BlockSpec((hw, d), lambda b, i: (0, 0)),
            pl.BlockSpec((hw, d), lambda b, i: (0, 0)),
            pl.BlockSpec((d, d), lambda b, i: (0, 0)),
        ],
        out_specs=pl.BlockSpec((tm, d), lambda b, i: (b * s_tiles + i, 0)),
        compiler_params=_cparams(("parallel", "parallel")),
        name="merge",
    )(ont, ost, proj, proj, x2d, wa, wb, wo)


PEER_TR = 1024


def _topk_ranked(s, idx, k, vals_ref=None, first_ref=None, want_rank=True):
    rows, n = s.shape
    sub = 8
    s = s.reshape(rows // sub, sub, n)
    idx = idx.reshape(rows // sub, sub, n)

    def all_sublanes(x, op):
        for sh in (4, 2, 1):
            x = op(x, pltpu.roll(x, sh, 0))
        return x

    rank = jnp.full(s.shape, float(k), F32) if want_rank else None
    for r in range(k):
        mx = all_sublanes(jnp.max(s, axis=0), jnp.maximum)
        first = all_sublanes(jnp.min(jnp.where(s == mx[None], idx, 1e9), axis=0), jnp.minimum)
        pick = idx == first[None]
        if want_rank:
            rank = jnp.where(pick, float(r), rank)
        s = jnp.where(pick, -jnp.inf, s)
        if vals_ref is not None:
            vals_ref[r:r + 1, :] = mx[0:1]
        if first_ref is not None:
            first_ref[r:r + 1, :] = first[0:1]
    return rank.reshape(rows, n) if want_rank else None


_CAND_GROUPS = ((0, 0, 16), (16, 1, 8), (24, 2, 8), (32, 3, 4), (36, 4, 4), (40, 5, 2), (42, 6, 2), (44, 7, 2))
_CAND_ROWS = 56


def _route_kernel(q_ref, sk_ref, n1_ref, a1_ref, b2_ref, e2_ref, v1_ref, v2_ref, f1_ref, cand_ref):
    k = PEER_TOPK
    half = PEER_QDIM // 2
    nt_dims = (((1,), (1,)), ((), ()))
    qh = q_ref[...].astype(BF16)
    s1 = lax.dot_general(sk_ref[0].astype(BF16), qh[:, :half], nt_dims, preferred_element_type=F32)
    s2 = lax.dot_general(sk_ref[1].astype(BF16), qh[:, half:], nt_dims, preferred_element_type=F32)
    tr = s1.shape[1]
    kidx = lax.broadcasted_iota(jnp.int32, s1.shape, 0).astype(F32)
    _topk_ranked(s1, kidx, k, v1_ref, f1_ref, want_rank=False)
    rank2 = _topk_ranked(s2, kidx, k, v2_ref)
    v1 = v1_ref[...]
    v2 = v2_ref[...]

    flat = lax.broadcasted_iota(jnp.int32, (_CAND_ROWS, tr), 0)
    row = flat
    for r0, a, nb in _CAND_GROUPS:
        cand_ref[r0:r0 + nb, :] = v1[a:a + 1] + v2[0:nb]
        flat = jnp.where((row >= r0) & (row < r0 + nb), row - r0 + a * k, flat)
    cand_ref[46:48, :] = jnp.full((2, tr), -jnp.inf, F32)
    cand_ref[48:56, :] = v1[8:16] + v2[0:1]
    flat = jnp.where(row >= 48, (row - 40) * k, flat).astype(F32)
    cand = cand_ref[...]
    crank = _topk_ranked(cand, flat, k)
    selc = jnp.where(crank < float(k), 1.0, 0.0)
    mx = v1[0:1] + v2[0:1]
    z = jnp.sum(jnp.where(crank < float(k), jnp.exp(cand - mx), 0.0), axis=0, keepdims=True)
    n_a = [jnp.sum(selc[r0:r0 + nb], axis=0, keepdims=True) for r0, _, nb in _CAND_GROUPS]
    n_a += [selc[48 + a - 8:48 + a - 7] for a in range(8, k)]

    f1 = f1_ref[...]
    n1 = jnp.zeros_like(s1)
    for a in range(k):
        n1 = jnp.where(kidx == f1[a:a + 1], n_a[a], n1)
    n1_ref[0] = n1
    a1_ref[0] = jnp.exp(s1 - v1[0:1]) / z
    b2_ref[0] = rank2.astype(b2_ref.dtype)
    e2_ref[0] = jnp.exp(s2 - v2[0:1]).astype(e2_ref.dtype)


def _route(qp, subkeys):
    t = qp.shape[0]
    tr = PEER_TR
    nk = PEER_NKEYS
    shp = jax.ShapeDtypeStruct((PEER_HEADS, nk, t), F32)
    shp16 = jax.ShapeDtypeStruct((PEER_HEADS, nk, t), BF16)
    spec = pl.BlockSpec((1, nk, tr), lambda i, h: (h, 0, i))
    return pl.pallas_call(
        _route_kernel,
        out_shape=(shp, shp, shp16, shp16),
        grid=(t // tr, PEER_HEADS),
        in_specs=[
            pl.BlockSpec((tr, PEER_QDIM), lambda i, h: (i, h)),
            pl.BlockSpec((2, nk, PEER_QDIM // 2), lambda i, h: (0, 0, 0)),
        ],
        out_specs=(spec, spec, spec, spec),
        scratch_shapes=[pltpu.VMEM((PEER_TOPK, tr), F32), pltpu.VMEM((PEER_TOPK, tr), F32),
                        pltpu.VMEM((PEER_TOPK, tr), F32), pltpu.VMEM((_CAND_ROWS, tr), F32)],
        compiler_params=_cparams(("parallel", "parallel")),
        name="peer_route",
    )(qp, subkeys)


PEER_TM = 512
PEER_NB = 16


def _peer_kernel(h_ref, g_ref, u_ref, vt_ref, n1_ref, a1_ref, b2_ref, e2_ref, o_ref, hnt_ref, acc_ref):
    kk = pl.program_id(1)
    nk = PEER_NKEYS

    @pl.when(kk == 0)
    def _():
        h = h_ref[...]
        ms = jnp.mean(h * h, axis=-1, keepdims=True)
        hnt_ref[...] = (h * lax.rsqrt(ms + EPS) * g_ref[...]).T.astype(BF16)
        acc_ref[...] = jnp.zeros_like(acc_ref)

    act = jax.nn.gelu(jnp.dot(u_ref[...], hnt_ref[...], preferred_element_type=F32)).astype(BF16)
    tm = act.shape[1]
    rep = nk // 16

    def bcast_row(ref, hd, i1):
        row16 = jnp.broadcast_to(ref[hd, pl.ds(i1, 1), :], (16, tm)).astype(BF16)
        return jnp.concatenate([row16] * rep, axis=0)

    parts = []
    for j in range(PEER_NB):
        i1 = kk * PEER_NB + j
        w = None
        for hd in range(PEER_HEADS):
            n1 = bcast_row(n1_ref, hd, i1)
            a1 = bcast_row(a1_ref, hd, i1)
            w_hd = jnp.where(b2_ref[hd] < n1, e2_ref[hd] * a1, jnp.zeros((), BF16))
            w = w_hd if w is None else w + w_hd
        parts.append(w * act[j * nk:(j + 1) * nk])
    mt = jnp.concatenate(parts, axis=0)
    acc_ref[...] += jnp.dot(vt_ref[...], mt, preferred_element_type=F32)

    @pl.when(kk == pl.num_programs(1) - 1)
    def _():
        o_ref[...] = h_ref[...] + acc_ref[...].T


def _peer(h2d, gamma, u_bf16, vt_bf16, tables):
    t, d = h2d.shape
    tm = min(PEER_TM, t)
    nk = PEER_NKEYS
    ke = PEER_NB * nk
    tspec = pl.BlockSpec((PEER_HEADS, nk, tm), lambda i, k: (0, 0, i))
    return pl.pallas_call(
        _peer_kernel,
        out_shape=jax.ShapeDtypeStruct((t, d), F32),
        grid=(t // tm, nk // PEER_NB),
        in_specs=[
            pl.BlockSpec((tm, d), lambda i, k: (i, 0)),
            pl.BlockSpec((1, d), lambda i, k: (0, 0)),
            pl.BlockSpec((ke, d), lambda i, k: (k, 0)),
            pl.BlockSpec((d, ke), lambda i, k: (0, k)),
            tspec, tspec, tspec, tspec,
        ],
        out_specs=pl.BlockSpec((tm, d), lambda i, k: (i, 0)),
        scratch_shapes=[pltpu.VMEM((d, tm), BF16), pltpu.VMEM((d, tm), F32)],
        compiler_params=_cparams(("parallel", "arbitrary")),
        name="peer_experts",
    )(h2d, gamma.reshape(1, d), u_bf16, vt_bf16, *tables)


def _reorder_w_in(w_in):
    d = w_in.shape[0]
    nsa_w = NSA_HEADS * HEAD_DIM
    kv_w = NSA_GROUPS * HEAD_DIM
    sb_w = SB_HEADS * HEAD_DIM
    c0 = nsa_w
    c1 = c0 + 6 * kv_w
    c2 = c1 + 3 * NSA_HEADS
    c3 = c2 + 3 * sb_w
    gn = w_in[:, c1:c2].reshape(d, 3, NSA_GROUPS, NSA_HPG).transpose(0, 2, 1, 3)
    gn = jnp.pad(gn.reshape(d, NSA_GROUPS, 3 * NSA_HPG), ((0, 0), (0, 0), (0, 16 - 3 * NSA_HPG)))
    gn = jnp.pad(gn.reshape(d, NSA_GROUPS * 16), ((0, 0), (0, LANES - NSA_GROUPS * 16)))
    return jnp.concatenate([w_in[:, c3:], w_in[:, :c0], w_in[:, c0:c1], w_in[:, c2:c3], gn], axis=1).astype(BF16)


def _overlap_t(seq):
    ncp = seq // CMP_STRIDE
    nc = (seq - CMP_LEN) // CMP_STRIDE + 1
    ns = seq // SEL_BLOCK
    c_start = jnp.arange(ncp) * CMP_STRIDE
    j_start = jnp.arange(ns) * SEL_BLOCK
    ov = jnp.clip(jnp.minimum(c_start[None, :] + CMP_LEN, j_start[:, None] + SEL_BLOCK)
                  - jnp.maximum(c_start[None, :], j_start[:, None]), 0).astype(F32) / CMP_LEN
    return jnp.where(jnp.arange(ncp)[None, :] < nc, ov, 0.0).astype(BF16)


def _token_mixers(x, norm1_g, w_in, q_norm_g, k_norm_g, cmp_pos_k, cmp_pos_v, cmp_k_w1, cmp_k_w2,
                  cmp_v_w1, cmp_v_w2, w_branch_nsa, w_branch_sb, w_out):
    batch, seq, d = x.shape
    t = batch * seq
    x2d = x.reshape(t, d)
    scale = 1.0 / math.sqrt(HEAD_DIM)

    proj = _norm_matmul(x2d, norm1_g, _reorder_w_in(w_in), tm=512, tn=N_CB * LANES, name="in_proj")

    cos, sin = _rope_tables(jnp.arange(seq))
    qg = jnp.tile(q_norm_g.astype(F32), 2) * scale
    gammas = jnp.stack([qg] * 4 + [jnp.tile(k_norm_g[1].astype(F32), 2), jnp.tile(k_norm_g[2].astype(F32), 2)])
    qk = _prep_norm_rope(proj, cos, sin, gammas.reshape(6, 1, LANES), seq,
                         ts=min(4096, seq)).reshape(batch, seq, 6 * LANES)
    tposed = _prep_transpose(proj, batch, seq, ts=min(8192, seq))

    ncp = seq // CMP_STRIDE
    cosc, sinc = _rope_tables(jnp.arange(ncp) * CMP_STRIDE + CMP_LEN - 1)
    kc = _compress(proj[:, CB_KC * LANES:(CB_KC + 1) * LANES], cmp_pos_k, cmp_k_w1, cmp_k_w2, batch, seq,
                   key_args=(cosc, sinc, jnp.tile(k_norm_g[0].astype(F32), 2).reshape(1, LANES)))
    vct = _compress(proj[:, CB_VC * LANES:(CB_VC + 1) * LANES], cmp_pos_v, cmp_v_w1, cmp_v_w2, batch, seq)

    ont = _nsa(qk, kc, vct, tposed, _overlap_t(seq), batch, seq)
    ost = _sb(proj.reshape(batch, seq, N_CB * LANES), tposed, batch, seq)
    return _merge(ont, ost, proj, x2d, w_branch_nsa.astype(BF16), w_branch_sb.astype(BF16),
                  w_out.astype(BF16), batch, seq, tm=1024)


def kernel(x, norm1_g, w_in, q_norm_g, k_norm_g, cmp_pos_k, cmp_pos_v, cmp_k_w1, cmp_k_w2, cmp_v_w1, cmp_v_w2,
           w_branch_nsa, w_branch_sb, w_out, norm2_g, peer_w_query, peer_subkeys, peer_u, peer_v):
    batch, seq, d = x.shape
    h = x.astype(F32)
    for l in range(norm1_g.shape[0]):
        h2d = _token_mixers(h, norm1_g[l], w_in[l], q_norm_g[l], k_norm_g[l], cmp_pos_k[l], cmp_pos_v[l],
                            cmp_k_w1[l], cmp_k_w2[l], cmp_v_w1[l], cmp_v_w2[l],
                            w_branch_nsa[l], w_branch_sb[l], w_out[l])
        qp = _norm_matmul(h2d, norm2_g[l], peer_w_query[l].astype(BF16), tm=1024, tn=2048, name="peer_query")
        tables = _route(qp, peer_subkeys[l])
        h2d = _peer(h2d, norm2_g[l], peer_u[l].astype(BF16), peer_v[l].T.astype(BF16), tables)
        h = h2d.reshape(batch, seq, d)
    return h.astype(x.dtype)
```
